```python
import jax, jax.numpy as jnp
from jax import lax
import numpy as np

D_MODEL = 1024
BATCH = 2
SEQ = 8192
DEPTH = 2

GRID_W = 64
CTX_LEN = 256

MIX_HALF = D_MODEL // 2
RET_HEADS = 4
RET_DV = MIX_HALF // RET_HEADS
RET_DK = RET_DV // 2
RET_DECAY_BASE = 5
GLA_HEADS = 4
GLA_DV = MIX_HALF // GLA_HEADS
GLA_DK = GLA_DV // 2
GLA_GATE_RANK = 16
GLA_GATE_TAU = 16.0
SCAN_CHUNK = 64
RET_QK = RET_HEADS * RET_DK
RET_V = RET_HEADS * RET_DV
GLA_QK = GLA_HEADS * GLA_DK
GLA_V = GLA_HEADS * GLA_DV
EVEN_SPLITS = (RET_QK, RET_QK, RET_V, RET_V, GLA_QK, GLA_QK, GLA_V, GLA_V, 2 * GLA_GATE_RANK)
EVEN_PROJ = 2 * RET_QK + 2 * RET_V + 2 * GLA_QK + 2 * GLA_V + 2 * GLA_GATE_RANK

ATT_HEAD_DIM = 64
ATT_Q_HEADS = D_MODEL // ATT_HEAD_DIM
ATT_GROUP = 4
ATT_KV_HEADS = ATT_Q_HEADS // ATT_GROUP
WINDOW = 128
ATT_BLOCK = 128
ATT_QW = ATT_Q_HEADS * ATT_HEAD_DIM
ATT_KVW = ATT_KV_HEADS * ATT_HEAD_DIM
ATT_PROJ = ATT_QW + 2 * ATT_KVW
ROPE_BASE = 10000.0

D_FF = 4 * D_MODEL
DEEPNORM_ALPHA = (2.0 * DEPTH) ** 0.25
DEEPNORM_BETA = (8.0 * DEPTH) ** -0.25
LN_EPS = 1e-5
RMS_EPS = 1e-6
N_EVEN = (DEPTH + 1) // 2
N_ODD = DEPTH // 2

kernel_name = 'hybrid_retention_gla_swa_dit'


def layer_norm(x, g, b):
    xf = x.astype(jnp.float32)
    mu = jnp.mean(xf, axis=-1, keepdims=True)
    var = jnp.mean(jnp.square(xf - mu), axis=-1, keepdims=True)
    return ((xf - mu) * lax.rsqrt(var + LN_EPS) * g + b).astype(x.dtype)


def rms_norm(x, g=None):
    xf = x.astype(jnp.float32)
    y = xf * lax.rsqrt(jnp.mean(jnp.square(xf), axis=-1, keepdims=True) + RMS_EPS)
    if g is not None:
        y = y * g
    return y.astype(x.dtype)


def modulation(cvec, w, b):
    m = jax.nn.silu(cvec) @ w + b
    return jnp.split(m[..., None, :], 6, axis=-1)


def sublayer_residual(x, y, gate, g, b):
    return layer_norm(DEEPNORM_ALPHA * x + gate * y, g, b)


def split_cols(t, sizes):
    idx = np.cumsum(sizes)[:-1].tolist()
    return jnp.split(t, idx, axis=-1)


def split_heads(t, h):
    b, n, _ = t.shape
    return t.reshape(b, n, h, -1).transpose(0, 2, 1, 3)


def merge_heads(t):
    b, h, n, d = t.shape
    return t.transpose(0, 2, 1, 3).reshape(b, n, h * d)


def sq_relu_mlp(h, w1, w2):
    return jnp.square(jax.nn.relu(h @ w1)) @ w2


def chunk_gated_scan(q, k, v, log_a, s0, strict):
    f32 = jnp.float32
    B, H, T, dk = q.shape
    dv = v.shape[-1]
    n = T // SCAN_CHUNK
    qc = q.astype(f32).reshape(B, H, n, SCAN_CHUNK, dk)
    kc = k.astype(f32).reshape(B, H, n, SCAN_CHUNK, dk)
    vc = v.astype(f32).reshape(B, H, n, SCAN_CHUNK, dv)
    bcum = jnp.cumsum(log_a.astype(f32).reshape(B, H, n, SCAN_CHUNK, dk), axis=3)
    b_last = bcum[:, :, :, -1:, :]
    q_dec = qc * jnp.exp(bcum)
    k_inv = kc * jnp.exp(-bcum)
    k_end = kc * jnp.exp(b_last - bcum)
    mask = jnp.tril(jnp.ones((SCAN_CHUNK, SCAN_CHUNK), bool), k=-1 if strict else 0)
    scores = jnp.einsum('bhncd,bhnsd->bhncs', q_dec, k_inv)
    o_intra = jnp.einsum('bhncs,bhnse->bhnce', jnp.where(mask, scores, 0.0), vc)
    kv = jnp.einsum('bhnsd,bhnse->bhnde', k_end, vc)
    chunk_decay = jnp.exp(b_last[:, :, :, 0, :])

    def step(s, inp):
        kv_n, dec_n = inp
        return dec_n[..., None] * s + kv_n, s

    s_final, s_prev = lax.scan(step, s0.astype(f32),
                               (jnp.moveaxis(kv, 2, 0), jnp.moveaxis(chunk_decay, 2, 0)))
    s_prev = jnp.moveaxis(s_prev, 0, 2)
    o_inter = jnp.einsum('bhncd,bhnde->bhnce', q_dec, s_prev)
    o = (o_intra + o_inter).reshape(B, H, T, dv)
    return o.astype(v.dtype), s_final


def bidir_scan(q, k, v, log_a_f, log_a_b, s0_f, s0_b):
    flip = lambda t: jnp.flip(t, axis=2)
    o_f, s_f = chunk_gated_scan(q, k, v, log_a_f, s0_f, strict=False)
    o_b, s_b = chunk_gated_scan(flip(q), flip(k), flip(v), flip(log_a_b), s0_b, strict=True)
    return o_f + flip(o_b), s_f, s_b


def retention_gla_mixer(h, w_in, ret_theta, gla_gk_w, gla_gk_b, gla_norm_g, s0):
    f32 = jnp.float32
    qa, ka, va, ga, qb, kb, vb, gb, lr = split_cols(h @ w_in, EVEN_SPLITS)
    qa = split_heads(qa, RET_HEADS)
    ka = split_heads(ka, RET_HEADS) * RET_DK ** -0.5
    va = split_heads(va, RET_HEADS)
    log_gamma = jnp.log1p(-jnp.exp(ret_theta.astype(f32)))
    la_f = jnp.broadcast_to(log_gamma[0][None, :, None, None], qa.shape)
    la_b = jnp.broadcast_to(log_gamma[1][None, :, None, None], qa.shape)
    o_a, ra_f, ra_b = bidir_scan(qa, ka, va, la_f, la_b, s0[0], s0[1])
    y_a = jax.nn.silu(ga) * merge_heads(rms_norm(o_a))
    qb = split_heads(qb, GLA_HEADS) * GLA_DK ** -0.5
    kb = split_heads(kb, GLA_HEADS)
    vb = split_heads(vb, GLA_HEADS)
    lr_f, lr_b = jnp.split(lr, 2, axis=-1)

    def gate(u, w, b):
        z = (u @ w + b).astype(f32)
        return split_heads(jax.nn.log_sigmoid(z) / GLA_GATE_TAU, GLA_HEADS)

    o_b, sb_f, sb_b = bidir_scan(qb, kb, vb, gate(lr_f, gla_gk_w[0], gla_gk_b[0]),
                                 gate(lr_b, gla_gk_w[1], gla_gk_b[1]), s0[2], s0[3])
    y_b = jax.nn.silu(gb) * merge_heads(rms_norm(o_b, gla_norm_g))
    return jnp.concatenate([y_a, y_b], axis=-1), (ra_f, ra_b, sb_f, sb_b)


def axial_rope(t, row, col):
    f32 = jnp.float32
    half = ATT_HEAD_DIM // 2
    inv_freq = ROPE_BASE ** (-jnp.arange(0, half, 2, dtype=f32) / half)

    def rot(u, p):
        ang = p.astype(f32)[:, None] * inv_freq[None, :]
        cos, sin = jnp.cos(ang), jnp.sin(ang)
        u1, u2 = u[..., :half // 2], u[..., half // 2:]
        return jnp.concatenate([u1 * cos - u2 * sin, u1 * sin + u2 * cos], axis=-1)

    return jnp.concatenate([rot(t[..., :half], row), rot(t[..., half:], col)], axis=-1).astype(t.dtype)


def window_attention(q, k, v, k_c, v_c, sink):
    f32 = jnp.float32
    B, Hq, T, dh = q.shape
    nb = T // ATT_BLOCK
    scale = dh ** -0.5
    qb = q.reshape(B, ATT_KV_HEADS, ATT_GROUP, nb, ATT_BLOCK, dh)
    pad = ((0, 0), (0, 0), (ATT_BLOCK, ATT_BLOCK), (0, 0))
    kp = jnp.pad(k, pad).reshape(B, ATT_KV_HEADS, nb + 2, ATT_BLOCK, dh)
    vp = jnp.pad(v, pad).reshape(B, ATT_KV_HEADS, nb + 2, ATT_BLOCK, dh)
    kb = jnp.concatenate([kp[:, :, 0:nb], kp[:, :, 1:nb + 1], kp[:, :, 2:nb + 2]], axis=3)
    vb = jnp.concatenate([vp[:, :, 0:nb], vp[:, :, 1:nb + 1], vp[:, :, 2:nb + 2]], axis=3)
    qi = jnp.arange(ATT_BLOCK)[:, None]
    kj = jnp.arange(3 * ATT_BLOCK)[None, :]
    in_window = jnp.abs(kj - ATT_BLOCK - qi) <= WINDOW
    kpos = (jnp.arange(nb)[:, None] - 1) * ATT_BLOCK + jnp.arange(3 * ATT_BLOCK)[None, :]
    in_range = (kpos >= 0) & (kpos < T)
    valid = in_window[None] & in_range[:, None, :]
    s_loc = jnp.einsum('bkgnqd,bkncd->bkgnqc', qb, kb).astype(f32) * scale
    s_loc = jnp.where(valid, s_loc, -jnp.inf)
    s_ctx = jnp.einsum('bkgnqd,bkld->bkgnql', qb, k_c).astype(f32) * scale
    s_sink = sink.astype(f32).reshape(1, ATT_KV_HEADS, ATT_GROUP, 1, 1, 1)
    m = jnp.maximum(jnp.maximum(jnp.max(s_loc, -1, keepdims=True), jnp.max(s_ctx, -1, keepdims=True)), s_sink)
    e_loc = jnp.exp(s_loc - m)
    e_ctx = jnp.exp(s_ctx - m)
    denom = jnp.sum(e_loc, -1, keepdims=True) + jnp.sum(e_ctx, -1, keepdims=True) + jnp.exp(s_sink - m)
    o = (jnp.einsum('bkgnqc,bkncd->bkgnqd', e_loc, vb.astype(f32))
         + jnp.einsum('bkgnql,bkld->bkgnqd', e_ctx, v_c.astype(f32))) / denom
    return o.reshape(B, Hq, T, dh).astype(q.dtype)


def context_attention(q_c, k_c, v_c, sink):
    f32 = jnp.float32
    B, Hq, L, dh = q_c.shape
    qg = q_c.reshape(B, ATT_KV_HEADS, ATT_GROUP, L, dh)
    s = jnp.einsum('bkgqd,bkld->bkgql', qg, k_c).astype(f32) * dh ** -0.5
    s_sink = jnp.broadcast_to(sink.astype(f32).reshape(1, ATT_KV_HEADS, ATT_GROUP, 1, 1), s.shape[:-1] + (1,))
    p = jax.nn.softmax(jnp.concatenate([s, s_sink], axis=-1), axis=-1)[..., :L]
    return jnp.einsum('bkgql,bkld->bkgqd', p, v_c.astype(f32)).reshape(B, Hq, L, dh).astype(q_c.dtype)


def setup_inputs(seed: int = 0) -> dict:
    key = jax.random.key(seed)
    ks = jax.random.split(key, 19)
    f32 = jnp.float32
    D = D_MODEL

    def nrm(k, shape, fan_in, scale=1.0):
        return jax.random.normal(k, shape, f32) * (scale * fan_in ** -0.5)

    ret_init = -(RET_DECAY_BASE + jnp.arange(RET_HEADS, dtype=f32)) * jnp.log(2.0)
    return {
        'x': jax.random.normal(ks[0], (BATCH, SEQ, D), f32),
        'c': jax.random.normal(ks[1], (BATCH, D), f32),
        'ctx': jax.random.normal(ks[2], (BATCH, CTX_LEN, D), f32),
        'c_ctx': jax.random.normal(ks[3], (D,), f32),
        'w_mod': nrm(ks[4], (DEPTH, D, 6 * D), D),
        'b_mod': 0.02 * jax.random.normal(ks[5], (DEPTH, 6 * D), f32),
        'ln_g': 1.0 + 0.02 * jax.random.normal(ks[6], (DEPTH, 2, D), f32),
        'ln_b': 0.02 * jax.random.normal(ks[7], (DEPTH, 2, D), f32),
        'mlp_w1': nrm(ks[8], (DEPTH, D, D_FF), D),
        'mlp_w2': nrm(ks[9], (DEPTH, D_FF, D), D_FF, DEEPNORM_BETA),
        'ev_w_in': nrm(ks[10], (N_EVEN, D, EVEN_PROJ), D),
        'ev_ret_theta': ret_init + 0.05 * jax.random.normal(ks[11], (N_EVEN, 2, RET_HEADS), f32),
        'ev_gla_gk_w': nrm(ks[12], (N_EVEN, 2, GLA_GATE_RANK, GLA_QK), GLA_GATE_RANK),
        'ev_gla_gk_b': 0.02 * jax.random.normal(ks[13], (N_EVEN, 2, GLA_QK), f32),
        'ev_gla_norm_g': 1.0 + 0.02 * jax.random.normal(ks[14], (N_EVEN, GLA_DV), f32),
        'ev_w_out': nrm(ks[15], (N_EVEN, D, D), D, DEEPNORM_BETA),
        'od_w_qkv': nrm(ks[16], (N_ODD, D, ATT_PROJ), D),
        'od_sink': 0.5 * jax.random.normal(ks[17], (N_ODD, ATT_Q_HEADS), f32),
        'od_w_out': nrm(ks[18], (N_ODD, D, D), D, DEEPNORM_BETA),
    }


def reference(x, c, ctx, c_ctx, w_mod, b_mod, ln_g, ln_b, mlp_w1, mlp_w2,
              ev_w_in, ev_ret_theta, ev_gla_gk_w, ev_gla_gk_b, ev_gla_norm_g, ev_w_out,
              od_w_qkv, od_sink, od_w_out):
    B, T, _ = x.shape
    rows = T // GRID_W
    row = jnp.repeat(jnp.arange(rows), GRID_W)
    col = jnp.tile(jnp.arange(GRID_W), rows)
    xc = ctx
    for i in range(DEPTH):
        last = i == DEPTH - 1
        j = i // 2
        sh1, sc1, g1, sh2, sc2, g2 = modulation(c, w_mod[i], b_mod[i])
        csh1, csc1, cg1, csh2, csc2, cg2 = modulation(c_ctx, w_mod[i], b_mod[i])
        h = x * (1.0 + sc1) + sh1
        hc = xc * (1.0 + csc1) + csh1
        if i % 2 == 0:
            prm = (ev_w_in[j], ev_ret_theta[j], ev_gla_gk_w[j], ev_gla_gk_b[j], ev_gla_norm_g[j])
            s0 = (jnp.zeros((B, RET_HEADS, RET_DK, RET_DV), jnp.float32),
                  jnp.zeros((B, RET_HEADS, RET_DK, RET_DV), jnp.float32),
                  jnp.zeros((B, GLA_HEADS, GLA_DK, GLA_DV), jnp.float32),
                  jnp.zeros((B, GLA_HEADS, GLA_DK, GLA_DV), jnp.float32))
            yc, ctx_states = retention_gla_mixer(hc, *prm, s0)
            y, _ = retention_gla_mixer(h, *prm, ctx_states)
            y = y @ ev_w_out[j]
            if not last:
                yc = yc @ ev_w_out[j]
        else:
            w = od_w_qkv[j]
            q, k, v = split_cols(h @ w, (ATT_QW, ATT_KVW, ATT_KVW))
            q = axial_rope(split_heads(q, ATT_Q_HEADS), row, col)
            k = axial_rope(split_heads(k, ATT_KV_HEADS), row, col)
            v = split_heads(v, ATT_KV_HEADS)
            k_c, v_c = split_cols(hc @ w[:, ATT_QW:], (ATT_KVW, ATT_KVW))
            k_c = split_heads(k_c, ATT_KV_HEADS)
            v_c = split_heads(v_c, ATT_KV_HEADS)
            y = merge_heads(window_attention(q, k, v, k_c, v_c, od_sink[j])) @ od_w_out[j]
            if not last:
                q_c = split_heads(hc @ w[:, :ATT_QW], ATT_Q_HEADS)
                yc = merge_heads(context_attention(q_c, k_c, v_c, od_sink[j])) @ od_w_out[j]
        x = sublayer_residual(x, y, g1, ln_g[i, 0], ln_b[i, 0])
        x = sublayer_residual(x, sq_relu_mlp(x * (1.0 + sc2) + sh2, mlp_w1[i], mlp_w2[i]), g2, ln_g[i, 1], ln_b[i, 1])
        if not last:
            xc = sublayer_residual(xc, yc, cg1, ln_g[i, 0], ln_b[i, 0])
            xc = sublayer_residual(xc, sq_relu_mlp(xc * (1.0 + csc2) + csh2, mlp_w1[i], mlp_w2[i]), cg2, ln_g[i, 1], ln_b[i, 1])
    return x
```

```python
import functools

import jax
import jax.numpy as jnp
from jax import lax
from jax.experimental import pallas as pl
from jax.experimental.pallas import tpu as pltpu

F32 = jnp.float32
BF16 = jnp.bfloat16

D = 1024
B = 2
T = 8192
L = 256
DEPTH = 2
GRID_W = 64
D_FF = 4 * D
HEAD_DV = 128
HEAD_DK = 64
GATE_RANK = 16
GATE_TAU = 16.0
QK_W = 256
V_W = 512
ATT_DH = 64
ATT_QH = 16
ATT_KVH = 4
ATT_GROUP = 4
ATT_KVW = ATT_KVH * ATT_DH
WINDOW = 128
ROPE_BASE = 10000.0
ALPHA = (2.0 * DEPTH) ** 0.25
LN_EPS = 1e-5
RMS_EPS = 1e-6

TM = 512
LAT_ROWS = B * T
ROWS = LAT_ROWS + B * L
N_LAT_TILES = LAT_ROWS // TM
N_TILES = ROWS // TM
TILES_PER_BATCH = T // TM

CH = 64
TB = 256
NCH = TB // CH
NBLK = T // TB
NSTEP = NBLK + 1
CTX_BLK0 = LAT_ROWS // TB

TQ = 512
QB = 128
KWIN = 3 * QB

VMEM_LIMIT = 56 * 1024 * 1024


def _dot(a, b):
    return jnp.dot(a, b, preferred_element_type=F32)


def _dot_nt(a, b):
    return lax.dot_general(a, b, (((1,), (1,)), ((), ())), preferred_element_type=F32)


def _full_spec(shape):
    nd = len(shape)
    return pl.BlockSpec(shape, lambda *_: (0,) * nd, pipeline_mode=pl.Buffered(1))


def _mod_row(t):
    return jnp.minimum(t // TILES_PER_BATCH, B)


def _mod_spec(layer):
    return pl.BlockSpec((None, 1, 6 * D), lambda t: (layer * 8 + _mod_row(t), 0, 0))


def _layer_norm(x, g, b):
    mu = jnp.mean(x, axis=-1, keepdims=True)
    xc = x - mu
    var = jnp.mean(xc * xc, axis=-1, keepdims=True)
    return xc * lax.rsqrt(var + LN_EPS) * g + b


def _log_sigmoid(z):
    return jnp.minimum(z, 0.0) - jnp.log1p(jnp.exp(-jnp.abs(z)))


MOD_TN = 1536


def _mod_kernel(c_ref, w_ref, b_ref, o_ref):
    s = jax.nn.silu(c_ref[...])
    o_ref[...] = jnp.dot(s, w_ref[...], precision=lax.Precision.HIGHEST,
                         preferred_element_type=F32) + b_ref[...]


def _modulation(cs, w_mod, b_mod):
    return pl.pallas_call(
        _mod_kernel,
        grid=(DEPTH, 6 * D // MOD_TN),
        in_specs=[
            pl.BlockSpec((8, D), lambda i, n: (0, 0)),
            pl.BlockSpec((None, D, MOD_TN), lambda i, n: (i, 0, n)),
            pl.BlockSpec((None, 1, MOD_TN), lambda i, n: (i, 0, n)),
        ],
        out_specs=pl.BlockSpec((None, 8, MOD_TN), lambda i, n: (i, 0, n)),
        out_shape=jax.ShapeDtypeStruct((DEPTH, 8, 6 * D), F32),
        compiler_params=pltpu.CompilerParams(
            dimension_semantics=("arbitrary", "arbitrary"), vmem_limit_bytes=VMEM_LIMIT),
        name="modulation",
    )(cs, w_mod, b_mod.reshape(DEPTH, 1, 6 * D))


def _in0_kernel(x_ref, ctx_ref, mod_ref,
                wq_a, wkT_a, wv_a, wg_a, wq_b, wkT_b, wv_b, wg_b, wlr, g2, g2T, gbr, gbc,
                qa_o, kaT_o, va_o, ga_o, qb_o, kbT_o, vb_o, gb_o, lf_o, lb_o, lfT_o, lbT_o):
    t = pl.program_id(0)
    xt = jnp.where(t == N_LAT_TILES, ctx_ref[...], x_ref[...])
    sh1 = mod_ref[:, 0:D]
    sc1 = mod_ref[:, D:2 * D]
    h = (xt * (1.0 + sc1) + sh1).astype(BF16)
    qk_scale = HEAD_DK ** -0.5

    qa_o[...] = _dot(h, wq_a[...]).astype(BF16)
    kaT_o[...] = (_dot_nt(wkT_a[...], h) * qk_scale).astype(BF16)
    va_o[...] = _dot(h, wv_a[...]).astype(BF16)
    ga_o[...] = jax.nn.silu(_dot(h, wg_a[...])).astype(BF16)

    qb_o[...] = (_dot(h, wq_b[...]) * qk_scale).astype(BF16)
    kbT_o[...] = _dot_nt(wkT_b[...], h).astype(BF16)
    vb_o[...] = _dot(h, wv_b[...]).astype(BF16)
    gb_o[...] = jax.nn.silu(_dot(h, wg_b[...])).astype(BF16)

    lr = _dot(h, wlr[...]).astype(BF16)
    ls = _log_sigmoid(_dot(lr, g2[...]) + gbr[...]) * (1.0 / GATE_TAU)
    lf_o[...] = ls[:, 0:QK_W]
    lb_o[...] = ls[:, QK_W:2 * QK_W]
    lsT = _log_sigmoid(_dot_nt(g2T[...], lr) + gbc[...]) * (1.0 / GATE_TAU)
    lfT_o[...] = lsT[0:QK_W, :]
    lbT_o[...] = lsT[QK_W:2 * QK_W, :]


def _in_proj0(x2, ctx2, mods, w):
    row = lambda t: (t, 0)
    col = lambda t: (0, t)
    out_shape = [
        jax.ShapeDtypeStruct((ROWS, QK_W), BF16), jax.ShapeDtypeStruct((QK_W, ROWS), BF16),
        jax.ShapeDtypeStruct((ROWS, V_W), BF16), jax.ShapeDtypeStruct((ROWS, V_W), BF16),
        jax.ShapeDtypeStruct((ROWS, QK_W), BF16), jax.ShapeDtypeStruct((QK_W, ROWS), BF16),
        jax.ShapeDtypeStruct((ROWS, V_W), BF16), jax.ShapeDtypeStruct((ROWS, V_W), BF16),
        jax.ShapeDtypeStruct((ROWS, QK_W), F32), jax.ShapeDtypeStruct((ROWS, QK_W), F32),
        jax.ShapeDtypeStruct((QK_W, ROWS), F32), jax.ShapeDtypeStruct((QK_W, ROWS), F32),
    ]
    out_specs = [
        pl.BlockSpec((TM, QK_W), row), pl.BlockSpec((QK_W, TM), col),
        pl.BlockSpec((TM, V_W), row), pl.BlockSpec((TM, V_W), row),
        pl.BlockSpec((TM, QK_W), row), pl.BlockSpec((QK_W, TM), col),
        pl.BlockSpec((TM, V_W), row), pl.BlockSpec((TM, V_W), row),
        pl.BlockSpec((TM, QK_W), row), pl.BlockSpec((TM, QK_W), row),
        pl.BlockSpec((QK_W, TM), col), pl.BlockSpec((QK_W, TM), col),
    ]
    weights = [w["wq_a"], w["wkT_a"], w["wv_a"], w["wg_a"], w["wq_b"], w["wkT_b"], w["wv_b"], w["wg_b"],
               w["wlr"], w["g2"], w["g2T"], w["gbr"], w["gbc"]]
    in_specs = [
        pl.BlockSpec((TM, D), lambda t: (jnp.minimum(t, N_LAT_TILES - 1), 0)),
        _full_spec((B * L, D)),
        _mod_spec(0),
    ] + [_full_spec(a.shape) for a in weights]
    return pl.pallas_call(
        _in0_kernel,
        grid=(N_TILES,),
        in_specs=in_specs,
        out_specs=out_specs,
        out_shape=out_shape,
        compiler_params=pltpu.CompilerParams(
            dimension_semantics=("arbitrary",), vmem_limit_bytes=VMEM_LIMIT),
        name="in_proj0",
    )(x2, ctx2, mods, *weights)


def _hi_lo(a):
    hi = a.astype(BF16)
    lo = (a - hi.astype(F32)).astype(BF16)
    return hi, lo


def _diag_blocks(kv):
    return jnp.concatenate([kv[0:HEAD_DK, 0:HEAD_DV], kv[HEAD_DK:2 * HEAD_DK, HEAD_DV:2 * HEAD_DV]], axis=0)


def _scan_kernel(*refs, gla):
    if gla:
        (q_ref, kT_ref, v_ref, g_ref, lf_ref, lb_ref, lfT_ref, lbT_ref, ng_ref,
         y_ref, s_ref, r_ref, rst_ref) = refs
    else:
        (q_ref, kT_ref, v_ref, g_ref, thr_ref, thc_ref,
         y_ref, s_ref, r_ref, rst_ref) = refs

    phase = pl.program_id(2)
    j = pl.program_id(3)

    ri = lax.broadcasted_iota(jnp.int32, (CH, CH), 0)
    ci = lax.broadcasted_iota(jnp.int32, (CH, CH), 1)
    tril = (ri >= ci).astype(BF16)
    triu = (ri <= ci).astype(BF16)

    def cumsums_bwd_T(lbT):
        hi, lo = _hi_lo(lbT)
        return _dot(hi, tril) + _dot(lo, tril)

    def cumsums_fwd_T(lfT):
        hi, lo = _hi_lo(lfT)
        return _dot(hi, triu) + _dot(lo, triu)

    def cumsum_fwd(lf):
        hi, lo = _hi_lo(lf)
        return _dot(tril, hi) + _dot(tril, lo)

    def cumsum_bwd(lb):
        hi, lo = _hi_lo(lb)
        return _dot(triu, hi) + _dot(triu, lo)

    if not gla:
        lg_row = jnp.log1p(-jnp.exp(thr_ref[...]))
        lg_col = jnp.log1p(-jnp.exp(thc_ref[...]))
        pos_r = lax.broadcasted_iota(jnp.int32, (CH, 2 * HEAD_DK), 0).astype(F32)
        pos_c = lax.broadcasted_iota(jnp.int32, (2 * HEAD_DK, CH), 1).astype(F32)
        ret_b = (pos_r + 1.0) * lg_row[0:1, :]
        ret_r = (CH - pos_r) * lg_row[1:2, :]
        ret_bT = (pos_c + 1.0) * lg_col[:, 0:1]
        ret_rT = (CH - pos_c) * lg_col[:, 1:2]

    def bwd_tables(rows):
        rT = cumsums_bwd_T(lbT_ref[:, rows]) if gla else ret_rT
        return jnp.exp(-rT), jnp.exp(rT[:, 0:1])

    def fwd_tables(rows):
        bT = cumsums_fwd_T(lfT_ref[:, rows]) if gla else ret_bT
        return jnp.exp(-bT), jnp.exp(bT[:, CH - 1:CH])

    @pl.when(phase == 0)
    def _():
        @pl.when(j == 0)
        def _():
            r_ref[...] = jnp.zeros_like(r_ref)

        jf = jnp.where(j == 0, 0, NSTEP - j)
        r_state = r_ref[...]
        for c in reversed(range(NCH)):
            rows = slice(c * CH, (c + 1) * CH)
            rst_ref[jf * NCH + c] = r_state.astype(BF16)
            kT = kT_ref[:, rows].astype(F32)
            v = v_ref[rows, :]
            erTi, d_r = bwd_tables(rows)
            kbe = (kT * erTi * d_r).astype(BF16)
            r_state = d_r * r_state + _diag_blocks(_dot(kbe, v))
        r_ref[...] = r_state

    @pl.when(phase == 1)
    def _():
        @pl.when(j == 0)
        def _():
            s_ref[...] = jnp.zeros_like(s_ref)

        lane = lax.broadcasted_iota(jnp.int32, (CH, 2 * HEAD_DK), 1)
        head_a = lane < HEAD_DK
        r2 = lax.broadcasted_iota(jnp.int32, (2 * CH, CH), 0) & (CH - 1)
        c2 = lax.broadcasted_iota(jnp.int32, (2 * CH, CH), 1)
        causal2 = r2 >= c2

        s_state = s_ref[...]
        for c in range(NCH):
            rows = slice(c * CH, (c + 1) * CH)
            q = q_ref[rows, :].astype(F32)
            kT = kT_ref[:, rows].astype(F32)
            v = v_ref[rows, :]
            if gla:
                e_b = jnp.exp(cumsum_fwd(lf_ref[rows, :]))
                e_r = jnp.exp(cumsum_bwd(lb_ref[rows, :]))
            else:
                e_b = jnp.exp(ret_b)
                e_r = jnp.exp(ret_r)
            ebTi, d_s = fwd_tables(rows)
            erTi, _ = bwd_tables(rows)

            qf = q * e_b
            qb = q * e_r
            qf_st = jnp.concatenate([jnp.where(head_a, qf, 0.0), jnp.where(head_a, 0.0, qf)], axis=0).astype(BF16)
            qb_st = jnp.concatenate([jnp.where(head_a, qb, 0.0), jnp.where(head_a, 0.0, qb)], axis=0).astype(BF16)
            kfi = kT * ebTi
            kbi = kT * erTi
            sc_f = _dot(qf_st, kfi.astype(BF16))
            sc_b = _dot(qb_st, kbi.astype(BF16))
            p = jnp.where(causal2, sc_f, sc_b).astype(BF16)
            r_in = rst_ref[j * NCH + c]
            o_inter = _dot(qf_st, s_state.astype(BF16)) + _dot(qb_st, r_in)
            o_a = _dot(p[0:CH], v[:, 0:HEAD_DV]) + o_inter[0:CH]
            o_b = _dot(p[CH:2 * CH], v[:, HEAD_DV:2 * HEAD_DV]) + o_inter[CH:2 * CH]

            def nrm(o):
                y = o * lax.rsqrt(jnp.mean(o * o, axis=-1, keepdims=True) + RMS_EPS)
                return y * ng_ref[...] if gla else y

            y = jnp.concatenate([nrm(o_a), nrm(o_b)], axis=1) * g_ref[rows, :].astype(F32)
            y_ref[rows, :] = y.astype(BF16)

            kfe = (kfi * d_s).astype(BF16)
            s_state = d_s * s_state + _diag_blocks(_dot(kfe, v))
        s_ref[...] = s_state


def _scan_blk(b, phase, j):
    lat = b * NBLK + jnp.where(phase == 0, NBLK - j, j - 1)
    return jnp.where(j == 0, CTX_BLK0 + b, lat)


def _scan_group(q, kT, v, g, extra, *, gla):
    rowmap = lambda b, p, ph, j: (_scan_blk(b, ph, j), p)
    colmap = lambda b, p, ph, j: (p, _scan_blk(b, ph, j))
    in_specs = [
        pl.BlockSpec((TB, 2 * HEAD_DK), rowmap),
        pl.BlockSpec((2 * HEAD_DK, TB), colmap),
        pl.BlockSpec((TB, 2 * HEAD_DV), rowmap),
        pl.BlockSpec((TB, 2 * HEAD_DV), rowmap),
    ]
    if gla:
        lf, lb, lfT, lbT, ng = extra
        in_specs += [
            pl.BlockSpec((TB, 2 * HEAD_DK), rowmap),
            pl.BlockSpec((TB, 2 * HEAD_DK), rowmap),
            pl.BlockSpec((2 * HEAD_DK, TB), colmap),
            pl.BlockSpec((2 * HEAD_DK, TB), colmap),
            pl.BlockSpec((1, HEAD_DV), lambda b, p, ph, j: (0, 0)),
        ]
        args = (q, kT, v, g, lf, lb, lfT, lbT, ng)
    else:
        th_row, th_col = extra
        in_specs += [
            pl.BlockSpec((None, 2, 2 * HEAD_DK), lambda b, p, ph, j: (p, 0, 0)),
            pl.BlockSpec((None, 2 * HEAD_DK, 2), lambda b, p, ph, j: (p, 0, 0)),
        ]
        args = (q, kT, v, g, th_row, th_col)
    out_map = lambda b, p, ph, j: (jnp.where(ph == 0, CTX_BLK0 + b, _scan_blk(b, ph, j)), p)
    return pl.pallas_call(
        functools.partial(_scan_kernel, gla=gla),
        grid=(B, 2, 2, NSTEP),
        in_specs=in_specs,
        out_specs=pl.BlockSpec((TB, 2 * HEAD_DV), out_map),
        out_shape=jax.ShapeDtypeStruct((ROWS, V_W), BF16),
        scratch_shapes=[
            pltpu.VMEM((2 * HEAD_DK, HEAD_DV), F32),
            pltpu.VMEM((2 * HEAD_DK, HEAD_DV), F32),
            pltpu.VMEM((NSTEP * NCH, 2 * HEAD_DK, HEAD_DV), BF16),
        ],
        compiler_params=pltpu.CompilerParams(
            dimension_semantics=("arbitrary",) * 4, vmem_limit_bytes=VMEM_LIMIT),
        name="scan_gla" if gla else "scan_ret",
    )(*args)


FF_CH = 512


def _out_kernel(*refs, layer, split_ctx):
    if split_ctx:
        ya_ref, yb_ref, x_ref, ctx_ref, mod_ref, ln_ref, wo_ref, w1_ref, w2_ref, o_ref = refs
        t = pl.program_id(0)
        x = jnp.where(t == N_LAT_TILES, ctx_ref[...], x_ref[...])
    else:
        ya_ref, yb_ref, x_ref, mod_ref, ln_ref, wo_ref, w1_ref, w2_ref, o_ref = refs
        x = x_ref[...]
    g1 = mod_ref[:, 2 * D:3 * D]
    sh2 = mod_ref[:, 3 * D:4 * D]
    sc2 = mod_ref[:, 4 * D:5 * D]
    g2 = mod_ref[:, 5 * D:6 * D]
    ln_g0 = ln_ref[2 * layer:2 * layer + 1, :]
    ln_g1 = ln_ref[2 * layer + 1:2 * layer + 2, :]
    ln_b0 = ln_ref[2 * DEPTH + 2 * layer:2 * DEPTH + 2 * layer + 1, :]
    ln_b1 = ln_ref[2 * DEPTH + 2 * layer + 1:2 * DEPTH + 2 * layer + 2, :]

    half = D // 2
    y = _dot(ya_ref[...], wo_ref[0:half, :]) + _dot(yb_ref[...], wo_ref[half:D, :])
    x1 = _layer_norm(ALPHA * x + g1 * y, ln_g0, ln_b0)
    h2 = (x1 * (1.0 + sc2) + sh2).astype(BF16)
    acc = jnp.zeros((TM, D), F32)
    for c in range(D_FF // FF_CH):
        cols = slice(c * FF_CH, (c + 1) * FF_CH)
        hc = jnp.maximum(_dot(h2, w1_ref[:, cols]), 0.0)
        acc = acc + _dot((hc * hc).astype(BF16), w2_ref[cols, :])
    o_ref[...] = _layer_norm(ALPHA * x1 + g2 * acc, ln_g1, ln_b1)


def _out_mlp(ya, yb, ymap_a, ymap_b, xs, ctx2, mods, ln, wo, w1, w2, *, layer, n_tiles, split_ctx):
    half = D // 2
    in_specs = [pl.BlockSpec((TM, half), ymap_a), pl.BlockSpec((TM, half), ymap_b)]
    args = [ya, yb]
    if split_ctx:
        in_specs += [pl.BlockSpec((TM, D), lambda t: (jnp.minimum(t, N_LAT_TILES - 1), 0)),
                     _full_spec((B * L, D))]
        args += [xs, ctx2]
    else:
        in_specs += [pl.BlockSpec((TM, D), lambda t: (t, 0))]
        args += [xs]
    in_specs += [_mod_spec(layer), _full_spec(ln.shape), _full_spec(wo.shape),
                 _full_spec(w1.shape), _full_spec(w2.shape)]
    args += [mods, ln, wo, w1, w2]
    return pl.pallas_call(
        functools.partial(_out_kernel, layer=layer, split_ctx=split_ctx),
        grid=(n_tiles,),
        in_specs=in_specs,
        out_specs=pl.BlockSpec((TM, D), lambda t: (t, 0)),
        out_shape=jax.ShapeDtypeStruct((n_tiles * TM, D), F32),
        compiler_params=pltpu.CompilerParams(
            dimension_semantics=("arbitrary",), vmem_limit_bytes=VMEM_LIMIT),
        name="out_mlp%d" % layer,
    )(*args)


def _in1_kernel(x_ref, mod_ref, cos_ref, sa_ref, sb_ref, wq_ref, wk_ref, wv_ref, q_o, k_o, v_o):
    sh1 = mod_ref[:, 0:D]
    sc1 = mod_ref[:, D:2 * D]
    h = (x_ref[...] * (1.0 + sc1) + sh1).astype(BF16)
    cos = cos_ref[...]
    sa = sa_ref[...]
    sb = sb_ref[...]

    def rope(u):
        return u * cos + pltpu.roll(u, 128 - 16, 1) * sa + pltpu.roll(u, 16, 1) * sb

    q = _dot(h, wq_ref[...]) * (ATT_DH ** -0.5)
    for i in range(D // 128):
        q_o[:, i * 128:(i + 1) * 128] = rope(q[:, i * 128:(i + 1) * 128]).astype(BF16)
    k = _dot(h, wk_ref[...])
    for i in range(ATT_KVW // 128):
        k_o[:, i * 128:(i + 1) * 128] = rope(k[:, i * 128:(i + 1) * 128]).astype(BF16)
    v_o[...] = _dot(h, wv_ref[...]).astype(BF16)


def _in_proj1(xs, mods, cos_t, sa_t, sb_t, wq, wk, wv):
    tab = lambda t: (jnp.where(t == N_LAT_TILES, TILES_PER_BATCH, t % TILES_PER_BATCH), 0)
    row = lambda t: (t, 0)
    return pl.pallas_call(
        _in1_kernel,
        grid=(N_TILES,),
        in_specs=[
            pl.BlockSpec((TM, D), row), _mod_spec(1),
            pl.BlockSpec((TM, 128), tab), pl.BlockSpec((TM, 128), tab), pl.BlockSpec((TM, 128), tab),
            _full_spec(wq.shape), _full_spec(wk.shape), _full_spec(wv.shape),
        ],
        out_specs=[pl.BlockSpec((TM, D), row), pl.BlockSpec((TM, ATT_KVW), row), pl.BlockSpec((TM, ATT_KVW), row)],
        out_shape=[jax.ShapeDtypeStruct((ROWS, D), BF16), jax.ShapeDtypeStruct((ROWS, ATT_KVW), BF16),
                   jax.ShapeDtypeStruct((ROWS, ATT_KVW), BF16)],
        compiler_params=pltpu.CompilerParams(
            dimension_semantics=("arbitrary",), vmem_limit_bytes=VMEM_LIMIT),
        name="in_proj1",
    )(xs, mods, cos_t, sa_t, sb_t, wq, wk, wv)


def _attn_kernel(sink_ref, q_ref, k_ref, v_ref, kc_ref, vc_ref, o_ref):
    n = pl.program_id(1)
    rq = lax.broadcasted_iota(jnp.int32, (ATT_GROUP * QB, KWIN), 0) & (QB - 1)
    ck = lax.broadcasted_iota(jnp.int32, (ATT_GROUP * QB, KWIN), 1)
    rel = ck - rq
    grp = lax.broadcasted_iota(jnp.int32, (ATT_GROUP * QB, 1), 0) // QB

    def block(i, carry):
        n0 = (n * (TQ // QB) + i) * QB
        start = pl.multiple_of(jnp.clip(n0 - QB, 0, T - KWIN), QB)
        d = rel + (start - n0)
        valid = (d >= -WINDOW) & (d <= WINDOW)
        qrows = pl.ds(pl.multiple_of(i * QB, QB), QB)
        for kh in range(ATT_KVH):
            kcols = slice(kh * ATT_DH, (kh + 1) * ATT_DH)
            qg = jnp.concatenate(
                [q_ref[qrows, (kh * ATT_GROUP + g) * ATT_DH:(kh * ATT_GROUP + g + 1) * ATT_DH]
                 for g in range(ATT_GROUP)], axis=0)
            sink = jnp.zeros((ATT_GROUP * QB, 1), F32)
            for g in range(ATT_GROUP):
                sink = jnp.where(grp == g, sink_ref[kh * ATT_GROUP + g], sink)
            kw = k_ref[pl.ds(start, KWIN), kcols]
            vw = v_ref[pl.ds(start, KWIN), kcols]
            s_loc = jnp.where(valid, _dot_nt(qg, kw), -jnp.inf)
            s_ctx = _dot_nt(qg, kc_ref[:, kcols])
            m = jnp.maximum(jnp.maximum(jnp.max(s_loc, axis=-1, keepdims=True),
                                        jnp.max(s_ctx, axis=-1, keepdims=True)), sink)
            e_loc = jnp.exp(s_loc - m)
            e_ctx = jnp.exp(s_ctx - m)
            denom = (jnp.sum(e_loc, axis=-1, keepdims=True) + jnp.sum(e_ctx, axis=-1, keepdims=True)
                     + jnp.exp(sink - m))
            o = (_dot(e_loc.astype(BF16), vw) + _dot(e_ctx.astype(BF16), vc_ref[:, kcols])) / denom
            for g in range(ATT_GROUP):
                hcols = slice((kh * ATT_GROUP + g) * ATT_DH, (kh * ATT_GROUP + g + 1) * ATT_DH)
                o_ref[qrows, hcols] = o[g * QB:(g + 1) * QB].astype(BF16)
        return carry

    lax.fori_loop(0, TQ // QB, block, 0)


def _attention(sink, q, k, v):
    nq = T // TQ
    return pl.pallas_call(
        _attn_kernel,
        grid=(B, nq),
        in_specs=[
            pl.BlockSpec(memory_space=pltpu.SMEM),
            pl.BlockSpec((TQ, D), lambda b, n: (b * nq + n, 0)),
            pl.BlockSpec((T, ATT_KVW), lambda b, n: (b, 0)),
            pl.BlockSpec((T, ATT_KVW), lambda b, n: (b, 0)),
            pl.BlockSpec((L, ATT_KVW), lambda b, n: (LAT_ROWS // L + b, 0)),
            pl.BlockSpec((L, ATT_KVW), lambda b, n: (LAT_ROWS // L + b, 0)),
        ],
        out_specs=pl.BlockSpec((TQ, D), lambda b, n: (b * nq + n, 0)),
        out_shape=jax.ShapeDtypeStruct((LAT_ROWS, D), BF16),
        compiler_params=pltpu.CompilerParams(
            dimension_semantics=("arbitrary", "arbitrary"), vmem_limit_bytes=VMEM_LIMIT),
        name="window_attn",
    )(sink, q, k, v, k, v)


def _rope_tables():
    half = ATT_DH // 2
    inv_freq = ROPE_BASE ** (-jnp.arange(0, half, 2, dtype=F32) / half)
    pos = jnp.arange(T)
    ang_r = (pos // GRID_W).astype(F32)[:, None] * inv_freq[None, :]
    ang_c = (pos % GRID_W).astype(F32)[:, None] * inv_freq[None, :]
    zero = jnp.zeros_like(ang_r)
    cos = jnp.concatenate([jnp.cos(ang_r), jnp.cos(ang_r), jnp.cos(ang_c), jnp.cos(ang_c)], axis=1)
    sa = jnp.concatenate([-jnp.sin(ang_r), zero, -jnp.sin(ang_c), zero], axis=1)
    sb = jnp.concatenate([zero, jnp.sin(ang_r), zero, jnp.sin(ang_c)], axis=1)
    ident = lambda a, fill: jnp.concatenate([a, jnp.full((TM, ATT_DH), fill, F32)], axis=0)
    tile2 = lambda a: jnp.concatenate([a, a], axis=1)
    return tile2(ident(cos, 1.0)), tile2(ident(sa, 0.0)), tile2(ident(sb, 0.0))


def kernel(x, c, ctx, c_ctx, w_mod, b_mod, ln_g, ln_b, mlp_w1, mlp_w2, ev_w_in, ev_ret_theta, ev_gla_gk_w,
           ev_gla_gk_b, ev_gla_norm_g, ev_w_out, od_w_qkv, od_sink, od_w_out):
    x2 = x.reshape(LAT_ROWS, D)
    ctx2 = ctx.reshape(B * L, D)

    cs = jnp.concatenate([c, c_ctx[None, :], jnp.zeros((8 - B - 1, D), F32)], axis=0)
    mods = _modulation(cs, w_mod, b_mod).reshape(DEPTH * 8, 1, 6 * D)
    ln = jnp.concatenate([ln_g.reshape(2 * DEPTH, D), ln_b.reshape(2 * DEPTH, D)], axis=0)

    w_in = ev_w_in[0]
    off = [0]
    for s in (QK_W, QK_W, V_W, V_W, QK_W, QK_W, V_W, V_W, 2 * GATE_RANK):
        off.append(off[-1] + s)
    piece = lambda i: w_in[:, off[i]:off[i + 1]].astype(BF16)
    gk_w = ev_gla_gk_w[0]
    zeros = jnp.zeros((GATE_RANK, QK_W), F32)
    g2 = jnp.concatenate([jnp.concatenate([gk_w[0], zeros], axis=1),
                          jnp.concatenate([zeros, gk_w[1]], axis=1)], axis=0)
    gb = ev_gla_gk_b[0].reshape(1, 2 * QK_W)
    w0 = dict(wq_a=piece(0), wkT_a=piece(1).T, wv_a=piece(2), wg_a=piece(3),
              wq_b=piece(4), wkT_b=piece(5).T, wv_b=piece(6), wg_b=piece(7), wlr=piece(8),
              g2=g2.astype(BF16), g2T=g2.T.astype(BF16), gbr=gb, gbc=gb.reshape(2 * QK_W, 1))
    qa, kaT, va, ga, qb, kbT, vb, gbv, lf, lb, lfT, lbT = _in_proj0(x2, ctx2, mods, w0)

    theta = ev_ret_theta[0]
    th_row = jnp.repeat(theta, HEAD_DK, axis=1).reshape(2, 2, 2 * HEAD_DK).transpose(1, 0, 2)
    th_col = th_row.transpose(0, 2, 1)
    y_ret = _scan_group(qa, kaT, va, ga, (th_row, th_col), gla=False)
    y_gla = _scan_group(qb, kbT, vb, gbv, (lf, lb, lfT, lbT, ev_gla_norm_g[0].reshape(1, HEAD_DV)), gla=True)

    xs = _out_mlp(y_ret, y_gla, lambda t: (t, 0), lambda t: (t, 0), x2, ctx2, mods, ln,
                  ev_w_out[0].astype(BF16), mlp_w1[0].astype(BF16), mlp_w2[0].astype(BF16),
                  layer=0, n_tiles=N_TILES, split_ctx=True)

    wqkv = od_w_qkv[0].astype(BF16)
    cos_t, sa_t, sb_t = _rope_tables()
    q1, k1, v1 = _in_proj1(xs, mods, cos_t, sa_t, sb_t, wqkv[:, 0:D], wqkv[:, D:D + ATT_KVW],
                           wqkv[:, D + ATT_KVW:D + 2 * ATT_KVW])
    att = _attention(od_sink[0], q1, k1, v1)
    out = _out_mlp(att, att, lambda t: (t, 0), lambda t: (t, 1), xs, None, mods, ln,
                   od_w_out[0].astype(BF16), mlp_w1[1].astype(BF16), mlp_w2[1].astype(BF16),
                   layer=1, n_tiles=N_LAT_TILES, split_ctx=False)
    return out.reshape(B, T, D)
```

```python
import functools

import jax
import jax.numpy as jnp
from jax import lax
from jax.experimental import pallas as pl
from jax.experimental.pallas import tpu as pltpu

F32 = jnp.float32
BF16 = jnp.bfloat16

D = 1024
B = 2
T = 8192
L = 256
DEPTH = 2
GRID_W = 64
D_FF = 4 * D
HEAD_DV = 128
HEAD_DK = 64
GATE_RANK = 16
GATE_TAU = 16.0
QK_W = 256
V_W = 512
ATT_DH = 64
ATT_QH = 16
ATT_KVH = 4
ATT_GROUP = 4
ATT_KVW = ATT_KVH * ATT_DH
WINDOW = 128
ROPE_BASE = 10000.0
ALPHA = (2.0 * DEPTH) ** 0.25
LN_EPS = 1e-5
RMS_EPS = 1e-6

TM = 512
LAT_ROWS = B * T
ROWS = LAT_ROWS + B * L
N_LAT_TILES = LAT_ROWS // TM
N_TILES = ROWS // TM
TILES_PER_BATCH = T // TM

CH = 64
SP = 2 * CH
TB = 1024
NP = TB // SP
NP_CTX = L // SP
NBLK = T // TB
NP_ALL = NP_CTX + T // SP
N_SLABS = ROWS // SP
KV_UNROLL = 8
OUT_UNROLL = 4

TQ = 512
QB = 128
KWIN = 3 * QB

VMEM_LIMIT = 56 * 1024 * 1024


def _dot(a, b):
    return jnp.dot(a, b, preferred_element_type=F32)


def _dot_nt(a, b):
    return lax.dot_general(a, b, (((1,), (1,)), ((), ())), preferred_element_type=F32)


def _full_spec(shape):
    nd = len(shape)
    return pl.BlockSpec(shape, lambda *_: (0,) * nd, pipeline_mode=pl.Buffered(1))


def _mod_row(t):
    return jnp.minimum(t // TILES_PER_BATCH, B)


def _mod_spec(layer):
    return pl.BlockSpec((None, 1, 6 * D), lambda t: (layer * 8 + _mod_row(t), 0, 0))


def _layer_norm(x, g, b):
    mu = jnp.mean(x, axis=-1, keepdims=True)
    xc = x - mu
    var = jnp.mean(xc * xc, axis=-1, keepdims=True)
    return xc * lax.rsqrt(var + LN_EPS) * g + b


def _log_sigmoid(z):
    return jnp.minimum(z, 0.0) - jnp.log1p(jnp.exp(-jnp.abs(z)))


MOD_TN = 1536


def _mod_kernel(c_ref, w_ref, b_ref, o_ref):
    s = jax.nn.silu(c_ref[...])
    o_ref[...] = jnp.dot(s, w_ref[...], precision=lax.Precision.HIGHEST,
                         preferred_element_type=F32) + b_ref[...]


def _modulation(cs, w_mod, b_mod):
    return pl.pallas_call(
        _mod_kernel,
        grid=(DEPTH, 6 * D // MOD_TN),
        in_specs=[
            pl.BlockSpec((8, D), lambda i, n: (0, 0)),
            pl.BlockSpec((None, D, MOD_TN), lambda i, n: (i, 0, n)),
            pl.BlockSpec((None, 1, MOD_TN), lambda i, n: (i, 0, n)),
        ],
        out_specs=pl.BlockSpec((None, 8, MOD_TN), lambda i, n: (i, 0, n)),
        out_shape=jax.ShapeDtypeStruct((DEPTH, 8, 6 * D), F32),
        compiler_params=pltpu.CompilerParams(
            dimension_semantics=("arbitrary", "arbitrary"), vmem_limit_bytes=VMEM_LIMIT),
        name="modulation",
    )(cs, w_mod, b_mod.reshape(DEPTH, 1, 6 * D))


def _in0_kernel(x_ref, ctx_ref, mod_ref,
                wq_a, wkT_a, wv_a, wg_a, wq_b, wkT_b, wv_b, wg_b, wlr, g2T, gbc,
                qa_o, kaT_o, va_o, ga_o, qb_o, kbT_o, vb_o, gb_o, lfT_o, lbT_o):
    t = pl.program_id(0)
    xt = jnp.where(t == N_LAT_TILES, ctx_ref[...], x_ref[...])
    sh1 = mod_ref[:, 0:D]
    sc1 = mod_ref[:, D:2 * D]
    h = (xt * (1.0 + sc1) + sh1).astype(BF16)
    qk_scale = HEAD_DK ** -0.5

    def put_slabs(o_ref, val):
        for i in range(TM // SP):
            o_ref[i] = val[:, i * SP:(i + 1) * SP].astype(o_ref.dtype)

    qa_o[...] = _dot(h, wq_a[...]).astype(BF16)
    put_slabs(kaT_o, _dot_nt(wkT_a[...], h) * qk_scale)
    va_o[...] = _dot(h, wv_a[...]).astype(BF16)
    ga_o[...] = jax.nn.silu(_dot(h, wg_a[...])).astype(BF16)

    qb_o[...] = (_dot(h, wq_b[...]) * qk_scale).astype(BF16)
    put_slabs(kbT_o, _dot_nt(wkT_b[...], h))
    vb_o[...] = _dot(h, wv_b[...]).astype(BF16)
    gb_o[...] = jax.nn.silu(_dot(h, wg_b[...])).astype(BF16)

    lr = _dot(h, wlr[...]).astype(BF16)
    lsT = _log_sigmoid(_dot_nt(g2T[...], lr) + gbc[...]) * (1.0 / GATE_TAU)
    put_slabs(lfT_o, lsT[0:QK_W, :])
    put_slabs(lbT_o, lsT[QK_W:2 * QK_W, :])


def _in_proj0(x2, ctx2, mods, w):
    row = lambda t: (t, 0)
    slab = lambda t: (t, 0, 0)
    nsl = TM // SP
    row_out = lambda width: (jax.ShapeDtypeStruct((ROWS, width), BF16), pl.BlockSpec((TM, width), row))
    slab_out = lambda dt: (jax.ShapeDtypeStruct((N_SLABS, QK_W, SP), dt), pl.BlockSpec((nsl, QK_W, SP), slab))
    outs = [row_out(QK_W), slab_out(BF16), row_out(V_W), row_out(V_W),
            row_out(QK_W), slab_out(BF16), row_out(V_W), row_out(V_W),
            slab_out(F32), slab_out(F32)]
    weights = [w["wq_a"], w["wkT_a"], w["wv_a"], w["wg_a"], w["wq_b"], w["wkT_b"], w["wv_b"], w["wg_b"],
               w["wlr"], w["g2T"], w["gbc"]]
    in_specs = [
        pl.BlockSpec((TM, D), lambda t: (jnp.minimum(t, N_LAT_TILES - 1), 0)),
        _full_spec((B * L, D)),
        _mod_spec(0),
    ] + [_full_spec(a.shape) for a in weights]
    return pl.pallas_call(
        _in0_kernel,
        grid=(N_TILES,),
        in_specs=in_specs,
        out_specs=[o[1] for o in outs],
        out_shape=[o[0] for o in outs],
        compiler_params=pltpu.CompilerParams(
            dimension_semantics=("arbitrary",), vmem_limit_bytes=VMEM_LIMIT),
        name="in_proj0",
    )(x2, ctx2, mods, *weights)


def _hi_lo(a):
    hi = a.astype(BF16)
    lo = (a - hi.astype(F32)).astype(BF16)
    return hi, lo


def _dot_hi_lo(a, w):
    hi, lo = _hi_lo(a)
    return _dot(hi, w) + _dot(lo, w)


def _dot_nt_hi_lo(w, a):
    hi, lo = _hi_lo(a)
    return _dot_nt(w, hi) + _dot_nt(w, lo)


def _chunk_diag(kv, c):
    r0 = c * 2 * HEAD_DK
    return jnp.concatenate([kv[r0:r0 + HEAD_DK, 0:HEAD_DV],
                            kv[r0 + HEAD_DK:r0 + 2 * HEAD_DK, HEAD_DV:2 * HEAD_DV]], axis=0)


def _scan_kernel(*refs, gla):
    if gla:
        (q_ref, kT_ref, v_ref, g_ref, lfT_ref, lbT_ref,
         qc_ref, kTc_ref, vc_ref, gc_ref, lfTc_ref, lbTc_ref, ng_ref,
         y_ref, yc_ref, s_ref, r_ref, rst_ref, sst_ref, kv_ref, dec_ref) = refs
        lat = (q_ref, kT_ref, v_ref, g_ref, lfT_ref, lbT_ref)
        cxt = (qc_ref, kTc_ref, vc_ref, gc_ref, lfTc_ref, lbTc_ref)
    else:
        (q_ref, kT_ref, v_ref, g_ref, qc_ref, kTc_ref, vc_ref, gc_ref, thr_ref, thc_ref,
         y_ref, yc_ref, s_ref, r_ref, rst_ref, sst_ref, kv_ref, dec_ref) = refs
        lat = (q_ref, kT_ref, v_ref, g_ref, None, None)
        cxt = (qc_ref, kTc_ref, vc_ref, gc_ref, None, None)

    phase = pl.program_id(2)
    j = pl.program_id(3)

    ri = lax.broadcasted_iota(jnp.int32, (SP, SP), 0)
    ci = lax.broadcasted_iota(jnp.int32, (SP, SP), 1)
    same = (ri // CH) == (ci // CH)
    first_lane = ci < CH
    head_a = ci < HEAD_DK

    if gla:
        as_w = lambda m: m.astype(BF16)
        tot = jnp.concatenate([jnp.broadcast_to(ri < CH, (SP, SP)), jnp.broadcast_to(ri >= CH, (SP, SP))], axis=1)
        w_end_f = jnp.concatenate([as_w(same & (ri > ci)), as_w(tot)], axis=1)
        w_end_b = jnp.concatenate([as_w(same & (ri < ci)), as_w(tot)], axis=1)
        w_cum_f = as_w(same & (ri <= ci))
        w_cum_b = as_w(same & (ri >= ci))
    else:
        lg_row = jnp.log1p(-jnp.exp(thr_ref[...]))
        lg_col = jnp.log1p(-jnp.exp(thc_ref[...]))
        it = (ci % CH).astype(F32)
        ir = (ri % CH).astype(F32)
        ret_end_f = jnp.exp((CH - 1.0 - it) * lg_col[:, 0:1])
        ret_end_b = jnp.exp(it * lg_col[:, 1:2])
        ret_dec_f = jnp.exp(jnp.broadcast_to(CH * lg_col[:, 0:1], (SP, SP)))
        ret_dec_b = jnp.exp(jnp.broadcast_to(CH * lg_col[:, 1:2], (SP, SP)))
        ret_ebTi = jnp.exp(-(it + 1.0) * lg_col[:, 0:1])
        ret_erTi = jnp.exp(-(CH - it) * lg_col[:, 1:2])
        ret_eb = jnp.exp((ir + 1.0) * lg_row[0:1, :])
        ret_er = jnp.exp((CH - ir) * lg_row[1:2, :])

    def kv_stage(blk, n, fwd):
        _, kT_r, v_r, _, lfT_r, lbT_r = blk

        def body(p, carry):
            kT = kT_r[p].astype(F32)
            v = v_r[pl.ds(pl.multiple_of(p * SP, SP), SP), :]
            if gla:
                res = _dot_hi_lo((lfT_r if fwd else lbT_r)[p], w_end_f if fwd else w_end_b)
                e_end = jnp.exp(res[:, 0:SP])
                dec0 = jnp.exp(res[:, SP:2 * SP])
                dec1 = jnp.exp(res[:, 2 * SP:3 * SP])
            else:
                e_end = ret_end_f if fwd else ret_end_b
                dec0 = dec1 = ret_dec_f if fwd else ret_dec_b
            ke = kT * e_end
            lhs = jnp.concatenate([jnp.where(first_lane, ke, 0.0), jnp.where(first_lane, 0.0, ke)],
                                  axis=0).astype(BF16)
            kv = _dot(lhs, v)
            kv_ref[2 * p] = _chunk_diag(kv, 0)
            kv_ref[2 * p + 1] = _chunk_diag(kv, 1)
            dec_ref[2 * p] = dec0
            dec_ref[2 * p + 1] = dec1
            return carry

        lax.fori_loop(0, n, body, 0, unroll=min(n, KV_UNROLL))

    def phase0_block(blk, n, slot0):
        kv_stage(blk, n, fwd=False)

        def body(i, r_state):
            p = n - 1 - i
            rst_ref[slot0 + p, :, HEAD_DV:2 * HEAD_DV] = r_state.astype(BF16)
            r_state = dec_ref[2 * p + 1] * r_state + kv_ref[2 * p + 1]
            rst_ref[slot0 + p, :, 0:HEAD_DV] = r_state.astype(BF16)
            return dec_ref[2 * p] * r_state + kv_ref[2 * p]

        r_ref[...] = lax.fori_loop(0, n, body, r_ref[...])

    def phase1_block(blk, n, slot0, out_ref):
        q_r, kT_r, v_r, g_r, lfT_r, lbT_r = blk
        kv_stage(blk, n, fwd=True)

        def rec(p, s_state):
            sst_ref[p, :, 0:HEAD_DV] = s_state.astype(BF16)
            s_state = dec_ref[2 * p] * s_state + kv_ref[2 * p]
            sst_ref[p, :, HEAD_DV:2 * HEAD_DV] = s_state.astype(BF16)
            return dec_ref[2 * p + 1] * s_state + kv_ref[2 * p + 1]

        s_ref[...] = lax.fori_loop(0, n, rec, s_ref[...])

        r2 = lax.broadcasted_iota(jnp.int32, (2 * SP, SP), 0) % SP
        c2 = lax.broadcasted_iota(jnp.int32, (2 * SP, SP), 1)
        same2 = (r2 // CH) == (c2 // CH)
        mask_f = same2 & (r2 >= c2)
        mask_b = same2 & (r2 < c2)

        def body(p, carry):
            rows = pl.ds(pl.multiple_of(p * SP, SP), SP)
            q = q_r[rows, :].astype(F32)
            kT = kT_r[p].astype(F32)
            v = v_r[rows, :]
            if gla:
                lfT = lfT_r[p]
                lbT = lbT_r[p]
                ebTi = jnp.exp(-_dot_hi_lo(lfT, w_cum_f))
                erTi = jnp.exp(-_dot_hi_lo(lbT, w_cum_b))
                e_b = jnp.exp(_dot_nt_hi_lo(w_cum_b, lfT))
                e_r = jnp.exp(_dot_nt_hi_lo(w_cum_f, lbT))
            else:
                ebTi, erTi, e_b, e_r = ret_ebTi, ret_erTi, ret_eb, ret_er
            qf = q * e_b
            qb = q * e_r
            lhs_f = jnp.concatenate([jnp.where(head_a, qf, 0.0), jnp.where(head_a, 0.0, qf)], axis=0).astype(BF16)
            lhs_b = jnp.concatenate([jnp.where(head_a, qb, 0.0), jnp.where(head_a, 0.0, qb)], axis=0).astype(BF16)
            sc_f = _dot(lhs_f, (kT * ebTi).astype(BF16))
            sc_b = _dot(lhs_b, (kT * erTi).astype(BF16))
            pm = jnp.where(mask_f, sc_f, jnp.where(mask_b, sc_b, 0.0)).astype(BF16)
            states = jnp.concatenate([sst_ref[p], rst_ref[slot0 + p]], axis=0)
            o_int = _dot(jnp.concatenate([lhs_f, lhs_b], axis=1), states)
            o_a = _dot(pm[0:SP], v[:, 0:HEAD_DV]) + jnp.concatenate(
                [o_int[0:CH, 0:HEAD_DV], o_int[CH:SP, HEAD_DV:2 * HEAD_DV]], axis=0)
            o_b = _dot(pm[SP:2 * SP], v[:, HEAD_DV:2 * HEAD_DV]) + jnp.concatenate(
                [o_int[SP:SP + CH, 0:HEAD_DV], o_int[SP + CH:2 * SP, HEAD_DV:2 * HEAD_DV]], axis=0)

            def nrm(o):
                y = o * lax.rsqrt(jnp.mean(o * o, axis=-1, keepdims=True) + RMS_EPS)
                return y * ng_ref[...] if gla else y

            y = jnp.concatenate([nrm(o_a), nrm(o_b)], axis=1) * g_r[rows, :].astype(F32)
            out_ref[rows, :] = y.astype(BF16)
            return carry

        lax.fori_loop(0, n, body, 0, unroll=min(n, OUT_UNROLL))

    @pl.when(phase == 0)
    def _():
        @pl.when(j == 0)
        def _():
            r_ref[...] = jnp.zeros_like(r_ref)
            phase0_block(cxt, NP_CTX, 0)

        phase0_block(lat, NP, NP_CTX + (NBLK - 1 - j) * NP)

    @pl.when(phase == 1)
    def _():
        @pl.when(j == 0)
        def _():
            s_ref[...] = jnp.zeros_like(s_ref)
            phase1_block(cxt, NP_CTX, 0, yc_ref)

        phase1_block(lat, NP, NP_CTX + j * NP, y_ref)


def _scan_group(q, kT, v, g, extra, *, gla):
    def blk(b, ph, j, used_in_phase0):
        jj = jnp.where(ph == 0, NBLK - 1 - j, j)
        if not used_in_phase0:
            jj = jnp.where(ph == 0, 0, jj)
        return b * NBLK + jj

    def lat_specs(used0):
        return dict(
            row=lambda w: pl.BlockSpec((TB, w), lambda b, p, ph, j: (blk(b, ph, j, used0), p)),
            slab=pl.BlockSpec((NP, SP, SP), lambda b, p, ph, j: (blk(b, ph, j, used0), p, 0)))

    ctx_row = lambda w: pl.BlockSpec((L, w), lambda b, p, ph, j: (LAT_ROWS // L + b, p))
    ctx_slab = pl.BlockSpec((NP_CTX, SP, SP), lambda b, p, ph, j: (LAT_ROWS // L + b, p, 0))
    used, unused = lat_specs(True), lat_specs(False)

    in_specs = [unused["row"](2 * HEAD_DK), used["slab"], used["row"](2 * HEAD_DV), unused["row"](2 * HEAD_DV)]
    ctx_specs = [ctx_row(2 * HEAD_DK), ctx_slab, ctx_row(2 * HEAD_DV), ctx_row(2 * HEAD_DV)]
    if gla:
        lfT, lbT, ng = extra
        in_specs += [unused["slab"], used["slab"]]
        ctx_specs += [ctx_slab, ctx_slab]
        args = (q, kT, v, g, lfT, lbT, q, kT, v, g, lfT, lbT, ng)
        in_specs = in_specs + ctx_specs + [pl.BlockSpec((1, HEAD_DV), lambda b, p, ph, j: (0, 0))]
    else:
        th_row, th_col = extra
        args = (q, kT, v, g, q, kT, v, g, th_row, th_col)
        in_specs = in_specs + ctx_specs + [
            pl.BlockSpec((None, 2, 2 * HEAD_DK), lambda b, p, ph, j: (p, 0, 0)),
            pl.BlockSpec((None, 2 * HEAD_DK, 2), lambda b, p, ph, j: (p, 0, 0)),
        ]
    return pl.pallas_call(
        functools.partial(_scan_kernel, gla=gla),
        grid=(B, 2, 2, NBLK),
        in_specs=in_specs,
        out_specs=[
            pl.BlockSpec((TB, 2 * HEAD_DV), lambda b, p, ph, j: (b * NBLK + jnp.where(ph == 0, 0, j), p)),
            pl.BlockSpec((L, 2 * HEAD_DV), lambda b, p, ph, j: (b, p)),
        ],
        out_shape=[jax.ShapeDtypeStruct((LAT_ROWS, V_W), BF16), jax.ShapeDtypeStruct((B * L, V_W), BF16)],
        scratch_shapes=[
            pltpu.VMEM((SP, HEAD_DV), F32),
            pltpu.VMEM((SP, HEAD_DV), F32),
            pltpu.VMEM((NP_ALL, SP, 2 * HEAD_DV), BF16),
            pltpu.VMEM((NP, SP, 2 * HEAD_DV), BF16),
            pltpu.VMEM((2 * NP, SP, HEAD_DV), F32),
            pltpu.VMEM((2 * NP, SP, HEAD_DV), F32),
        ],
        compiler_params=pltpu.CompilerParams(
            dimension_semantics=("arbitrary",) * 4, vmem_limit_bytes=VMEM_LIMIT),
        name="scan_gla" if gla else "scan_ret",
    )(*args)


FF_CH = 512


def _out_kernel(*refs, layer, split_ctx):
    if split_ctx:
        ya_ref, yb_ref, yac_ref, ybc_ref, x_ref, ctx_ref, mod_ref, ln_ref, wo_ref, w1_ref, w2_ref, o_ref = refs
        is_ctx = pl.program_id(0) == N_LAT_TILES
        x = jnp.where(is_ctx, ctx_ref[...], x_ref[...])
        ya = jnp.where(is_ctx, yac_ref[...], ya_ref[...])
        yb = jnp.where(is_ctx, ybc_ref[...], yb_ref[...])
    else:
        ya_ref, yb_ref, x_ref, mod_ref, ln_ref, wo_ref, w1_ref, w2_ref, o_ref = refs
        x = x_ref[...]
        ya = ya_ref[...]
        yb = yb_ref[...]
    g1 = mod_ref[:, 2 * D:3 * D]
    sh2 = mod_ref[:, 3 * D:4 * D]
    sc2 = mod_ref[:, 4 * D:5 * D]
    g2 = mod_ref[:, 5 * D:6 * D]
    ln_g0 = ln_ref[2 * layer:2 * layer + 1, :]
    ln_g1 = ln_ref[2 * layer + 1:2 * layer + 2, :]
    ln_b0 = ln_ref[2 * DEPTH + 2 * layer:2 * DEPTH + 2 * layer + 1, :]
    ln_b1 = ln_ref[2 * DEPTH + 2 * layer + 1:2 * DEPTH + 2 * layer + 2, :]

    half = D // 2
    y = _dot(ya, wo_ref[0:half, :]) + _dot(yb, wo_ref[half:D, :])
    x1 = _layer_norm(ALPHA * x + g1 * y, ln_g0, ln_b0)
    h2 = (x1 * (1.0 + sc2) + sh2).astype(BF16)
    acc = jnp.zeros((TM, D), F32)
    for c in range(D_FF // FF_CH):
        cols = slice(c * FF_CH, (c + 1) * FF_CH)
        hc = jnp.maximum(_dot(h2, w1_ref[:, cols]), 0.0)
        acc = acc + _dot((hc * hc).astype(BF16), w2_ref[cols, :])
    o_ref[...] = _layer_norm(ALPHA * x1 + g2 * acc, ln_g1, ln_b1)


def _out_mlp(ys, xs, ctx2, mods, ln, wo, w1, w2, *, layer, n_tiles, split_ctx):
    half = D // 2
    lat_row = lambda t: (jnp.minimum(t, N_LAT_TILES - 1), 0)
    if split_ctx:
        ya, yb, yac, ybc = ys
        in_specs = [pl.BlockSpec((TM, half), lat_row), pl.BlockSpec((TM, half), lat_row),
                    _full_spec((B * L, half)), _full_spec((B * L, half)),
                    pl.BlockSpec((TM, D), lat_row), _full_spec((B * L, D))]
        args = [ya, yb, yac, ybc, xs, ctx2]
    else:
        (att,) = ys
        in_specs = [pl.BlockSpec((TM, half), lambda t: (t, 0)), pl.BlockSpec((TM, half), lambda t: (t, 1)),
                    pl.BlockSpec((TM, D), lambda t: (t, 0))]
        args = [att, att, xs]
    in_specs += [_mod_spec(layer), _full_spec(ln.shape), _full_spec(wo.shape),
                 _full_spec(w1.shape), _full_spec(w2.shape)]
    args += [mods, ln, wo, w1, w2]
    return pl.pallas_call(
        functools.partial(_out_kernel, layer=layer, split_ctx=split_ctx),
        grid=(n_tiles,),
        in_specs=in_specs,
        out_specs=pl.BlockSpec((TM, D), lambda t: (t, 0)),
        out_shape=jax.ShapeDtypeStruct((n_tiles * TM, D), F32),
        compiler_params=pltpu.CompilerParams(
            dimension_semantics=("arbitrary",), vmem_limit_bytes=VMEM_LIMIT),
        name="out_mlp%d" % layer,
    )(*args)


def _in1_kernel(x_ref, mod_ref, cos_ref, sa_ref, sb_ref, wq_ref, wk_ref, wv_ref, q_o, k_o, v_o):
    sh1 = mod_ref[:, 0:D]
    sc1 = mod_ref[:, D:2 * D]
    h = (x_ref[...] * (1.0 + sc1) + sh1).astype(BF16)
    cos = cos_ref[...]
    sa = sa_ref[...]
    sb = sb_ref[...]

    def rope(u):
        return u * cos + pltpu.roll(u, 128 - 16, 1) * sa + pltpu.roll(u, 16, 1) * sb

    q = _dot(h, wq_ref[...]) * (ATT_DH ** -0.5)
    for i in range(D // 128):
        q_o[:, i * 128:(i + 1) * 128] = rope(q[:, i * 128:(i + 1) * 128]).astype(BF16)
    k = _dot(h, wk_ref[...])
    for i in range(ATT_KVW // 128):
        k_o[:, i * 128:(i + 1) * 128] = rope(k[:, i * 128:(i + 1) * 128]).astype(BF16)
    v_o[...] = _dot(h, wv_ref[...]).astype(BF16)


def _in_proj1(xs, mods, cos_t, sa_t, sb_t, wq, wk, wv):
    tab = lambda t: (jnp.where(t == N_LAT_TILES, TILES_PER_BATCH, t % TILES_PER_BATCH), 0)
    row = lambda t: (t, 0)
    return pl.pallas_call(
        _in1_kernel,
        grid=(N_TILES,),
        in_specs=[
            pl.BlockSpec((TM, D), row), _mod_spec(1),
            pl.BlockSpec((TM, 128), tab), pl.BlockSpec((TM, 128), tab), pl.BlockSpec((TM, 128), tab),
            _full_spec(wq.shape), _full_spec(wk.shape), _full_spec(wv.shape),
        ],
        out_specs=[pl.BlockSpec((TM, D), row), pl.BlockSpec((TM, ATT_KVW), row), pl.BlockSpec((TM, ATT_KVW), row)],
        out_shape=[jax.ShapeDtypeStruct((ROWS, D), BF16), jax.ShapeDtypeStruct((ROWS, ATT_KVW), BF16),
                   jax.ShapeDtypeStruct((ROWS, ATT_KVW), BF16)],
        compiler_params=pltpu.CompilerParams(
            dimension_semantics=("arbitrary",), vmem_limit_bytes=VMEM_LIMIT),
        name="in_proj1",
    )(xs, mods, cos_t, sa_t, sb_t, wq, wk, wv)


def _attn_kernel(sink_ref, q_ref, k_ref, v_ref, kc_ref, vc_ref, o_ref):
    n = pl.program_id(1)
    rq = lax.broadcasted_iota(jnp.int32, (ATT_GROUP * QB, KWIN), 0) & (QB - 1)
    ck = lax.broadcasted_iota(jnp.int32, (ATT_GROUP * QB, KWIN), 1)
    rel = ck - rq
    grp = lax.broadcasted_iota(jnp.int32, (ATT_GROUP * QB, 1), 0) // QB

    def block(i, carry):
        n0 = (n * (TQ // QB) + i) * QB
        start = pl.multiple_of(jnp.clip(n0 - QB, 0, T - KWIN), QB)
        d = rel + (start - n0)
        valid = (d >= -WINDOW) & (d <= WINDOW)
        qrows = pl.ds(pl.multiple_of(i * QB, QB), QB)
        for kh in range(ATT_KVH):
            kcols = slice(kh * ATT_DH, (kh + 1) * ATT_DH)
            qg = jnp.concatenate(
                [q_ref[qrows, (kh * ATT_GROUP + g) * ATT_DH:(kh * ATT_GROUP + g + 1) * ATT_DH]
                 for g in range(ATT_GROUP)], axis=0)
            sink = jnp.zeros((ATT_GROUP * QB, 1), F32)
            for g in range(ATT_GROUP):
                sink = jnp.where(grp == g, sink_ref[kh * ATT_GROUP + g], sink)
            kw = k_ref[pl.ds(start, KWIN), kcols]
            vw = v_ref[pl.ds(start, KWIN), kcols]
            s_loc = jnp.where(valid, _dot_nt(qg, kw), -jnp.inf)
            s_ctx = _dot_nt(qg, kc_ref[:, kcols])
            m = jnp.maximum(jnp.maximum(jnp.max(s_loc, axis=-1, keepdims=True),
                                        jnp.max(s_ctx, axis=-1, keepdims=True)), sink)
            e_loc = jnp.exp(s_loc - m)
            e_ctx = jnp.exp(s_ctx - m)
            denom = (jnp.sum(e_loc, axis=-1, keepdims=True) + jnp.sum(e_ctx, axis=-1, keepdims=True)
                     + jnp.exp(sink - m))
            o = (_dot(e_loc.astype(BF16), vw) + _dot(e_ctx.astype(BF16), vc_ref[:, kcols])) / denom
            for g in range(ATT_GROUP):
                hcols = slice((kh * ATT_GROUP + g) * ATT_DH, (kh * ATT_GROUP + g + 1) * ATT_DH)
                o_ref[qrows, hcols] = o[g * QB:(g + 1) * QB].astype(BF16)
        return carry

    lax.fori_loop(0, TQ // QB, block, 0)


def _attention(sink, q, k, v):
    nq = T // TQ
    return pl.pallas_call(
        _attn_kernel,
        grid=(B, nq),
        in_specs=[
            pl.BlockSpec(memory_space=pltpu.SMEM),
            pl.BlockSpec((TQ, D), lambda b, n: (b * nq + n, 0)),
            pl.BlockSpec((T, ATT_KVW), lambda b, n: (b, 0)),
            pl.BlockSpec((T, ATT_KVW), lambda b, n: (b, 0)),
            pl.BlockSpec((L, ATT_KVW), lambda b, n: (LAT_ROWS // L + b, 0)),
            pl.BlockSpec((L, ATT_KVW), lambda b, n: (LAT_ROWS // L + b, 0)),
        ],
        out_specs=pl.BlockSpec((TQ, D), lambda b, n: (b * nq + n, 0)),
        out_shape=jax.ShapeDtypeStruct((LAT_ROWS, D), BF16),
        compiler_params=pltpu.CompilerParams(
            dimension_semantics=("arbitrary", "arbitrary"), vmem_limit_bytes=VMEM_LIMIT),
        name="window_attn",
    )(sink, q, k, v, k, v)


def _rope_tables():
    half = ATT_DH // 2
    inv_freq = ROPE_BASE ** (-jnp.arange(0, half, 2, dtype=F32) / half)
    pos = jnp.arange(T)
    ang_r = (pos // GRID_W).astype(F32)[:, None] * inv_freq[None, :]
    ang_c = (pos % GRID_W).astype(F32)[:, None] * inv_freq[None, :]
    zero = jnp.zeros_like(ang_r)
    cos = jnp.concatenate([jnp.cos(ang_r), jnp.cos(ang_r), jnp.cos(ang_c), jnp.cos(ang_c)], axis=1)
    sa = jnp.concatenate([-jnp.sin(ang_r), zero, -jnp.sin(ang_c), zero], axis=1)
    sb = jnp.concatenate([zero, jnp.sin(ang_r), zero, jnp.sin(ang_c)], axis=1)
    ident = lambda a, fill: jnp.concatenate([a, jnp.full((TM, ATT_DH), fill, F32)], axis=0)
    tile2 = lambda a: jnp.concatenate([a, a], axis=1)
    return tile2(ident(cos, 1.0)), tile2(ident(sa, 0.0)), tile2(ident(sb, 0.0))


def kernel(x, c, ctx, c_ctx, w_mod, b_mod, ln_g, ln_b, mlp_w1, mlp_w2, ev_w_in, ev_ret_theta, ev_gla_gk_w,
           ev_gla_gk_b, ev_gla_norm_g, ev_w_out, od_w_qkv, od_sink, od_w_out):
    x2 = x.reshape(LAT_ROWS, D)
    ctx2 = ctx.reshape(B * L, D)

    cs = jnp.concatenate([c, c_ctx[None, :], jnp.zeros((8 - B - 1, D), F32)], axis=0)
    mods = _modulation(cs, w_mod, b_mod).reshape(DEPTH * 8, 1, 6 * D)
    ln = jnp.concatenate([ln_g.reshape(2 * DEPTH, D), ln_b.reshape(2 * DEPTH, D)], axis=0)

    w_in = ev_w_in[0]
    off = [0]
    for s in (QK_W, QK_W, V_W, V_W, QK_W, QK_W, V_W, V_W, 2 * GATE_RANK):
        off.append(off[-1] + s)
    piece = lambda i: w_in[:, off[i]:off[i + 1]].astype(BF16)
    gk_w = ev_gla_gk_w[0]
    zeros = jnp.zeros((GATE_RANK, QK_W), F32)
    g2 = jnp.concatenate([jnp.concatenate([gk_w[0], zeros], axis=1),
                          jnp.concatenate([zeros, gk_w[1]], axis=1)], axis=0)
    w0 = dict(wq_a=piece(0), wkT_a=piece(1).T, wv_a=piece(2), wg_a=piece(3),
              wq_b=piece(4), wkT_b=piece(5).T, wv_b=piece(6), wg_b=piece(7), wlr=piece(8),
              g2T=g2.T.astype(BF16), gbc=ev_gla_gk_b[0].reshape(2 * QK_W, 1))
    qa, kaT, va, ga, qb, kbT, vb, gbv, lfT, lbT = _in_proj0(x2, ctx2, mods, w0)

    theta = ev_ret_theta[0]
    th_row = jnp.repeat(theta, HEAD_DK, axis=1).reshape(2, 2, 2 * HEAD_DK).transpose(1, 0, 2)
    th_col = th_row.transpose(0, 2, 1)
    y_ret, yc_ret = _scan_group(qa, kaT, va, ga, (th_row, th_col), gla=False)
    y_gla, yc_gla = _scan_group(qb, kbT, vb, gbv, (lfT, lbT, ev_gla_norm_g[0].reshape(1, HEAD_DV)), gla=True)

    xs = _out_mlp((y_ret, y_gla, yc_ret, yc_gla), x2, ctx2, mods, ln,
                  ev_w_out[0].astype(BF16), mlp_w1[0].astype(BF16), mlp_w2[0].astype(BF16),
                  layer=0, n_tiles=N_TILES, split_ctx=True)

    wqkv = od_w_qkv[0].astype(BF16)
    cos_t, sa_t, sb_t = _rope_tables()
    q1, k1, v1 = _in_proj1(xs, mods, cos_t, sa_t, sb_t, wqkv[:, 0:D], wqkv[:, D:D + ATT_KVW],
                           wqkv[:, D + ATT_KVW:D + 2 * ATT_KVW])
    att = _attention(od_sink[0], q1, k1, v1)
    out = _out_mlp((att,), xs, None, mods, ln,
                   od_w_out[0].astype(BF16), mlp_w1[1].astype(BF16), mlp_w2[1].astype(BF16),
                   layer=1, n_tiles=N_LAT_TILES, split_ctx=False)
    return out.reshape(B, T, D)
```

```python
import functools

import jax
import jax.numpy as jnp
from jax import lax
from jax.experimental import pallas as pl
from jax.experimental.pallas import tpu as pltpu

F32 = jnp.float32
BF16 = jnp.bfloat16

D = 1024
B = 2
T = 8192
L = 256
DEPTH = 2
GRID_W = 64
D_FF = 4 * D
HEAD_DV = 128
HEAD_DK = 64
GATE_RANK = 16
GATE_TAU = 16.0
QK_W = 256
V_W = 512
ATT_DH = 64
ATT_QH = 16
ATT_KVH = 4
ATT_GROUP = 4
ATT_KVW = ATT_KVH * ATT_DH
WINDOW = 128
ROPE_BASE = 10000.0
ALPHA = (2.0 * DEPTH) ** 0.25
LN_EPS = 1e-5
RMS_EPS = 1e-6

TM = 512
LAT_ROWS = B * T
ROWS = LAT_ROWS + B * L
N_LAT_TILES = LAT_ROWS // TM
N_TILES = ROWS // TM
TILES_PER_BATCH = T // TM

CH = 64
SP = 2 * CH
TB = 1024
NP = TB // SP
NP_CTX = L // SP
NBLK = T // TB
NP_ALL = NP_CTX + T // SP
N_SLABS = ROWS // SP
KV_UNROLL = 8
OUT_UNROLL = 4

TQ = 512
QB = 128
KWIN = 3 * QB
LOG2E = 1.4426950408889634

VMEM_LIMIT = 56 * 1024 * 1024


def _dot(a, b):
    return jnp.dot(a, b, preferred_element_type=F32)


def _dot_nt(a, b):
    return lax.dot_general(a, b, (((1,), (1,)), ((), ())), preferred_element_type=F32)


def _full_spec(shape):
    nd = len(shape)
    return pl.BlockSpec(shape, lambda *_: (0,) * nd, pipeline_mode=pl.Buffered(1))


def _mod_row(t):
    return jnp.minimum(t // TILES_PER_BATCH, B)


def _mod_spec(layer):
    return pl.BlockSpec((None, 1, 6 * D), lambda t: (layer * 8 + _mod_row(t), 0, 0))


def _layer_norm(x, g, b):
    mu = jnp.mean(x, axis=-1, keepdims=True)
    xc = x - mu
    var = jnp.mean(xc * xc, axis=-1, keepdims=True)
    return xc * lax.rsqrt(var + LN_EPS) * g + b


def _log_sigmoid(z):
    return jnp.minimum(z, 0.0) - jnp.log1p(jnp.exp(-jnp.abs(z)))


MOD_TN = 1536


def _mod_kernel(c_ref, w_ref, b_ref, o_ref):
    s = jax.nn.silu(c_ref[...])
    o_ref[...] = jnp.dot(s, w_ref[...], precision=lax.Precision.HIGHEST,
                         preferred_element_type=F32) + b_ref[...]


def _modulation(cs, w_mod, b_mod):
    return pl.pallas_call(
        _mod_kernel,
        grid=(DEPTH, 6 * D // MOD_TN),
        in_specs=[
            pl.BlockSpec((8, D), lambda i, n: (0, 0)),
            pl.BlockSpec((None, D, MOD_TN), lambda i, n: (i, 0, n)),
            pl.BlockSpec((None, 1, MOD_TN), lambda i, n: (i, 0, n)),
        ],
        out_specs=pl.BlockSpec((None, 8, MOD_TN), lambda i, n: (i, 0, n)),
        out_shape=jax.ShapeDtypeStruct((DEPTH, 8, 6 * D), F32),
        compiler_params=pltpu.CompilerParams(
            dimension_semantics=("arbitrary", "arbitrary"), vmem_limit_bytes=VMEM_LIMIT),
        name="modulation",
    )(cs, w_mod, b_mod.reshape(DEPTH, 1, 6 * D))


def _in0_kernel(x_ref, ctx_ref, mod_ref,
                wq_a, wkT_a, wv_a, wg_a, wq_b, wkT_b, wv_b, wg_b, wlr, g2T, gbc,
                qa_o, kaT_o, va_o, ga_o, qb_o, kbT_o, vb_o, gb_o, lfT_o, lbT_o):
    t = pl.program_id(0)
    xt = jnp.where(t == N_LAT_TILES, ctx_ref[...], x_ref[...])
    sh1 = mod_ref[:, 0:D]
    sc1 = mod_ref[:, D:2 * D]
    h = (xt * (1.0 + sc1) + sh1).astype(BF16)
    qk_scale = HEAD_DK ** -0.5

    def put_slabs(o_ref, val):
        for i in range(TM // SP):
            o_ref[i] = val[:, i * SP:(i + 1) * SP].astype(o_ref.dtype)

    qa_o[...] = _dot(h, wq_a[...]).astype(BF16)
    put_slabs(kaT_o, _dot_nt(wkT_a[...], h) * qk_scale)
    va_o[...] = _dot(h, wv_a[...]).astype(BF16)
    ga_o[...] = jax.nn.silu(_dot(h, wg_a[...])).astype(BF16)

    qb_o[...] = (_dot(h, wq_b[...]) * qk_scale).astype(BF16)
    put_slabs(kbT_o, _dot_nt(wkT_b[...], h))
    vb_o[...] = _dot(h, wv_b[...]).astype(BF16)
    gb_o[...] = jax.nn.silu(_dot(h, wg_b[...])).astype(BF16)

    lr = _dot(h, wlr[...]).astype(BF16)
    lsT = _log_sigmoid(_dot_nt(g2T[...], lr) + gbc[...]) * (1.0 / GATE_TAU)
    put_slabs(lfT_o, lsT[0:QK_W, :])
    put_slabs(lbT_o, lsT[QK_W:2 * QK_W, :])


def _in_proj0(x2, ctx2, mods, w):
    row = lambda t: (t, 0)
    slab = lambda t: (t, 0, 0)
    nsl = TM // SP
    row_out = lambda width: (jax.ShapeDtypeStruct((ROWS, width), BF16), pl.BlockSpec((TM, width), row))
    slab_out = lambda dt: (jax.ShapeDtypeStruct((N_SLABS, QK_W, SP), dt), pl.BlockSpec((nsl, QK_W, SP), slab))
    outs = [row_out(QK_W), slab_out(BF16), row_out(V_W), row_out(V_W),
            row_out(QK_W), slab_out(BF16), row_out(V_W), row_out(V_W),
            slab_out(F32), slab_out(F32)]
    weights = [w["wq_a"], w["wkT_a"], w["wv_a"], w["wg_a"], w["wq_b"], w["wkT_b"], w["wv_b"], w["wg_b"],
               w["wlr"], w["g2T"], w["gbc"]]
    in_specs = [
        pl.BlockSpec((TM, D), lambda t: (jnp.minimum(t, N_LAT_TILES - 1), 0)),
        _full_spec((B * L, D)),
        _mod_spec(0),
    ] + [_full_spec(a.shape) for a in weights]
    return pl.pallas_call(
        _in0_kernel,
        grid=(N_TILES,),
        in_specs=in_specs,
        out_specs=[o[1] for o in outs],
        out_shape=[o[0] for o in outs],
        compiler_params=pltpu.CompilerParams(
            dimension_semantics=("arbitrary",), vmem_limit_bytes=VMEM_LIMIT),
        name="in_proj0",
    )(x2, ctx2, mods, *weights)


def _hi_lo(a):
    hi = a.astype(BF16)
    lo = (a - hi.astype(F32)).astype(BF16)
    return hi, lo


def _dot_hi_lo(a, w):
    hi, lo = _hi_lo(a)
    return _dot(hi, w) + _dot(lo, w)


def _dot_nt_hi_lo(w, a):
    hi, lo = _hi_lo(a)
    return _dot_nt(w, hi) + _dot_nt(w, lo)


def _chunk_diag(kv, c):
    r0 = c * 2 * HEAD_DK
    return jnp.concatenate([kv[r0:r0 + HEAD_DK, 0:HEAD_DV],
                            kv[r0 + HEAD_DK:r0 + 2 * HEAD_DK, HEAD_DV:2 * HEAD_DV]], axis=0)


def _scan_kernel(*refs, gla):
    if gla:
        (q_ref, kT_ref, v_ref, g_ref, lfT_ref, lbT_ref,
         qc_ref, kTc_ref, vc_ref, gc_ref, lfTc_ref, lbTc_ref, ng_ref,
         y_ref, yc_ref, s_ref, r_ref, rst_ref, sst_ref, kv_ref, dec_ref) = refs
        lat = (q_ref, kT_ref, v_ref, g_ref, lfT_ref, lbT_ref)
        cxt = (qc_ref, kTc_ref, vc_ref, gc_ref, lfTc_ref, lbTc_ref)
    else:
        (q_ref, kT_ref, v_ref, g_ref, qc_ref, kTc_ref, vc_ref, gc_ref, thr_ref, thc_ref,
         y_ref, yc_ref, s_ref, r_ref, rst_ref, sst_ref, kv_ref, dec_ref) = refs
        lat = (q_ref, kT_ref, v_ref, g_ref, None, None)
        cxt = (qc_ref, kTc_ref, vc_ref, gc_ref, None, None)

    phase = pl.program_id(2)
    j = pl.program_id(3)

    ri = lax.broadcasted_iota(jnp.int32, (SP, SP), 0)
    ci = lax.broadcasted_iota(jnp.int32, (SP, SP), 1)
    same = (ri // CH) == (ci // CH)
    first_lane = ci < CH
    head_a = ci < HEAD_DK

    if gla:
        as_w = lambda m: m.astype(BF16)
        tot = jnp.concatenate([jnp.broadcast_to(ri < CH, (SP, SP)), jnp.broadcast_to(ri >= CH, (SP, SP))], axis=1)
        w_end_f = jnp.concatenate([as_w(same & (ri > ci)), as_w(tot)], axis=1)
        w_end_b = jnp.concatenate([as_w(same & (ri < ci)), as_w(tot)], axis=1)
        w_cum_f = as_w(same & (ri <= ci))
        w_cum_b = as_w(same & (ri >= ci))
    else:
        lg_row = jnp.log1p(-jnp.exp(thr_ref[...]))
        lg_col = jnp.log1p(-jnp.exp(thc_ref[...]))
        it = (ci % CH).astype(F32)
        ir = (ri % CH).astype(F32)
        ret_end_f = jnp.exp((CH - 1.0 - it) * lg_col[:, 0:1])
        ret_end_b = jnp.exp(it * lg_col[:, 1:2])
        ret_dec_f = jnp.exp(jnp.broadcast_to(CH * lg_col[:, 0:1], (SP, SP)))
        ret_dec_b = jnp.exp(jnp.broadcast_to(CH * lg_col[:, 1:2], (SP, SP)))
        ret_ebTi = jnp.exp(-(it + 1.0) * lg_col[:, 0:1])
        ret_erTi = jnp.exp(-(CH - it) * lg_col[:, 1:2])
        ret_eb = jnp.exp((ir + 1.0) * lg_row[0:1, :])
        ret_er = jnp.exp((CH - ir) * lg_row[1:2, :])

    def kv_stage(blk, n, fwd):
        _, kT_r, v_r, _, lfT_r, lbT_r = blk

        def body(p, carry):
            kT = kT_r[p].astype(F32)
            v = v_r[pl.ds(pl.multiple_of(p * SP, SP), SP), :]
            if gla:
                res = _dot_hi_lo((lfT_r if fwd else lbT_r)[p], w_end_f if fwd else w_end_b)
                e_end = jnp.exp(res[:, 0:SP])
                dec0 = jnp.exp(res[:, SP:2 * SP])
                dec1 = jnp.exp(res[:, 2 * SP:3 * SP])
            else:
                e_end = ret_end_f if fwd else ret_end_b
                dec0 = dec1 = ret_dec_f if fwd else ret_dec_b
            ke = kT * e_end
            lhs = jnp.concatenate([jnp.where(first_lane, ke, 0.0), jnp.where(first_lane, 0.0, ke)],
                                  axis=0).astype(BF16)
            kv = _dot(lhs, v)
            kv_ref[2 * p] = _chunk_diag(kv, 0)
            kv_ref[2 * p + 1] = _chunk_diag(kv, 1)
            dec_ref[2 * p] = dec0
            dec_ref[2 * p + 1] = dec1
            return carry

        lax.fori_loop(0, n, body, 0, unroll=min(n, KV_UNROLL))

    def phase0_block(blk, n, slot0):
        kv_stage(blk, n, fwd=False)

        def body(i, r_state):
            p = n - 1 - i
            rst_ref[slot0 + p, :, HEAD_DV:2 * HEAD_DV] = r_state.astype(BF16)
            r_state = dec_ref[2 * p + 1] * r_state + kv_ref[2 * p + 1]
            rst_ref[slot0 + p, :, 0:HEAD_DV] = r_state.astype(BF16)
            return dec_ref[2 * p] * r_state + kv_ref[2 * p]

        r_ref[...] = lax.fori_loop(0, n, body, r_ref[...])

    def phase1_block(blk, n, slot0, out_ref):
        q_r, kT_r, v_r, g_r, lfT_r, lbT_r = blk
        kv_stage(blk, n, fwd=True)

        def rec(p, s_state):
            sst_ref[p, :, 0:HEAD_DV] = s_state.astype(BF16)
            s_state = dec_ref[2 * p] * s_state + kv_ref[2 * p]
            sst_ref[p, :, HEAD_DV:2 * HEAD_DV] = s_state.astype(BF16)
            return dec_ref[2 * p + 1] * s_state + kv_ref[2 * p + 1]

        s_ref[...] = lax.fori_loop(0, n, rec, s_ref[...])

        r2 = lax.broadcasted_iota(jnp.int32, (2 * SP, SP), 0) % SP
        c2 = lax.broadcasted_iota(jnp.int32, (2 * SP, SP), 1)
        same2 = (r2 // CH) == (c2 // CH)
        mask_f = same2 & (r2 >= c2)
        mask_b = same2 & (r2 < c2)

        def body(p, carry):
            rows = pl.ds(pl.multiple_of(p * SP, SP), SP)
            q = q_r[rows, :].astype(F32)
            kT = kT_r[p].astype(F32)
            v = v_r[rows, :]
            if gla:
                lfT = lfT_r[p]
                lbT = lbT_r[p]
                ebTi = jnp.exp(-_dot_hi_lo(lfT, w_cum_f))
                erTi = jnp.exp(-_dot_hi_lo(lbT, w_cum_b))
                e_b = jnp.exp(_dot_nt_hi_lo(w_cum_b, lfT))
                e_r = jnp.exp(_dot_nt_hi_lo(w_cum_f, lbT))
            else:
                ebTi, erTi, e_b, e_r = ret_ebTi, ret_erTi, ret_eb, ret_er
            qf = q * e_b
            qb = q * e_r
            lhs_f = jnp.concatenate([jnp.where(head_a, qf, 0.0), jnp.where(head_a, 0.0, qf)], axis=0).astype(BF16)
            lhs_b = jnp.concatenate([jnp.where(head_a, qb, 0.0), jnp.where(head_a, 0.0, qb)], axis=0).astype(BF16)
            sc_f = _dot(lhs_f, (kT * ebTi).astype(BF16))
            sc_b = _dot(lhs_b, (kT * erTi).astype(BF16))
            pm = jnp.where(mask_f, sc_f, jnp.where(mask_b, sc_b, 0.0)).astype(BF16)
            states = jnp.concatenate([sst_ref[p], rst_ref[slot0 + p]], axis=0)
            o_int = _dot(jnp.concatenate([lhs_f, lhs_b], axis=1), states)
            o_a = _dot(pm[0:SP], v[:, 0:HEAD_DV]) + jnp.concatenate(
                [o_int[0:CH, 0:HEAD_DV], o_int[CH:SP, HEAD_DV:2 * HEAD_DV]], axis=0)
            o_b = _dot(pm[SP:2 * SP], v[:, HEAD_DV:2 * HEAD_DV]) + jnp.concatenate(
                [o_int[SP:SP + CH, 0:HEAD_DV], o_int[SP + CH:2 * SP, HEAD_DV:2 * HEAD_DV]], axis=0)

            def nrm(o):
                y = o * lax.rsqrt(jnp.mean(o * o, axis=-1, keepdims=True) + RMS_EPS)
                return y * ng_ref[...] if gla else y

            y = jnp.concatenate([nrm(o_a), nrm(o_b)], axis=1) * g_r[rows, :].astype(F32)
            out_ref[rows, :] = y.astype(BF16)
            return carry

        lax.fori_loop(0, n, body, 0, unroll=min(n, OUT_UNROLL))

    @pl.when(phase == 0)
    def _():
        @pl.when(j == 0)
        def _():
            r_ref[...] = jnp.zeros_like(r_ref)
            phase0_block(cxt, NP_CTX, 0)

        phase0_block(lat, NP, NP_CTX + (NBLK - 1 - j) * NP)

    @pl.when(phase == 1)
    def _():
        @pl.when(j == 0)
        def _():
            s_ref[...] = jnp.zeros_like(s_ref)
            phase1_block(cxt, NP_CTX, 0, yc_ref)

        phase1_block(lat, NP, NP_CTX + j * NP, y_ref)


def _scan_group(q, kT, v, g, extra, *, gla):
    def blk(b, ph, j, used_in_phase0):
        jj = jnp.where(ph == 0, NBLK - 1 - j, j)
        if not used_in_phase0:
            jj = jnp.where(ph == 0, 0, jj)
        return b * NBLK + jj

    def lat_specs(used0):
        return dict(
            row=lambda w: pl.BlockSpec((TB, w), lambda b, p, ph, j: (blk(b, ph, j, used0), p)),
            slab=pl.BlockSpec((NP, SP, SP), lambda b, p, ph, j: (blk(b, ph, j, used0), p, 0)))

    ctx_row = lambda w: pl.BlockSpec((L, w), lambda b, p, ph, j: (LAT_ROWS // L + b, p))
    ctx_slab = pl.BlockSpec((NP_CTX, SP, SP), lambda b, p, ph, j: (LAT_ROWS // L + b, p, 0))
    used, unused = lat_specs(True), lat_specs(False)

    in_specs = [unused["row"](2 * HEAD_DK), used["slab"], used["row"](2 * HEAD_DV), unused["row"](2 * HEAD_DV)]
    ctx_specs = [ctx_row(2 * HEAD_DK), ctx_slab, ctx_row(2 * HEAD_DV), ctx_row(2 * HEAD_DV)]
    if gla:
        lfT, lbT, ng = extra
        in_specs += [unused["slab"], used["slab"]]
        ctx_specs += [ctx_slab, ctx_slab]
        args = (q, kT, v, g, lfT, lbT, q, kT, v, g, lfT, lbT, ng)
        in_specs = in_specs + ctx_specs + [pl.BlockSpec((1, HEAD_DV), lambda b, p, ph, j: (0, 0))]
    else:
        th_row, th_col = extra
        args = (q, kT, v, g, q, kT, v, g, th_row, th_col)
        in_specs = in_specs + ctx_specs + [
            pl.BlockSpec((None, 2, 2 * HEAD_DK), lambda b, p, ph, j: (p, 0, 0)),
            pl.BlockSpec((None, 2 * HEAD_DK, 2), lambda b, p, ph, j: (p, 0, 0)),
        ]
    return pl.pallas_call(
        functools.partial(_scan_kernel, gla=gla),
        grid=(B, 2, 2, NBLK),
        in_specs=in_specs,
        out_specs=[
            pl.BlockSpec((TB, 2 * HEAD_DV), lambda b, p, ph, j: (b * NBLK + jnp.where(ph == 0, 0, j), p)),
            pl.BlockSpec((L, 2 * HEAD_DV), lambda b, p, ph, j: (b, p)),
        ],
        out_shape=[jax.ShapeDtypeStruct((LAT_ROWS, V_W), BF16), jax.ShapeDtypeStruct((B * L, V_W), BF16)],
        scratch_shapes=[
            pltpu.VMEM((SP, HEAD_DV), F32),
            pltpu.VMEM((SP, HEAD_DV), F32),
            pltpu.VMEM((NP_ALL, SP, 2 * HEAD_DV), BF16),
            pltpu.VMEM((NP, SP, 2 * HEAD_DV), BF16),
            pltpu.VMEM((2 * NP, SP, HEAD_DV), F32),
            pltpu.VMEM((2 * NP, SP, HEAD_DV), F32),
        ],
        compiler_params=pltpu.CompilerParams(
            dimension_semantics=("arbitrary",) * 4, vmem_limit_bytes=VMEM_LIMIT),
        name="scan_gla" if gla else "scan_ret",
    )(*args)


FF_CH = 512


def _out_kernel(*refs, layer, split_ctx):
    if split_ctx:
        ya_ref, yb_ref, yac_ref, ybc_ref, x_ref, ctx_ref, mod_ref, ln_ref, wo_ref, w1_ref, w2_ref, o_ref = refs
        is_ctx = pl.program_id(0) == N_LAT_TILES
        x = jnp.where(is_ctx, ctx_ref[...], x_ref[...])
        ya = jnp.where(is_ctx, yac_ref[...], ya_ref[...])
        yb = jnp.where(is_ctx, ybc_ref[...], yb_ref[...])
    else:
        ya_ref, yb_ref, x_ref, mod_ref, ln_ref, wo_ref, w1_ref, w2_ref, o_ref = refs
        x = x_ref[...]
        ya = ya_ref[...]
        yb = yb_ref[...]
    g1 = mod_ref[:, 2 * D:3 * D]
    sh2 = mod_ref[:, 3 * D:4 * D]
    sc2 = mod_ref[:, 4 * D:5 * D]
    g2 = mod_ref[:, 5 * D:6 * D]
    ln_g0 = ln_ref[2 * layer:2 * layer + 1, :]
    ln_g1 = ln_ref[2 * layer + 1:2 * layer + 2, :]
    ln_b0 = ln_ref[2 * DEPTH + 2 * layer:2 * DEPTH + 2 * layer + 1, :]
    ln_b1 = ln_ref[2 * DEPTH + 2 * layer + 1:2 * DEPTH + 2 * layer + 2, :]

    half = D // 2
    y = _dot(ya, wo_ref[0:half, :]) + _dot(yb, wo_ref[half:D, :])
    x1 = _layer_norm(ALPHA * x + g1 * y, ln_g0, ln_b0)
    h2 = (x1 * (1.0 + sc2) + sh2).astype(BF16)
    acc = jnp.zeros((TM, D), F32)
    for c in range(D_FF // FF_CH):
        cols = slice(c * FF_CH, (c + 1) * FF_CH)
        hc = jnp.maximum(_dot(h2, w1_ref[:, cols]), 0.0)
        acc = acc + _dot((hc * hc).astype(BF16), w2_ref[cols, :])
    o_ref[...] = _layer_norm(ALPHA * x1 + g2 * acc, ln_g1, ln_b1)


def _out_mlp(ys, xs, ctx2, mods, ln, wo, w1, w2, *, layer, n_tiles, split_ctx):
    half = D // 2
    lat_row = lambda t: (jnp.minimum(t, N_LAT_TILES - 1), 0)
    if split_ctx:
        ya, yb, yac, ybc = ys
        in_specs = [pl.BlockSpec((TM, half), lat_row), pl.BlockSpec((TM, half), lat_row),
                    _full_spec((B * L, half)), _full_spec((B * L, half)),
                    pl.BlockSpec((TM, D), lat_row), _full_spec((B * L, D))]
        args = [ya, yb, yac, ybc, xs, ctx2]
    else:
        (att,) = ys
        in_specs = [pl.BlockSpec((TM, half), lambda t: (t, 0)), pl.BlockSpec((TM, half), lambda t: (t, 1)),
                    pl.BlockSpec((TM, D), lambda t: (t, 0))]
        args = [att, att, xs]
    in_specs += [_mod_spec(layer), _full_spec(ln.shape), _full_spec(wo.shape),
                 _full_spec(w1.shape), _full_spec(w2.shape)]
    args += [mods, ln, wo, w1, w2]
    return pl.pallas_call(
        functools.partial(_out_kernel, layer=layer, split_ctx=split_ctx),
        grid=(n_tiles,),
        in_specs=in_specs,
        out_specs=pl.BlockSpec((TM, D), lambda t: (t, 0)),
        out_shape=jax.ShapeDtypeStruct((n_tiles * TM, D), F32),
        compiler_params=pltpu.CompilerParams(
            dimension_semantics=("arbitrary",), vmem_limit_bytes=VMEM_LIMIT),
        name="out_mlp%d" % layer,
    )(*args)


def _in1_kernel(x_ref, mod_ref, cos_ref, sa_ref, sb_ref, cosT_ref, saT_ref, sbT_ref,
                wqT_ref, wk_ref, wvT_ref, qT_o, k_o, vT_o):
    sh1 = mod_ref[:, 0:D]
    sc1 = mod_ref[:, D:2 * D]
    h = (x_ref[...] * (1.0 + sc1) + sh1).astype(BF16)
    nsl = TM // SP

    cos, sa, sb = cos_ref[...], sa_ref[...], sb_ref[...]
    k = _dot(h, wk_ref[...])
    for i in range(ATT_KVW // 128):
        u = k[:, i * 128:(i + 1) * 128]
        r = u * cos + pltpu.roll(u, 128 - 16, 1) * sa + pltpu.roll(u, 16, 1) * sb
        k_o[:, i * 128:(i + 1) * 128] = r.astype(BF16)

    cosT, saT, sbT = cosT_ref[...], saT_ref[...], sbT_ref[...]
    qT = _dot_nt(wqT_ref[...], h) * (LOG2E * ATT_DH ** -0.5)
    for i in range(D // 128):
        u = qT[i * 128:(i + 1) * 128, :]
        r = (u * cosT + pltpu.roll(u, 128 - 16, 0) * saT + pltpu.roll(u, 16, 0) * sbT).astype(BF16)
        for s in range(nsl):
            qT_o[s, i * 128:(i + 1) * 128, :] = r[:, s * SP:(s + 1) * SP]

    vT = _dot_nt(wvT_ref[...], h).astype(BF16)
    for s in range(nsl):
        vT_o[s] = vT[:, s * SP:(s + 1) * SP]


def _in_proj1(xs, mods, tabs, tabsT, wqT, wk, wvT):
    tile = lambda t: jnp.where(t == N_LAT_TILES, TILES_PER_BATCH, t % TILES_PER_BATCH)
    row = lambda t: (t, 0)
    slab = lambda t: (t, 0, 0)
    nsl = TM // SP
    tab_spec = pl.BlockSpec((TM, 128), lambda t: (tile(t), 0))
    tabT_spec = pl.BlockSpec((128, TM), lambda t: (0, tile(t)))
    return pl.pallas_call(
        _in1_kernel,
        grid=(N_TILES,),
        in_specs=[pl.BlockSpec((TM, D), row), _mod_spec(1)] + [tab_spec] * 3 + [tabT_spec] * 3
                 + [_full_spec(wqT.shape), _full_spec(wk.shape), _full_spec(wvT.shape)],
        out_specs=[pl.BlockSpec((nsl, D, SP), slab), pl.BlockSpec((TM, ATT_KVW), row),
                   pl.BlockSpec((nsl, ATT_KVW, SP), slab)],
        out_shape=[jax.ShapeDtypeStruct((N_SLABS, D, SP), BF16), jax.ShapeDtypeStruct((ROWS, ATT_KVW), BF16),
                   jax.ShapeDtypeStruct((N_SLABS, ATT_KVW, SP), BF16)],
        compiler_params=pltpu.CompilerParams(
            dimension_semantics=("arbitrary",), vmem_limit_bytes=VMEM_LIMIT),
        name="in_proj1",
    )(xs, mods, *tabs, *tabsT, wqT, wk, wvT)


def _attn_kernel(sink_ref, qT_ref, k_ref, vT_ref, kc_ref, vcT_ref, o_ref, bias_ref):
    n = pl.program_id(1)
    nql = ATT_GROUP * QB
    lane_g = lax.broadcasted_iota(jnp.int32, (1, nql), 1) // QB
    zero_half = jnp.zeros((ATT_DH, nql), BF16)
    ones_rows = jnp.ones((16, KWIN + L), BF16)

    @pl.when((pl.program_id(0) == 0) & (n == 0))
    def _():
        kj = lax.broadcasted_iota(jnp.int32, (KWIN, QB), 0)
        qi = lax.broadcasted_iota(jnp.int32, (KWIN, QB), 1)
        for case, delta in enumerate((-QB, 0, -2 * QB)):
            d = kj - qi + delta
            bias_ref[case] = jnp.where((d >= -WINDOW) & (d <= WINDOW), 0.0, -jnp.inf)

    def scores(i, kh):
        n0 = (n * (TQ // QB) + i) * QB
        start = pl.multiple_of(jnp.clip(n0 - QB, 0, T - KWIN), QB)
        bias1 = bias_ref[jnp.where(n0 == 0, 1, jnp.where(n0 == T - QB, 2, 0))]
        bias = jnp.concatenate([bias1] * ATT_GROUP, axis=1)
        pair = slice((kh // 2) * 128, (kh // 2 + 1) * 128)
        qT = jnp.concatenate(
            [qT_ref[i, (kh * ATT_GROUP + g) * ATT_DH:(kh * ATT_GROUP + g + 1) * ATT_DH, :]
             for g in range(ATT_GROUP)], axis=1)
        q_pad = jnp.concatenate([zero_half, qT] if kh % 2 else [qT, zero_half], axis=0)
        s_loc = _dot(k_ref[pl.ds(start, KWIN), pair], q_pad) + bias
        s_ctx = _dot(kc_ref[:, pair], q_pad)
        return s_loc, s_ctx, start // SP

    def finish(i, kh, s_loc, s_ctx, slab0):
        drows = slice(kh * ATT_DH, (kh + 1) * ATT_DH)
        sink = jnp.zeros((1, nql), F32)
        for g in range(ATT_GROUP):
            sink = jnp.where(lane_g == g, sink_ref[kh * ATT_GROUP + g] * LOG2E, sink)
        m = jnp.maximum(jnp.maximum(jnp.max(s_loc, axis=0, keepdims=True),
                                    jnp.max(s_ctx, axis=0, keepdims=True)), sink)
        pT = jnp.concatenate([jnp.exp2(s_loc - m).astype(BF16),
                              jnp.exp2(s_ctx - m).astype(BF16)], axis=0)
        vT = jnp.concatenate([vT_ref[slab0 + t, drows, :] for t in range(KWIN // SP)]
                             + [vcT_ref[t, drows, :] for t in range(L // SP)], axis=1)
        o_ext = _dot(jnp.concatenate([vT, ones_rows], axis=0), pT)
        denom = o_ext[ATT_DH:ATT_DH + 1, :] + jnp.exp2(sink - m)
        oT = o_ext[0:ATT_DH, :] / denom
        for g in range(0, ATT_GROUP, 2):
            two = jnp.concatenate([oT[:, g * QB:(g + 1) * QB], oT[:, (g + 1) * QB:(g + 2) * QB]], axis=0)
            c0 = (kh * ATT_GROUP + g) * ATT_DH
            o_ref[i * QB:(i + 1) * QB, c0:c0 + 2 * ATT_DH] = two.T.astype(BF16)

    items = [(i, kh) for i in range(TQ // QB) for kh in range(ATT_KVH)]
    pending = scores(*items[0])
    for t, item in enumerate(items):
        nxt = scores(*items[t + 1]) if t + 1 < len(items) else None
        finish(*item, *pending)
        pending = nxt


def _attention(sink, qT, k, vT):
    nq = T // TQ
    return pl.pallas_call(
        _attn_kernel,
        grid=(B, nq),
        in_specs=[
            pl.BlockSpec(memory_space=pltpu.SMEM),
            pl.BlockSpec((TQ // SP, D, SP), lambda b, n: (b * nq + n, 0, 0)),
            pl.BlockSpec((T, ATT_KVW), lambda b, n: (b, 0)),
            pl.BlockSpec((T // SP, ATT_KVW, SP), lambda b, n: (b, 0, 0)),
            pl.BlockSpec((L, ATT_KVW), lambda b, n: (LAT_ROWS // L + b, 0)),
            pl.BlockSpec((L // SP, ATT_KVW, SP), lambda b, n: (LAT_ROWS // L + b, 0, 0)),
        ],
        out_specs=pl.BlockSpec((TQ, D), lambda b, n: (b * nq + n, 0)),
        out_shape=jax.ShapeDtypeStruct((LAT_ROWS, D), BF16),
        compiler_params=pltpu.CompilerParams(
            dimension_semantics=("arbitrary", "arbitrary"), vmem_limit_bytes=VMEM_LIMIT),
        scratch_shapes=[pltpu.VMEM((3, KWIN, QB), F32)],
        name="window_attn",
    )(sink, qT, k, vT, k, vT)


def _rope_tables():
    half = ATT_DH // 2
    inv_freq = ROPE_BASE ** (-jnp.arange(0, half, 2, dtype=F32) / half)
    pos = jnp.arange(T)
    ang_r = (pos // GRID_W).astype(F32)[:, None] * inv_freq[None, :]
    ang_c = (pos % GRID_W).astype(F32)[:, None] * inv_freq[None, :]
    zero = jnp.zeros_like(ang_r)
    cos = jnp.concatenate([jnp.cos(ang_r), jnp.cos(ang_r), jnp.cos(ang_c), jnp.cos(ang_c)], axis=1)
    sa = jnp.concatenate([-jnp.sin(ang_r), zero, -jnp.sin(ang_c), zero], axis=1)
    sb = jnp.concatenate([zero, jnp.sin(ang_r), zero, jnp.sin(ang_c)], axis=1)
    ident = lambda a, fill: jnp.concatenate([a, jnp.full((TM, ATT_DH), fill, F32)], axis=0)
    tile2 = lambda a: jnp.concatenate([a, a], axis=1)
    return tile2(ident(cos, 1.0)), tile2(ident(sa, 0.0)), tile2(ident(sb, 0.0))


def kernel(x, c, ctx, c_ctx, w_mod, b_mod, ln_g, ln_b, mlp_w1, mlp_w2, ev_w_in, ev_ret_theta, ev_gla_gk_w,
           ev_gla_gk_b, ev_gla_norm_g, ev_w_out, od_w_qkv, od_sink, od_w_out):
    x2 = x.reshape(LAT_ROWS, D)
    ctx2 = ctx.reshape(B * L, D)

    cs = jnp.concatenate([c, c_ctx[None, :], jnp.zeros((8 - B - 1, D), F32)], axis=0)
    mods = _modulation(cs, w_mod, b_mod).reshape(DEPTH * 8, 1, 6 * D)
    ln = jnp.concatenate([ln_g.reshape(2 * DEPTH, D), ln_b.reshape(2 * DEPTH, D)], axis=0)

    w_in = ev_w_in[0]
    off = [0]
    for s in (QK_W, QK_W, V_W, V_W, QK_W, QK_W, V_W, V_W, 2 * GATE_RANK):
        off.append(off[-1] + s)
    piece = lambda i: w_in[:, off[i]:off[i + 1]].astype(BF16)
    gk_w = ev_gla_gk_w[0]
    zeros = jnp.zeros((GATE_RANK, QK_W), F32)
    g2 = jnp.concatenate([jnp.concatenate([gk_w[0], zeros], axis=1),
                          jnp.concatenate([zeros, gk_w[1]], axis=1)], axis=0)
    w0 = dict(wq_a=piece(0), wkT_a=piece(1).T, wv_a=piece(2), wg_a=piece(3),
              wq_b=piece(4), wkT_b=piece(5).T, wv_b=piece(6), wg_b=piece(7), wlr=piece(8),
              g2T=g2.T.astype(BF16), gbc=ev_gla_gk_b[0].reshape(2 * QK_W, 1))
    qa, kaT, va, ga, qb, kbT, vb, gbv, lfT, lbT = _in_proj0(x2, ctx2, mods, w0)

    theta = ev_ret_theta[0]
    th_row = jnp.repeat(theta, HEAD_DK, axis=1).reshape(2, 2, 2 * HEAD_DK).transpose(1, 0, 2)
    th_col = th_row.transpose(0, 2, 1)
    y_ret, yc_ret = _scan_group(qa, kaT, va, ga, (th_row, th_col), gla=False)
    y_gla, yc_gla = _scan_group(qb, kbT, vb, gbv, (lfT, lbT, ev_gla_norm_g[0].reshape(1, HEAD_DV)), gla=True)

    xs = _out_mlp((y_ret, y_gla, yc_ret, yc_gla), x2, ctx2, mods, ln,
                  ev_w_out[0].astype(BF16), mlp_w1[0].astype(BF16), mlp_w2[0].astype(BF16),
                  layer=0, n_tiles=N_TILES, split_ctx=True)

    wqkv = od_w_qkv[0].astype(BF16)
    tabs = _rope_tables()
    tabsT = tuple(a.T for a in tabs)
    q1T, k1, v1T = _in_proj1(xs, mods, tabs, tabsT, wqkv[:, 0:D].T, wqkv[:, D:D + ATT_KVW],
                             wqkv[:, D + ATT_KVW:D + 2 * ATT_KVW].T)
    att = _attention(od_sink[0], q1T, k1, v1T)
    out = _out_mlp((att,), xs, None, mods, ln,
                   od_w_out[0].astype(BF16), mlp_w1[1].astype(BF16), mlp_w2[1].astype(BF16),
                   layer=1, n_tiles=N_LAT_TILES, split_ctx=False)
    return out.reshape(B, T, D)
```

```python
import functools

import jax
import jax.numpy as jnp
from jax import lax
from jax.experimental import pallas as pl
from jax.experimental.pallas import tpu as pltpu

F32 = jnp.float32
BF16 = jnp.bfloat16

D = 1024
B = 2
T = 8192
L = 256
DEPTH = 2
GRID_W = 64
D_FF = 4 * D
HEAD_DV = 128
HEAD_DK = 64
GATE_RANK = 16
GATE_TAU = 16.0
QK_W = 256
V_W = 512
ATT_DH = 64
ATT_QH = 16
ATT_KVH = 4
ATT_GROUP = 4
ATT_KVW = ATT_KVH * ATT_DH
WINDOW = 128
ROPE_BASE = 10000.0
ALPHA = (2.0 * DEPTH) ** 0.25
LN_EPS = 1e-5
RMS_EPS = 1e-6

TM = 512
LAT_ROWS = B * T
ROWS = LAT_ROWS + B * L
N_LAT_TILES = LAT_ROWS // TM
N_TILES = ROWS // TM
TILES_PER_BATCH = T // TM

CH = 64
SP = 2 * CH
TB = 1024
NP = TB // SP
NP_CTX = L // SP
NBLK = T // TB
NP_ALL = NP_CTX + T // SP
N_SLABS = ROWS // SP
KV_UNROLL = 8
OUT_UNROLL = 8

TQ = 512
QB = 128
KWIN = 3 * QB
LOG2E = 1.4426950408889634

VMEM_LIMIT = 56 * 1024 * 1024


def _dot(a, b):
    return jnp.dot(a, b, preferred_element_type=F32)


def _dot_nt(a, b):
    return lax.dot_general(a, b, (((1,), (1,)), ((), ())), preferred_element_type=F32)


def _full_spec(shape):
    nd = len(shape)
    return pl.BlockSpec(shape, lambda *_: (0,) * nd, pipeline_mode=pl.Buffered(1))


def _mod_row(t):
    return jnp.minimum(t // TILES_PER_BATCH, B)


def _mod_spec(layer):
    return pl.BlockSpec((None, 1, 6 * D), lambda t: (layer * 8 + _mod_row(t), 0, 0))


def _layer_norm(x, g, b):
    mu = jnp.mean(x, axis=-1, keepdims=True)
    xc = x - mu
    var = jnp.mean(xc * xc, axis=-1, keepdims=True)
    return xc * lax.rsqrt(var + LN_EPS) * g + b


def _log_sigmoid(z):
    return jnp.minimum(z, 0.0) - jnp.log1p(jnp.exp(-jnp.abs(z)))


MOD_TN = 1536


def _mod_kernel(c_ref, w_ref, b_ref, o_ref):
    s = jax.nn.silu(c_ref[...])
    o_ref[...] = jnp.dot(s, w_ref[...], precision=lax.Precision.HIGHEST,
                         preferred_element_type=F32) + b_ref[...]


def _modulation(cs, w_mod, b_mod):
    return pl.pallas_call(
        _mod_kernel,
        grid=(DEPTH, 6 * D // MOD_TN),
        in_specs=[
            pl.BlockSpec((8, D), lambda i, n: (0, 0)),
            pl.BlockSpec((None, D, MOD_TN), lambda i, n: (i, 0, n)),
            pl.BlockSpec((None, 1, MOD_TN), lambda i, n: (i, 0, n)),
        ],
        out_specs=pl.BlockSpec((None, 8, MOD_TN), lambda i, n: (i, 0, n)),
        out_shape=jax.ShapeDtypeStruct((DEPTH, 8, 6 * D), F32),
        compiler_params=pltpu.CompilerParams(
            dimension_semantics=("arbitrary", "arbitrary"), vmem_limit_bytes=VMEM_LIMIT),
        name="modulation",
    )(cs, w_mod, b_mod.reshape(DEPTH, 1, 6 * D))


def _in0_kernel(x_ref, ctx_ref, mod_ref,
                wq_a, wkT_a, wv_a, wg_a, wq_b, wkT_b, wv_b, wg_b, wlr, g2T, gbc,
                qa_o, kaT_o, va_o, ga_o, qb_o, kbT_o, vb_o, gb_o, lfT_o, lbT_o):
    t = pl.program_id(0)
    xt = jnp.where(t == N_LAT_TILES, ctx_ref[...], x_ref[...])
    sh1 = mod_ref[:, 0:D]
    sc1 = mod_ref[:, D:2 * D]
    h = (xt * (1.0 + sc1) + sh1).astype(BF16)
    qk_scale = HEAD_DK ** -0.5

    def put_slabs(o_ref, val):
        for i in range(TM // SP):
            o_ref[i] = val[:, i * SP:(i + 1) * SP].astype(o_ref.dtype)

    qa_o[...] = _dot(h, wq_a[...]).astype(BF16)
    put_slabs(kaT_o, _dot_nt(wkT_a[...], h) * qk_scale)
    va_o[...] = _dot(h, wv_a[...]).astype(BF16)
    ga_o[...] = jax.nn.silu(_dot(h, wg_a[...])).astype(BF16)

    qb_o[...] = (_dot(h, wq_b[...]) * qk_scale).astype(BF16)
    put_slabs(kbT_o, _dot_nt(wkT_b[...], h))
    vb_o[...] = _dot(h, wv_b[...]).astype(BF16)
    gb_o[...] = jax.nn.silu(_dot(h, wg_b[...])).astype(BF16)

    lr = _dot(h, wlr[...]).astype(BF16)
    lsT = _log_sigmoid(_dot_nt(g2T[...], lr) + gbc[...]) * (1.0 / GATE_TAU)
    put_slabs(lfT_o, lsT[0:QK_W, :])
    put_slabs(lbT_o, lsT[QK_W:2 * QK_W, :])


def _in_proj0(x2, ctx2, mods, w):
    row = lambda t: (t, 0)
    slab = lambda t: (t, 0, 0)
    nsl = TM // SP
    row_out = lambda width: (jax.ShapeDtypeStruct((ROWS, width), BF16), pl.BlockSpec((TM, width), row))
    slab_out = lambda dt: (jax.ShapeDtypeStruct((N_SLABS, QK_W, SP), dt), pl.BlockSpec((nsl, QK_W, SP), slab))
    outs = [row_out(QK_W), slab_out(BF16), row_out(V_W), row_out(V_W),
            row_out(QK_W), slab_out(BF16), row_out(V_W), row_out(V_W),
            slab_out(BF16), slab_out(BF16)]
    weights = [w["wq_a"], w["wkT_a"], w["wv_a"], w["wg_a"], w["wq_b"], w["wkT_b"], w["wv_b"], w["wg_b"],
               w["wlr"], w["g2T"], w["gbc"]]
    in_specs = [
        pl.BlockSpec((TM, D), lambda t: (jnp.minimum(t, N_LAT_TILES - 1), 0)),
        _full_spec((B * L, D)),
        _mod_spec(0),
    ] + [_full_spec(a.shape) for a in weights]
    return pl.pallas_call(
        _in0_kernel,
        grid=(N_TILES,),
        in_specs=in_specs,
        out_specs=[o[1] for o in outs],
        out_shape=[o[0] for o in outs],
        compiler_params=pltpu.CompilerParams(
            dimension_semantics=("arbitrary",), vmem_limit_bytes=VMEM_LIMIT),
        name="in_proj0",
    )(x2, ctx2, mods, *weights)


def _chunk_diag(kv, c):
    r0 = c * 2 * HEAD_DK
    return jnp.concatenate([kv[r0:r0 + HEAD_DK, 0:HEAD_DV],
                            kv[r0 + HEAD_DK:r0 + 2 * HEAD_DK, HEAD_DV:2 * HEAD_DV]], axis=0)


def _scan_kernel(*refs, gla):
    if gla:
        (q_ref, kT_ref, v_ref, g_ref, lfT_ref, lbT_ref,
         qc_ref, kTc_ref, vc_ref, gc_ref, lfTc_ref, lbTc_ref, ng_ref,
         y_ref, yc_ref, s_ref, r_ref, rst_ref, sst_ref, kv_ref, dec_ref,
         kd_ref, lhs_ref, pm_ref) = refs
        lat = (q_ref, kT_ref, v_ref, g_ref, lfT_ref, lbT_ref)
        cxt = (qc_ref, kTc_ref, vc_ref, gc_ref, lfTc_ref, lbTc_ref)
    else:
        (q_ref, kT_ref, v_ref, g_ref, qc_ref, kTc_ref, vc_ref, gc_ref, thr_ref, thc_ref,
         y_ref, yc_ref, s_ref, r_ref, rst_ref, sst_ref, kv_ref, dec_ref,
         kd_ref, lhs_ref, pm_ref) = refs
        lat = (q_ref, kT_ref, v_ref, g_ref, None, None)
        cxt = (qc_ref, kTc_ref, vc_ref, gc_ref, None, None)

    phase = pl.program_id(2)
    j = pl.program_id(3)

    ri = lax.broadcasted_iota(jnp.int32, (SP, SP), 0)
    ci = lax.broadcasted_iota(jnp.int32, (SP, SP), 1)
    same = (ri // CH) == (ci // CH)
    first_lane = ci < CH
    head_a = ci < HEAD_DK

    if gla:
        as_w = lambda m: m.astype(BF16)
        tot = jnp.concatenate([jnp.broadcast_to(ri < CH, (SP, SP)), jnp.broadcast_to(ri >= CH, (SP, SP))], axis=1)
        w_end_f = jnp.concatenate([as_w(same & (ri > ci)), as_w(tot)], axis=1)
        w_end_b = jnp.concatenate([as_w(same & (ri < ci)), as_w(tot)], axis=1)
        w_cum_f = as_w(same & (ri <= ci))
        w_cum_b = as_w(same & (ri >= ci))
    else:
        lg_row = jnp.log1p(-jnp.exp(thr_ref[...]))
        lg_col = jnp.log1p(-jnp.exp(thc_ref[...]))
        it = (ci % CH).astype(F32)
        ir = (ri % CH).astype(F32)
        ret_end_f = jnp.exp((CH - 1.0 - it) * lg_col[:, 0:1])
        ret_end_b = jnp.exp(it * lg_col[:, 1:2])
        ret_dec_f = jnp.exp(jnp.broadcast_to(CH * lg_col[:, 0:1], (SP, SP)))
        ret_dec_b = jnp.exp(jnp.broadcast_to(CH * lg_col[:, 1:2], (SP, SP)))
        ret_ebTi = jnp.exp(-(it + 1.0) * lg_col[:, 0:1])
        ret_erTi = jnp.exp(-(CH - it) * lg_col[:, 1:2])
        ret_eb = jnp.exp((ir + 1.0) * lg_row[0:1, :])
        ret_er = jnp.exp((CH - ir) * lg_row[1:2, :])

    def kv_stage(blk, n, fwd):
        _, kT_r, v_r, _, lfT_r, lbT_r = blk

        def body(p, carry):
            kT = kT_r[p].astype(F32)
            v = v_r[pl.ds(pl.multiple_of(p * SP, SP), SP), :]
            if gla:
                res = _dot((lfT_r if fwd else lbT_r)[p], w_end_f if fwd else w_end_b)
                e_end = jnp.exp(res[:, 0:SP])
                dec0 = jnp.exp(res[:, SP:2 * SP])
                dec1 = jnp.exp(res[:, 2 * SP:3 * SP])
            else:
                e_end = ret_end_f if fwd else ret_end_b
                dec0 = dec1 = ret_dec_f if fwd else ret_dec_b
            ke = kT * e_end
            lhs = jnp.concatenate([jnp.where(first_lane, ke, 0.0), jnp.where(first_lane, 0.0, ke)],
                                  axis=0).astype(BF16)
            kv = _dot(lhs, v)
            kv_ref[2 * p] = _chunk_diag(kv, 0)
            kv_ref[2 * p + 1] = _chunk_diag(kv, 1)
            dec_ref[2 * p] = dec0
            dec_ref[2 * p + 1] = dec1
            return carry

        lax.fori_loop(0, n, body, 0, unroll=min(n, KV_UNROLL))

    def phase0_block(blk, n, slot0):
        kv_stage(blk, n, fwd=False)

        def body(i, r_state):
            p = n - 1 - i
            rst_ref[slot0 + p, :, HEAD_DV:2 * HEAD_DV] = r_state.astype(BF16)
            r_state = dec_ref[2 * p + 1] * r_state + kv_ref[2 * p + 1]
            rst_ref[slot0 + p, :, 0:HEAD_DV] = r_state.astype(BF16)
            return dec_ref[2 * p] * r_state + kv_ref[2 * p]

        r_ref[...] = lax.fori_loop(0, n, body, r_ref[...])

    def phase1_block(blk, n, slot0, out_ref):
        q_r, kT_r, v_r, g_r, lfT_r, lbT_r = blk
        kv_stage(blk, n, fwd=True)

        def rec(p, s_state):
            sst_ref[p, :, 0:HEAD_DV] = s_state.astype(BF16)
            s_state = dec_ref[2 * p] * s_state + kv_ref[2 * p]
            sst_ref[p, :, HEAD_DV:2 * HEAD_DV] = s_state.astype(BF16)
            return dec_ref[2 * p + 1] * s_state + kv_ref[2 * p + 1]

        s_ref[...] = lax.fori_loop(0, n, rec, s_ref[...])

        r2 = lax.broadcasted_iota(jnp.int32, (2 * SP, SP), 0) % SP
        c2 = lax.broadcasted_iota(jnp.int32, (2 * SP, SP), 1)
        same2 = (r2 // CH) == (c2 // CH)
        mask_f = same2 & (r2 >= c2)
        mask_b = same2 & (r2 < c2)

        def prep(p, carry):
            rows = pl.ds(pl.multiple_of(p * SP, SP), SP)
            q = q_r[rows, :].astype(F32)
            kT = kT_r[p].astype(F32)
            if gla:
                lfT = lfT_r[p]
                lbT = lbT_r[p]
                ebTi = jnp.exp(-_dot(lfT, w_cum_f))
                erTi = jnp.exp(-_dot(lbT, w_cum_b))
                e_b = jnp.exp(_dot_nt(w_cum_b, lfT))
                e_r = jnp.exp(_dot_nt(w_cum_f, lbT))
            else:
                ebTi, erTi, e_b, e_r = ret_ebTi, ret_erTi, ret_eb, ret_er
            kd_ref[p, 0:SP, 0:SP] = (kT * ebTi).astype(BF16)
            kd_ref[p, SP:2 * SP, SP:2 * SP] = (kT * erTi).astype(BF16)
            qf = q * e_b
            qb = q * e_r
            lhs_ref[p, :, 0:SP] = jnp.concatenate(
                [jnp.where(head_a, qf, 0.0), jnp.where(head_a, 0.0, qf)], axis=0).astype(BF16)
            lhs_ref[p, :, SP:2 * SP] = jnp.concatenate(
                [jnp.where(head_a, qb, 0.0), jnp.where(head_a, 0.0, qb)], axis=0).astype(BF16)
            return carry

        def score(p, carry):
            sc = _dot(lhs_ref[p], kd_ref[p])
            pm_ref[p] = jnp.where(mask_f, sc[:, 0:SP], jnp.where(mask_b, sc[:, SP:2 * SP], 0.0)).astype(BF16)
            return carry

        def emit(p, carry):
            rows = pl.ds(pl.multiple_of(p * SP, SP), SP)
            v = v_r[rows, :]
            states = jnp.concatenate([sst_ref[p], rst_ref[slot0 + p]], axis=0)
            o_int = _dot(lhs_ref[p], states)
            o_a = _dot(pm_ref[p, 0:SP, :], v[:, 0:HEAD_DV]) + jnp.concatenate(
                [o_int[0:CH, 0:HEAD_DV], o_int[CH:SP, HEAD_DV:2 * HEAD_DV]], axis=0)
            o_b = _dot(pm_ref[p, SP:2 * SP, :], v[:, HEAD_DV:2 * HEAD_DV]) + jnp.concatenate(
                [o_int[SP:SP + CH, 0:HEAD_DV], o_int[SP + CH:2 * SP, HEAD_DV:2 * HEAD_DV]], axis=0)

            def nrm(o):
                y = o * lax.rsqrt(jnp.mean(o * o, axis=-1, keepdims=True) + RMS_EPS)
                return y * ng_ref[...] if gla else y

            y = jnp.concatenate([nrm(o_a), nrm(o_b)], axis=1) * g_r[rows, :].astype(F32)
            out_ref[rows, :] = y.astype(BF16)
            return carry

        lax.fori_loop(0, n, prep, 0, unroll=min(n, OUT_UNROLL))
        lax.fori_loop(0, n, score, 0, unroll=min(n, OUT_UNROLL))
        lax.fori_loop(0, n, emit, 0, unroll=min(n, OUT_UNROLL))

    @pl.when((pl.program_id(0) == 0) & (pl.program_id(1) == 0) & (phase == 0) & (j == 0))
    def _():
        kd_ref[...] = jnp.zeros_like(kd_ref)

    @pl.when(phase == 0)
    def _():
        @pl.when(j == 0)
        def _():
            r_ref[...] = jnp.zeros_like(r_ref)
            phase0_block(cxt, NP_CTX, 0)

        phase0_block(lat, NP, NP_CTX + (NBLK - 1 - j) * NP)

    @pl.when(phase == 1)
    def _():
        @pl.when(j == 0)
        def _():
            s_ref[...] = jnp.zeros_like(s_ref)
            phase1_block(cxt, NP_CTX, 0, yc_ref)

        phase1_block(lat, NP, NP_CTX + j * NP, y_ref)


def _scan_group(q, kT, v, g, extra, *, gla):
    def blk(b, ph, j, used_in_phase0):
        jj = jnp.where(ph == 0, NBLK - 1 - j, j)
        if not used_in_phase0:
            jj = jnp.where(ph == 0, 0, jj)
        return b * NBLK + jj

    def lat_specs(used0):
        return dict(
            row=lambda w: pl.BlockSpec((TB, w), lambda b, p, ph, j: (blk(b, ph, j, used0), p)),
            slab=pl.BlockSpec((NP, SP, SP), lambda b, p, ph, j: (blk(b, ph, j, used0), p, 0)))

    ctx_row = lambda w: pl.BlockSpec((L, w), lambda b, p, ph, j: (LAT_ROWS // L + b, p))
    ctx_slab = pl.BlockSpec((NP_CTX, SP, SP), lambda b, p, ph, j: (LAT_ROWS // L + b, p, 0))
    used, unused = lat_specs(True), lat_specs(False)

    in_specs = [unused["row"](2 * HEAD_DK), used["slab"], used["row"](2 * HEAD_DV), unused["row"](2 * HEAD_DV)]
    ctx_specs = [ctx_row(2 * HEAD_DK), ctx_slab, ctx_row(2 * HEAD_DV), ctx_row(2 * HEAD_DV)]
    if gla:
        lfT, lbT, ng = extra
        in_specs += [unused["slab"], used["slab"]]
        ctx_specs += [ctx_slab, ctx_slab]
        args = (q, kT, v, g, lfT, lbT, q, kT, v, g, lfT, lbT, ng)
        in_specs = in_specs + ctx_specs + [pl.BlockSpec((1, HEAD_DV), lambda b, p, ph, j: (0, 0))]
    else:
        th_row, th_col = extra
        args = (q, kT, v, g, q, kT, v, g, th_row, th_col)
        in_specs = in_specs + ctx_specs + [
            pl.BlockSpec((None, 2, 2 * HEAD_DK), lambda b, p, ph, j: (p, 0, 0)),
            pl.BlockSpec((None, 2 * HEAD_DK, 2), lambda b, p, ph, j: (p, 0, 0)),
        ]
    return pl.pallas_call(
        functools.partial(_scan_kernel, gla=gla),
        grid=(B, 2, 2, NBLK),
        in_specs=in_specs,
        out_specs=[
            pl.BlockSpec((TB, 2 * HEAD_DV), lambda b, p, ph, j: (b * NBLK + jnp.where(ph == 0, 0, j), p)),
            pl.BlockSpec((L, 2 * HEAD_DV), lambda b, p, ph, j: (b, p)),
        ],
        out_shape=[jax.ShapeDtypeStruct((LAT_ROWS, V_W), BF16), jax.ShapeDtypeStruct((B * L, V_W), BF16)],
        scratch_shapes=[
            pltpu.VMEM((SP, HEAD_DV), F32),
            pltpu.VMEM((SP, HEAD_DV), F32),
            pltpu.VMEM((NP_ALL, SP, 2 * HEAD_DV), BF16),
            pltpu.VMEM((NP, SP, 2 * HEAD_DV), BF16),
            pltpu.VMEM((2 * NP, SP, HEAD_DV), F32),
            pltpu.VMEM((2 * NP, SP, HEAD_DV), F32),
            pltpu.VMEM((NP, 2 * SP, 2 * SP), BF16),
            pltpu.VMEM((NP, 2 * SP, 2 * SP), BF16),
            pltpu.VMEM((NP, 2 * SP, SP), BF16),
        ],
        compiler_params=pltpu.CompilerParams(
            dimension_semantics=("arbitrary",) * 4, vmem_limit_bytes=VMEM_LIMIT),
        name="scan_gla" if gla else "scan_ret",
    )(*args)


FF_CH = 512


def _out_kernel(*refs, layer, split_ctx):
    if split_ctx:
        ya_ref, yb_ref, yac_ref, ybc_ref, x_ref, ctx_ref, mod_ref, ln_ref, wo_ref, w1_ref, w2_ref, o_ref = refs
        is_ctx = pl.program_id(0) == N_LAT_TILES
        x = jnp.where(is_ctx, ctx_ref[...], x_ref[...])
        ya = jnp.where(is_ctx, yac_ref[...], ya_ref[...])
        yb = jnp.where(is_ctx, ybc_ref[...], yb_ref[...])
    else:
        ya_ref, yb_ref, x_ref, mod_ref, ln_ref, wo_ref, w1_ref, w2_ref, o_ref = refs
        x = x_ref[...]
        ya = ya_ref[...]
        yb = yb_ref[...]
    g1 = mod_ref[:, 2 * D:3 * D]
    sh2 = mod_ref[:, 3 * D:4 * D]
    sc2 = mod_ref[:, 4 * D:5 * D]
    g2 = mod_ref[:, 5 * D:6 * D]
    ln_g0 = ln_ref[2 * layer:2 * layer + 1, :]
    ln_g1 = ln_ref[2 * layer + 1:2 * layer + 2, :]
    ln_b0 = ln_ref[2 * DEPTH + 2 * layer:2 * DEPTH + 2 * layer + 1, :]
    ln_b1 = ln_ref[2 * DEPTH + 2 * layer + 1:2 * DEPTH + 2 * layer + 2, :]

    half = D // 2
    y = _dot(ya, wo_ref[0:half, :]) + _dot(yb, wo_ref[half:D, :])
    x1 = _layer_norm(ALPHA * x + g1 * y, ln_g0, ln_b0)
    h2 = (x1 * (1.0 + sc2) + sh2).astype(BF16)
    acc = jnp.zeros((TM, D), F32)
    for c in range(D_FF // FF_CH):
        cols = slice(c * FF_CH, (c + 1) * FF_CH)
        hc = jnp.maximum(_dot(h2, w1_ref[:, cols]), 0.0)
        acc = acc + _dot((hc * hc).astype(BF16), w2_ref[cols, :])
    o_ref[...] = _layer_norm(ALPHA * x1 + g2 * acc, ln_g1, ln_b1)


def _out_mlp(ys, xs, ctx2, mods, ln, wo, w1, w2, *, layer, n_tiles, split_ctx):
    half = D // 2
    lat_row = lambda t: (jnp.minimum(t, N_LAT_TILES - 1), 0)
    if split_ctx:
        ya, yb, yac, ybc = ys
        in_specs = [pl.BlockSpec((TM, half), lat_row), pl.BlockSpec((TM, half), lat_row),
                    _full_spec((B * L, half)), _full_spec((B * L, half)),
                    pl.BlockSpec((TM, D), lat_row), _full_spec((B * L, D))]
        args = [ya, yb, yac, ybc, xs, ctx2]
    else:
        (att,) = ys
        in_specs = [pl.BlockSpec((TM, half), lambda t: (t, 0)), pl.BlockSpec((TM, half), lambda t: (t, 1)),
                    pl.BlockSpec((TM, D), lambda t: (t, 0))]
        args = [att, att, xs]
    in_specs += [_mod_spec(layer), _full_spec(ln.shape), _full_spec(wo.shape),
                 _full_spec(w1.shape), _full_spec(w2.shape)]
    args += [mods, ln, wo, w1, w2]
    return pl.pallas_call(
        functools.partial(_out_kernel, layer=layer, split_ctx=split_ctx),
        grid=(n_tiles,),
        in_specs=in_specs,
        out_specs=pl.BlockSpec((TM, D), lambda t: (t, 0)),
        out_shape=jax.ShapeDtypeStruct((n_tiles * TM, D), F32),
        compiler_params=pltpu.CompilerParams(
            dimension_semantics=("arbitrary",), vmem_limit_bytes=VMEM_LIMIT),
        name="out_mlp%d" % layer,
    )(*args)


def _in1_kernel(x_ref, mod_ref, cos_ref, sa_ref, sb_ref, cosT_ref, saT_ref, sbT_ref,
                wqT_ref, wk_ref, wvT_ref, qT_o, k_o, vT_o):
    sh1 = mod_ref[:, 0:D]
    sc1 = mod_ref[:, D:2 * D]
    h = (x_ref[...] * (1.0 + sc1) + sh1).astype(BF16)
    nsl = TM // SP

    cos, sa, sb = cos_ref[...], sa_ref[...], sb_ref[...]
    k = _dot(h, wk_ref[...])
    for i in range(ATT_KVW // 128):
        u = k[:, i * 128:(i + 1) * 128]
        r = u * cos + pltpu.roll(u, 128 - 16, 1) * sa + pltpu.roll(u, 16, 1) * sb
        k_o[:, i * 128:(i + 1) * 128] = r.astype(BF16)

    cosT, saT, sbT = cosT_ref[...], saT_ref[...], sbT_ref[...]
    qT = _dot_nt(wqT_ref[...], h) * (LOG2E * ATT_DH ** -0.5)
    for i in range(D // 128):
        u = qT[i * 128:(i + 1) * 128, :]
        r = (u * cosT + pltpu.roll(u, 128 - 16, 0) * saT + pltpu.roll(u, 16, 0) * sbT).astype(BF16)
        for s in range(nsl):
            qT_o[s, i * 128:(i + 1) * 128, :] = r[:, s * SP:(s + 1) * SP]

    vT = _dot_nt(wvT_ref[...], h).astype(BF16)
    for s in range(nsl):
        vT_o[s] = vT[:, s * SP:(s + 1) * SP]


def _in_proj1(xs, mods, tabs, tabsT, wqT, wk, wvT):
    tile = lambda t: jnp.where(t == N_LAT_TILES, TILES_PER_BATCH, t % TILES_PER_BATCH)
    row = lambda t: (t, 0)
    slab = lambda t: (t, 0, 0)
    nsl = TM // SP
    tab_spec = pl.BlockSpec((TM, 128), lambda t: (tile(t), 0))
    tabT_spec = pl.BlockSpec((128, TM), lambda t: (0, tile(t)))
    return pl.pallas_call(
        _in1_kernel,
        grid=(N_TILES,),
        in_specs=[pl.BlockSpec((TM, D), row), _mod_spec(1)] + [tab_spec] * 3 + [tabT_spec] * 3
                 + [_full_spec(wqT.shape), _full_spec(wk.shape), _full_spec(wvT.shape)],
        out_specs=[pl.BlockSpec((nsl, D, SP), slab), pl.BlockSpec((TM, ATT_KVW), row),
                   pl.BlockSpec((nsl, ATT_KVW, SP), slab)],
        out_shape=[jax.ShapeDtypeStruct((N_SLABS, D, SP), BF16), jax.ShapeDtypeStruct((ROWS, ATT_KVW), BF16),
                   jax.ShapeDtypeStruct((N_SLABS, ATT_KVW, SP), BF16)],
        compiler_params=pltpu.CompilerParams(
            dimension_semantics=("arbitrary",), vmem_limit_bytes=VMEM_LIMIT),
        name="in_proj1",
    )(xs, mods, *tabs, *tabsT, wqT, wk, wvT)


def _attn_kernel(sink_ref, qT_ref, k_ref, vT_ref, kc_ref, vcT_ref, o_ref, bias_ref):
    n = pl.program_id(1)
    nql = ATT_GROUP * QB
    lane_g = lax.broadcasted_iota(jnp.int32, (1, nql), 1) // QB
    zero_half = jnp.zeros((ATT_DH, nql), BF16)
    ones_rows = jnp.ones((16, KWIN + L), BF16)

    @pl.when((pl.program_id(0) == 0) & (n == 0))
    def _():
        kj = lax.broadcasted_iota(jnp.int32, (KWIN, QB), 0)
        qi = lax.broadcasted_iota(jnp.int32, (KWIN, QB), 1)
        for case, delta in enumerate((-QB, 0, -2 * QB)):
            d = kj - qi + delta
            bias_ref[case] = jnp.where((d >= -WINDOW) & (d <= WINDOW), 0.0, -jnp.inf)

    def scores(i, kh):
        n0 = (n * (TQ // QB) + i) * QB
        start = pl.multiple_of(jnp.clip(n0 - QB, 0, T - KWIN), QB)
        bias1 = bias_ref[jnp.where(n0 == 0, 1, jnp.where(n0 == T - QB, 2, 0))]
        bias = jnp.concatenate([bias1] * ATT_GROUP, axis=1)
        pair = slice((kh // 2) * 128, (kh // 2 + 1) * 128)
        qT = jnp.concatenate(
            [qT_ref[i, (kh * ATT_GROUP + g) * ATT_DH:(kh * ATT_GROUP + g + 1) * ATT_DH, :]
             for g in range(ATT_GROUP)], axis=1)
        q_pad = jnp.concatenate([zero_half, qT] if kh % 2 else [qT, zero_half], axis=0)
        s_loc = _dot(k_ref[pl.ds(start, KWIN), pair], q_pad) + bias
        s_ctx = _dot(kc_ref[:, pair], q_pad)
        return s_loc, s_ctx, start // SP

    def finish(i, kh, s_loc, s_ctx, slab0):
        drows = slice(kh * ATT_DH, (kh + 1) * ATT_DH)
        sink = jnp.zeros((1, nql), F32)
        for g in range(ATT_GROUP):
            sink = jnp.where(lane_g == g, sink_ref[kh * ATT_GROUP + g] * LOG2E, sink)
        m = jnp.maximum(jnp.maximum(jnp.max(s_loc, axis=0, keepdims=True),
                                    jnp.max(s_ctx, axis=0, keepdims=True)), sink)
        pT = jnp.concatenate([jnp.exp2(s_loc - m).astype(BF16),
                              jnp.exp2(s_ctx - m).astype(BF16)], axis=0)
        vT = jnp.concatenate([vT_ref[slab0 + t, drows, :] for t in range(KWIN // SP)]
                             + [vcT_ref[t, drows, :] for t in range(L // SP)], axis=1)
        o_ext = _dot(jnp.concatenate([vT, ones_rows], axis=0), pT)
        denom = o_ext[ATT_DH:ATT_DH + 1, :] + jnp.exp2(sink - m)
        oT = o_ext[0:ATT_DH, :] / denom
        for g in range(0, ATT_GROUP, 2):
            two = jnp.concatenate([oT[:, g * QB:(g + 1) * QB], oT[:, (g + 1) * QB:(g + 2) * QB]], axis=0)
            c0 = (kh * ATT_GROUP + g) * ATT_DH
            o_ref[i * QB:(i + 1) * QB, c0:c0 + 2 * ATT_DH] = two.T.astype(BF16)

    items = [(i, kh) for i in range(TQ // QB) for kh in range(ATT_KVH)]
    pending = scores(*items[0])
    for t, item in enumerate(items):
        nxt = scores(*items[t + 1]) if t + 1 < len(items) else None
        finish(*item, *pending)
        pending = nxt


def _attention(sink, qT, k, vT):
    nq = T // TQ
    return pl.pallas_call(
        _attn_kernel,
        grid=(B, nq),
        in_specs=[
            pl.BlockSpec(memory_space=pltpu.SMEM),
            pl.BlockSpec((TQ // SP, D, SP), lambda b, n: (b * nq + n, 0, 0)),
            pl.BlockSpec((T, ATT_KVW), lambda b, n: (b, 0)),
            pl.BlockSpec((T // SP, ATT_KVW, SP), lambda b, n: (b, 0, 0)),
            pl.BlockSpec((L, ATT_KVW), lambda b, n: (LAT_ROWS // L + b, 0)),
            pl.BlockSpec((L // SP, ATT_KVW, SP), lambda b, n: (LAT_ROWS // L + b, 0, 0)),
        ],
        out_specs=pl.BlockSpec((TQ, D), lambda b, n: (b * nq + n, 0)),
        out_shape=jax.ShapeDtypeStruct((LAT_ROWS, D), BF16),
        compiler_params=pltpu.CompilerParams(
            dimension_semantics=("arbitrary", "arbitrary"), vmem_limit_bytes=VMEM_LIMIT),
        scratch_shapes=[pltpu.VMEM((3, KWIN, QB), F32)],
        name="window_attn",
    )(sink, qT, k, vT, k, vT)


def _rope_tables():
    half = ATT_DH // 2
    inv_freq = ROPE_BASE ** (-jnp.arange(0, half, 2, dtype=F32) / half)
    pos = jnp.arange(T)
    ang_r = (pos // GRID_W).astype(F32)[:, None] * inv_freq[None, :]
    ang_c = (pos % GRID_W).astype(F32)[:, None] * inv_freq[None, :]
    zero = jnp.zeros_like(ang_r)
    cos = jnp.concatenate([jnp.cos(ang_r), jnp.cos(ang_r), jnp.cos(ang_c), jnp.cos(ang_c)], axis=1)
    sa = jnp.concatenate([-jnp.sin(ang_r), zero, -jnp.sin(ang_c), zero], axis=1)
    sb = jnp.concatenate([zero, jnp.sin(ang_r), zero, jnp.sin(ang_c)], axis=1)
    ident = lambda a, fill: jnp.concatenate([a, jnp.full((TM, ATT_DH), fill, F32)], axis=0)
    tile2 = lambda a: jnp.concatenate([a, a], axis=1)
    return tile2(ident(cos, 1.0)), tile2(ident(sa, 0.0)), tile2(ident(sb, 0.0))


def kernel(x, c, ctx, c_ctx, w_mod, b_mod, ln_g, ln_b, mlp_w1, mlp_w2, ev_w_in, ev_ret_theta, ev_gla_gk_w,
           ev_gla_gk_b, ev_gla_norm_g, ev_w_out, od_w_qkv, od_sink, od_w_out):
    x2 = x.reshape(LAT_ROWS, D)
    ctx2 = ctx.reshape(B * L, D)

    cs = jnp.concatenate([c, c_ctx[None, :], jnp.zeros((8 - B - 1, D), F32)], axis=0)
    mods = _modulation(cs, w_mod, b_mod).reshape(DEPTH * 8, 1, 6 * D)
    ln = jnp.concatenate([ln_g.reshape(2 * DEPTH, D), ln_b.reshape(2 * DEPTH, D)], axis=0)

    w_in = ev_w_in[0]
    off = [0]
    for s in (QK_W, QK_W, V_W, V_W, QK_W, QK_W, V_W, V_W, 2 * GATE_RANK):
        off.append(off[-1] + s)
    piece = lambda i: w_in[:, off[i]:off[i + 1]].astype(BF16)
    gk_w = ev_gla_gk_w[0]
    zeros = jnp.zeros((GATE_RANK, QK_W), F32)
    g2 = jnp.concatenate([jnp.concatenate([gk_w[0], zeros], axis=1),
                          jnp.concatenate([zeros, gk_w[1]], axis=1)], axis=0)
    w0 = dict(wq_a=piece(0), wkT_a=piece(1).T, wv_a=piece(2), wg_a=piece(3),
              wq_b=piece(4), wkT_b=piece(5).T, wv_b=piece(6), wg_b=piece(7), wlr=piece(8),
              g2T=g2.T.astype(BF16), gbc=ev_gla_gk_b[0].reshape(2 * QK_W, 1))
    qa, kaT, va, ga, qb, kbT, vb, gbv, lfT, lbT = _in_proj0(x2, ctx2, mods, w0)

    theta = ev_ret_theta[0]
    th_row = jnp.repeat(theta, HEAD_DK, axis=1).reshape(2, 2, 2 * HEAD_DK).transpose(1, 0, 2)
    th_col = th_row.transpose(0, 2, 1)
    y_ret, yc_ret = _scan_group(qa, kaT, va, ga, (th_row, th_col), gla=False)
    y_gla, yc_gla = _scan_group(qb, kbT, vb, gbv, (lfT, lbT, ev_gla_norm_g[0].reshape(1, HEAD_DV)), gla=True)

    xs = _out_mlp((y_ret, y_gla, yc_ret, yc_gla), x2, ctx2, mods, ln,
                  ev_w_out[0].astype(BF16), mlp_w1[0].astype(BF16), mlp_w2[0].astype(BF16),
                  layer=0, n_tiles=N_TILES, split_ctx=True)

    wqkv = od_w_qkv[0].astype(BF16)
    tabs = _rope_tables()
    tabsT = tuple(a.T for a in tabs)
    q1T, k1, v1T = _in_proj1(xs, mods, tabs, tabsT, wqkv[:, 0:D].T, wqkv[:, D:D + ATT_KVW],
                             wqkv[:, D + ATT_KVW:D + 2 * ATT_KVW].T)
    att = _attention(od_sink[0], q1T, k1, v1T)
    out = _out_mlp((att,), xs, None, mods, ln,
                   od_w_out[0].astype(BF16), mlp_w1[1].astype(BF16), mlp_w2[1].astype(BF16),
                   layer=1, n_tiles=N_LAT_TILES, split_ctx=False)
    return out.reshape(B, T, D)
```

```python
import functools

import jax
import jax.numpy as jnp
import numpy as np
from jax import lax
from jax.experimental import pallas as pl
from jax.experimental.pallas import tpu as pltpu

F32 = jnp.float32
BF16 = jnp.bfloat16

D = 1024
B = 2
T = 8192
L = 256
DEPTH = 2
GRID_W = 64
D_FF = 4 * D
HEAD_DV = 128
HEAD_DK = 64
GATE_RANK = 16
GATE_TAU = 16.0
QK_W = 256
V_W = 512
ATT_DH = 64
ATT_QH = 16
ATT_KVH = 4
ATT_GROUP = 4
ATT_KVW = ATT_KVH * ATT_DH
WINDOW = 128
ROPE_BASE = 10000.0
ALPHA = (2.0 * DEPTH) ** 0.25
LN_EPS = 1e-5
RMS_EPS = 1e-6

TM = 512
LAT_ROWS = B * T
ROWS = LAT_ROWS + B * L
N_LAT_TILES = LAT_ROWS // TM
N_TILES = ROWS // TM
TILES_PER_BATCH = T // TM

CH = 64
SP = 2 * CH
TB = 1024
NP = TB // SP
NP_CTX = L // SP
NBLK = T // TB
NP_ALL = NP_CTX + T // SP
N_SLABS = ROWS // SP
KV_UNROLL = 8
OUT_UNROLL = 8

TQ = 512
QB = 128
KWIN = 3 * QB
LOG2E = 1.4426950408889634

VMEM_LIMIT = 56 * 1024 * 1024


def _dot(a, b):
    return jnp.dot(a, b, preferred_element_type=F32)


def _dot_nt(a, b):
    return lax.dot_general(a, b, (((1,), (1,)), ((), ())), preferred_element_type=F32)


def _full_spec(shape):
    nd = len(shape)
    return pl.BlockSpec(shape, lambda *_: (0,) * nd, pipeline_mode=pl.Buffered(1))


def _mod_row(t):
    return jnp.minimum(t // TILES_PER_BATCH, B)


def _mod_spec(layer):
    return pl.BlockSpec((None, 1, 6 * D), lambda t: (layer * 8 + _mod_row(t), 0, 0))


def _layer_norm(x, g, b):
    mu = jnp.mean(x, axis=-1, keepdims=True)
    xc = x - mu
    var = jnp.mean(xc * xc, axis=-1, keepdims=True)
    return xc * lax.rsqrt(var + LN_EPS) * g + b


def _log_sigmoid(z):
    return jnp.minimum(z, 0.0) - jnp.log1p(jnp.exp(-jnp.abs(z)))


MOD_TN = 1536


def _mod_kernel(c_ref, w_ref, b_ref, o_ref):
    s = jax.nn.silu(c_ref[...])
    o_ref[...] = jnp.dot(s, w_ref[...], precision=lax.Precision.HIGHEST,
                         preferred_element_type=F32) + b_ref[...]


def _modulation(cs, w_mod, b_mod):
    return pl.pallas_call(
        _mod_kernel,
        grid=(DEPTH, 6 * D // MOD_TN),
        in_specs=[
            pl.BlockSpec((8, D), lambda i, n: (0, 0)),
            pl.BlockSpec((None, D, MOD_TN), lambda i, n: (i, 0, n)),
            pl.BlockSpec((None, 1, MOD_TN), lambda i, n: (i, 0, n)),
        ],
        out_specs=pl.BlockSpec((None, 8, MOD_TN), lambda i, n: (i, 0, n)),
        out_shape=jax.ShapeDtypeStruct((DEPTH, 8, 6 * D), F32),
        compiler_params=pltpu.CompilerParams(
            dimension_semantics=("arbitrary", "arbitrary"), vmem_limit_bytes=VMEM_LIMIT),
        name="modulation",
    )(cs, w_mod, b_mod.reshape(DEPTH, 1, 6 * D))


IN0_OFF = (0, 256, 512, 1024, 1536, 1792, 2048, 2560, 3072, 3104)


def _transpose_bf16(w):
    return w.astype(F32).T.astype(BF16)


def _in0_kernel(x_ref, ctx_ref, mod_ref, w_ref, g2T, gbc,
                qa_o, kaT_o, va_o, ga_o, qb_o, kbT_o, vb_o, gb_o, lfT_o, lbT_o, wkT_s):
    t = pl.program_id(0)
    piece = lambda i: w_ref[:, IN0_OFF[i]:IN0_OFF[i + 1]]

    @pl.when(t == 0)
    def _():
        wkT_s[0] = _transpose_bf16(piece(1))
        wkT_s[1] = _transpose_bf16(piece(5))

    xt = jnp.where(t == N_LAT_TILES, ctx_ref[...], x_ref[...])
    sh1 = mod_ref[:, 0:D]
    sc1 = mod_ref[:, D:2 * D]
    h = (xt * (1.0 + sc1) + sh1).astype(BF16)
    qk_scale = HEAD_DK ** -0.5

    def put_slabs(o_ref, val):
        for i in range(TM // SP):
            o_ref[i] = val[:, i * SP:(i + 1) * SP].astype(o_ref.dtype)

    qa_o[...] = _dot(h, piece(0)).astype(BF16)
    put_slabs(kaT_o, _dot_nt(wkT_s[0], h) * qk_scale)
    va_o[...] = _dot(h, piece(2)).astype(BF16)
    ga_o[...] = jax.nn.silu(_dot(h, piece(3))).astype(BF16)

    qb_o[...] = (_dot(h, piece(4)) * qk_scale).astype(BF16)
    put_slabs(kbT_o, _dot_nt(wkT_s[1], h))
    vb_o[...] = _dot(h, piece(6)).astype(BF16)
    gb_o[...] = jax.nn.silu(_dot(h, piece(7))).astype(BF16)

    lr = _dot(h, piece(8)).astype(BF16)
    lsT = _log_sigmoid(_dot_nt(g2T[...], lr) + gbc[...]) * (1.0 / GATE_TAU)
    put_slabs(lfT_o, lsT[0:QK_W, :])
    put_slabs(lbT_o, lsT[QK_W:2 * QK_W, :])


def _in_proj0(x2, ctx2, mods, w_in, g2T, gbc):
    row = lambda t: (t, 0)
    slab = lambda t: (t, 0, 0)
    nsl = TM // SP
    row_out = lambda width: (jax.ShapeDtypeStruct((ROWS, width), BF16), pl.BlockSpec((TM, width), row))
    slab_out = lambda dt: (jax.ShapeDtypeStruct((N_SLABS, QK_W, SP), dt), pl.BlockSpec((nsl, QK_W, SP), slab))
    outs = [row_out(QK_W), slab_out(BF16), row_out(V_W), row_out(V_W),
            row_out(QK_W), slab_out(BF16), row_out(V_W), row_out(V_W),
            slab_out(BF16), slab_out(BF16)]
    in_specs = [
        pl.BlockSpec((TM, D), lambda t: (jnp.minimum(t, N_LAT_TILES - 1), 0)),
        _full_spec((B * L, D)),
        _mod_spec(0),
        pl.BlockSpec((None,) + w_in.shape[1:], lambda t: (0, 0, 0), pipeline_mode=pl.Buffered(1)),
        _full_spec(g2T.shape), _full_spec(gbc.shape),
    ]
    return pl.pallas_call(
        _in0_kernel,
        grid=(N_TILES,),
        in_specs=in_specs,
        out_specs=[o[1] for o in outs],
        out_shape=[o[0] for o in outs],
        scratch_shapes=[pltpu.VMEM((2, QK_W, D), BF16)],
        compiler_params=pltpu.CompilerParams(
            dimension_semantics=("arbitrary",), vmem_limit_bytes=VMEM_LIMIT),
        name="in_proj0",
    )(x2, ctx2, mods, w_in, g2T, gbc)


def _chunk_diag(kv, c):
    r0 = c * 2 * HEAD_DK
    return jnp.concatenate([kv[r0:r0 + HEAD_DK, 0:HEAD_DV],
                            kv[r0 + HEAD_DK:r0 + 2 * HEAD_DK, HEAD_DV:2 * HEAD_DV]], axis=0)


def _scan_kernel(*refs, gla):
    if gla:
        (q_ref, kT_ref, v_ref, g_ref, lfT_ref, lbT_ref,
         qc_ref, kTc_ref, vc_ref, gc_ref, lfTc_ref, lbTc_ref, ng_ref,
         y_ref, yc_ref, s_ref, r_ref, rst_ref, sst_ref, kv_ref, dec_ref,
         kd_ref, lhs_ref, pm_ref) = refs
        lat = (q_ref, kT_ref, v_ref, g_ref, lfT_ref, lbT_ref)
        cxt = (qc_ref, kTc_ref, vc_ref, gc_ref, lfTc_ref, lbTc_ref)
    else:
        (q_ref, kT_ref, v_ref, g_ref, qc_ref, kTc_ref, vc_ref, gc_ref, th_ref,
         y_ref, yc_ref, s_ref, r_ref, rst_ref, sst_ref, kv_ref, dec_ref,
         kd_ref, lhs_ref, pm_ref) = refs
        lat = (q_ref, kT_ref, v_ref, g_ref, None, None)
        cxt = (qc_ref, kTc_ref, vc_ref, gc_ref, None, None)

    phase = pl.program_id(2)
    j = pl.program_id(3)

    ri = lax.broadcasted_iota(jnp.int32, (SP, SP), 0)
    ci = lax.broadcasted_iota(jnp.int32, (SP, SP), 1)
    same = (ri // CH) == (ci // CH)
    first_lane = ci < CH
    head_a = ci < HEAD_DK

    if gla:
        as_w = lambda m: m.astype(BF16)
        tot = jnp.concatenate([jnp.broadcast_to(ri < CH, (SP, SP)), jnp.broadcast_to(ri >= CH, (SP, SP))], axis=1)
        w_end_f = jnp.concatenate([as_w(same & (ri > ci)), as_w(tot)], axis=1)
        w_end_b = jnp.concatenate([as_w(same & (ri < ci)), as_w(tot)], axis=1)
        w_cum_f = as_w(same & (ri <= ci))
        w_cum_b = as_w(same & (ri >= ci))
    else:
        hp = pl.program_id(1)
        th = [[th_ref[0, dr, 2 * hp + hd] for hd in range(2)] for dr in range(2)]
        lane1 = lax.broadcasted_iota(jnp.int32, (1, SP), 1) < HEAD_DK
        row1 = lax.broadcasted_iota(jnp.int32, (SP, 1), 0) < HEAD_DK
        lg_row = [jnp.log1p(-jnp.exp(jnp.where(lane1, th[dr][0], th[dr][1]))) for dr in range(2)]
        lg_col = [jnp.log1p(-jnp.exp(jnp.where(row1, th[dr][0], th[dr][1]))) for dr in range(2)]
        it = (ci % CH).astype(F32)
        ir = (ri % CH).astype(F32)
        ret_end_f = jnp.exp((CH - 1.0 - it) * lg_col[0])
        ret_end_b = jnp.exp(it * lg_col[1])
        ret_dec_f = jnp.exp(jnp.broadcast_to(CH * lg_col[0], (SP, SP)))
        ret_dec_b = jnp.exp(jnp.broadcast_to(CH * lg_col[1], (SP, SP)))
        ret_ebTi = jnp.exp(-(it + 1.0) * lg_col[0])
        ret_erTi = jnp.exp(-(CH - it) * lg_col[1])
        ret_eb = jnp.exp((ir + 1.0) * lg_row[0])
        ret_er = jnp.exp((CH - ir) * lg_row[1])

    def kv_stage(blk, n, fwd):
        _, kT_r, v_r, _, lfT_r, lbT_r = blk

        def body(p, carry):
            kT = kT_r[p].astype(F32)
            v = v_r[pl.ds(pl.multiple_of(p * SP, SP), SP), :]
            if gla:
                res = _dot((lfT_r if fwd else lbT_r)[p], w_end_f if fwd else w_end_b)
                e_end = jnp.exp(res[:, 0:SP])
                dec0 = jnp.exp(res[:, SP:2 * SP])
                dec1 = jnp.exp(res[:, 2 * SP:3 * SP])
            else:
                e_end = ret_end_f if fwd else ret_end_b
                dec0 = dec1 = ret_dec_f if fwd else ret_dec_b
            ke = kT * e_end
            lhs = jnp.concatenate([jnp.where(first_lane, ke, 0.0), jnp.where(first_lane, 0.0, ke)],
                                  axis=0).astype(BF16)
            kv = _dot(lhs, v)
            kv_ref[2 * p] = _chunk_diag(kv, 0)
            kv_ref[2 * p + 1] = _chunk_diag(kv, 1)
            dec_ref[2 * p] = dec0
            dec_ref[2 * p + 1] = dec1
            return carry

        lax.fori_loop(0, n, body, 0, unroll=min(n, KV_UNROLL))

    def phase0_block(blk, n, slot0):
        kv_stage(blk, n, fwd=False)

        def body(i, r_state):
            p = n - 1 - i
            rst_ref[slot0 + p, :, HEAD_DV:2 * HEAD_DV] = r_state.astype(BF16)
            r_state = dec_ref[2 * p + 1] * r_state + kv_ref[2 * p + 1]
            rst_ref[slot0 + p, :, 0:HEAD_DV] = r_state.astype(BF16)
            return dec_ref[2 * p] * r_state + kv_ref[2 * p]

        r_ref[...] = lax.fori_loop(0, n, body, r_ref[...])

    def phase1_block(blk, n, slot0, out_ref):
        q_r, kT_r, v_r, g_r, lfT_r, lbT_r = blk
        kv_stage(blk, n, fwd=True)

        def rec(p, s_state):
            sst_ref[p, :, 0:HEAD_DV] = s_state.astype(BF16)
            s_state = dec_ref[2 * p] * s_state + kv_ref[2 * p]
            sst_ref[p, :, HEAD_DV:2 * HEAD_DV] = s_state.astype(BF16)
            return dec_ref[2 * p + 1] * s_state + kv_ref[2 * p + 1]

        s_ref[...] = lax.fori_loop(0, n, rec, s_ref[...])

        r2 = lax.broadcasted_iota(jnp.int32, (2 * SP, SP), 0) % SP
        c2 = lax.broadcasted_iota(jnp.int32, (2 * SP, SP), 1)
        same2 = (r2 // CH) == (c2 // CH)
        mask_f = same2 & (r2 >= c2)
        mask_b = same2 & (r2 < c2)

        def prep(p, carry):
            rows = pl.ds(pl.multiple_of(p * SP, SP), SP)
            q = q_r[rows, :].astype(F32)
            kT = kT_r[p].astype(F32)
            if gla:
                lfT = lfT_r[p]
                lbT = lbT_r[p]
                ebTi = jnp.exp(-_dot(lfT, w_cum_f))
                erTi = jnp.exp(-_dot(lbT, w_cum_b))
                e_b = jnp.exp(_dot_nt(w_cum_b, lfT))
                e_r = jnp.exp(_dot_nt(w_cum_f, lbT))
            else:
                ebTi, erTi, e_b, e_r = ret_ebTi, ret_erTi, ret_eb, ret_er
            kd_ref[p, 0:SP, 0:SP] = (kT * ebTi).astype(BF16)
            kd_ref[p, SP:2 * SP, SP:2 * SP] = (kT * erTi).astype(BF16)
            qf = q * e_b
            qb = q * e_r
            lhs_ref[p, :, 0:SP] = jnp.concatenate(
                [jnp.where(head_a, qf, 0.0), jnp.where(head_a, 0.0, qf)], axis=0).astype(BF16)
            lhs_ref[p, :, SP:2 * SP] = jnp.concatenate(
                [jnp.where(head_a, qb, 0.0), jnp.where(head_a, 0.0, qb)], axis=0).astype(BF16)
            return carry

        def score(p, carry):
            sc = _dot(lhs_ref[p], kd_ref[p])
            pm_ref[p] = jnp.where(mask_f, sc[:, 0:SP], jnp.where(mask_b, sc[:, SP:2 * SP], 0.0)).astype(BF16)
            return carry

        def emit(p, carry):
            rows = pl.ds(pl.multiple_of(p * SP, SP), SP)
            v = v_r[rows, :]
            states = jnp.concatenate([sst_ref[p], rst_ref[slot0 + p]], axis=0)
            o_int = _dot(lhs_ref[p], states)
            o_a = _dot(pm_ref[p, 0:SP, :], v[:, 0:HEAD_DV]) + jnp.concatenate(
                [o_int[0:CH, 0:HEAD_DV], o_int[CH:SP, HEAD_DV:2 * HEAD_DV]], axis=0)
            o_b = _dot(pm_ref[p, SP:2 * SP, :], v[:, HEAD_DV:2 * HEAD_DV]) + jnp.concatenate(
                [o_int[SP:SP + CH, 0:HEAD_DV], o_int[SP + CH:2 * SP, HEAD_DV:2 * HEAD_DV]], axis=0)

            def nrm(o):
                y = o * lax.rsqrt(jnp.mean(o * o, axis=-1, keepdims=True) + RMS_EPS)
                return y * ng_ref[...] if gla else y

            y = jnp.concatenate([nrm(o_a), nrm(o_b)], axis=1) * g_r[rows, :].astype(F32)
            out_ref[rows, :] = y.astype(BF16)
            return carry

        lax.fori_loop(0, n, prep, 0, unroll=min(n, OUT_UNROLL))
        lax.fori_loop(0, n, score, 0, unroll=min(n, OUT_UNROLL))
        lax.fori_loop(0, n, emit, 0, unroll=min(n, OUT_UNROLL))

    @pl.when((pl.program_id(0) == 0) & (pl.program_id(1) == 0) & (phase == 0) & (j == 0))
    def _():
        kd_ref[...] = jnp.zeros_like(kd_ref)

    @pl.when(phase == 0)
    def _():
        @pl.when(j == 0)
        def _():
            r_ref[...] = jnp.zeros_like(r_ref)
            phase0_block(cxt, NP_CTX, 0)

        phase0_block(lat, NP, NP_CTX + (NBLK - 1 - j) * NP)

    @pl.when(phase == 1)
    def _():
        @pl.when(j == 0)
        def _():
            s_ref[...] = jnp.zeros_like(s_ref)
            phase1_block(cxt, NP_CTX, 0, yc_ref)

        phase1_block(lat, NP, NP_CTX + j * NP, y_ref)


def _scan_group(q, kT, v, g, extra, *, gla):
    def blk(b, ph, j, used_in_phase0):
        jj = jnp.where(ph == 0, NBLK - 1 - j, j)
        if not used_in_phase0:
            jj = jnp.where(ph == 0, 0, jj)
        return b * NBLK + jj

    def lat_specs(used0):
        return dict(
            row=lambda w: pl.BlockSpec((TB, w), lambda b, p, ph, j: (blk(b, ph, j, used0), p)),
            slab=pl.BlockSpec((NP, SP, SP), lambda b, p, ph, j: (blk(b, ph, j, used0), p, 0)))

    ctx_row = lambda w: pl.BlockSpec((L, w), lambda b, p, ph, j: (LAT_ROWS // L + b, p))
    ctx_slab = pl.BlockSpec((NP_CTX, SP, SP), lambda b, p, ph, j: (LAT_ROWS // L + b, p, 0))
    used, unused = lat_specs(True), lat_specs(False)

    in_specs = [unused["row"](2 * HEAD_DK), used["slab"], used["row"](2 * HEAD_DV), unused["row"](2 * HEAD_DV)]
    ctx_specs = [ctx_row(2 * HEAD_DK), ctx_slab, ctx_row(2 * HEAD_DV), ctx_row(2 * HEAD_DV)]
    if gla:
        lfT, lbT, ng = extra
        in_specs += [unused["slab"], used["slab"]]
        ctx_specs += [ctx_slab, ctx_slab]
        args = (q, kT, v, g, lfT, lbT, q, kT, v, g, lfT, lbT, ng)
        in_specs = in_specs + ctx_specs + [pl.BlockSpec((1, HEAD_DV), lambda b, p, ph, j: (0, 0))]
    else:
        (theta,) = extra
        args = (q, kT, v, g, q, kT, v, g, theta)
        in_specs = in_specs + ctx_specs + [pl.BlockSpec(memory_space=pltpu.SMEM)]
    return pl.pallas_call(
        functools.partial(_scan_kernel, gla=gla),
        grid=(B, 2, 2, NBLK),
        in_specs=in_specs,
        out_specs=[
            pl.BlockSpec((TB, 2 * HEAD_DV), lambda b, p, ph, j: (b * NBLK + jnp.where(ph == 0, 0, j), p)),
            pl.BlockSpec((L, 2 * HEAD_DV), lambda b, p, ph, j: (b, p)),
        ],
        out_shape=[jax.ShapeDtypeStruct((LAT_ROWS, V_W), BF16), jax.ShapeDtypeStruct((B * L, V_W), BF16)],
        scratch_shapes=[
            pltpu.VMEM((SP, HEAD_DV), F32),
            pltpu.VMEM((SP, HEAD_DV), F32),
            pltpu.VMEM((NP_ALL, SP, 2 * HEAD_DV), BF16),
            pltpu.VMEM((NP, SP, 2 * HEAD_DV), BF16),
            pltpu.VMEM((2 * NP, SP, HEAD_DV), F32),
            pltpu.VMEM((2 * NP, SP, HEAD_DV), F32),
            pltpu.VMEM((NP, 2 * SP, 2 * SP), BF16),
            pltpu.VMEM((NP, 2 * SP, 2 * SP), BF16),
            pltpu.VMEM((NP, 2 * SP, SP), BF16),
        ],
        compiler_params=pltpu.CompilerParams(
            dimension_semantics=("arbitrary",) * 4, vmem_limit_bytes=VMEM_LIMIT),
        name="scan_gla" if gla else "scan_ret",
    )(*args)


FF_CH = 512


def _out_kernel(*refs, layer, split_ctx):
    if split_ctx:
        (ya_ref, yb_ref, yac_ref, ybc_ref, x_ref, ctx_ref, mod_ref, lng_ref, lnb_ref,
         wo_ref, w1_ref, w2_ref, o_ref) = refs
        is_ctx = pl.program_id(0) == N_LAT_TILES
        x = jnp.where(is_ctx, ctx_ref[...], x_ref[...])
        ya = jnp.where(is_ctx, yac_ref[...], ya_ref[...])
        yb = jnp.where(is_ctx, ybc_ref[...], yb_ref[...])
    else:
        ya_ref, yb_ref, x_ref, mod_ref, lng_ref, lnb_ref, wo_ref, w1_ref, w2_ref, o_ref = refs
        x = x_ref[...]
        ya = ya_ref[...]
        yb = yb_ref[...]
    g1 = mod_ref[:, 2 * D:3 * D]
    sh2 = mod_ref[:, 3 * D:4 * D]
    sc2 = mod_ref[:, 4 * D:5 * D]
    g2 = mod_ref[:, 5 * D:6 * D]
    ln_g0 = lng_ref[2 * layer:2 * layer + 1, :]
    ln_g1 = lng_ref[2 * layer + 1:2 * layer + 2, :]
    ln_b0 = lnb_ref[2 * layer:2 * layer + 1, :]
    ln_b1 = lnb_ref[2 * layer + 1:2 * layer + 2, :]

    half = D // 2
    y = _dot(ya, wo_ref[0:half, :]) + _dot(yb, wo_ref[half:D, :])
    x1 = _layer_norm(ALPHA * x + g1 * y, ln_g0, ln_b0)
    h2 = (x1 * (1.0 + sc2) + sh2).astype(BF16)
    acc = jnp.zeros((TM, D), F32)
    for c in range(D_FF // FF_CH):
        cols = slice(c * FF_CH, (c + 1) * FF_CH)
        hc = jnp.maximum(_dot(h2, w1_ref[:, cols]), 0.0)
        acc = acc + _dot((hc * hc).astype(BF16), w2_ref[cols, :])
    o_ref[...] = _layer_norm(ALPHA * x1 + g2 * acc, ln_g1, ln_b1)


def _out_mlp(ys, xs, ctx2, mods, ln, wo, w1, w2, *, layer, n_tiles, split_ctx):
    half = D // 2
    lat_row = lambda t: (jnp.minimum(t, N_LAT_TILES - 1), 0)
    if split_ctx:
        ya, yb, yac, ybc = ys
        in_specs = [pl.BlockSpec((TM, half), lat_row), pl.BlockSpec((TM, half), lat_row),
                    _full_spec((B * L, half)), _full_spec((B * L, half)),
                    pl.BlockSpec((TM, D), lat_row), _full_spec((B * L, D))]
        args = [ya, yb, yac, ybc, xs, ctx2]
    else:
        (att,) = ys
        in_specs = [pl.BlockSpec((TM, half), lambda t: (t, 0)), pl.BlockSpec((TM, half), lambda t: (t, 1)),
                    pl.BlockSpec((TM, D), lambda t: (t, 0))]
        args = [att, att, xs]
    stacked = lambda w, i: pl.BlockSpec((None,) + w.shape[1:], lambda t: (i, 0, 0), pipeline_mode=pl.Buffered(1))
    in_specs += [_mod_spec(layer), _full_spec(ln[0].shape), _full_spec(ln[1].shape),
                 stacked(wo, 0), stacked(w1, layer), stacked(w2, layer)]
    args += [mods, ln[0], ln[1], wo, w1, w2]
    return pl.pallas_call(
        functools.partial(_out_kernel, layer=layer, split_ctx=split_ctx),
        grid=(n_tiles,),
        in_specs=in_specs,
        out_specs=pl.BlockSpec((TM, D), lambda t: (t, 0)),
        out_shape=jax.ShapeDtypeStruct((n_tiles * TM, D), F32),
        compiler_params=pltpu.CompilerParams(
            dimension_semantics=("arbitrary",), vmem_limit_bytes=VMEM_LIMIT),
        name="out_mlp%d" % layer,
    )(*args)


def _in1_kernel(x_ref, mod_ref, cos_ref, sin_ref, w_ref, qT_o, k_o, vT_o, wqT_s, wvT_s):
    @pl.when(pl.program_id(0) == 0)
    def _():
        wqT_s[...] = _transpose_bf16(w_ref[:, 0:D])
        wvT_s[...] = _transpose_bf16(w_ref[:, D + ATT_KVW:D + 2 * ATT_KVW])

    sh1 = mod_ref[:, 0:D]
    sc1 = mod_ref[:, D:2 * D]
    h = (x_ref[...] * (1.0 + sc1) + sh1).astype(BF16)
    nsl = TM // SP

    cos = cos_ref[...]
    sin = sin_ref[...]
    first = (lax.broadcasted_iota(jnp.int32, (TM, 128), 1) % 32) < 16
    sa = jnp.where(first, -sin, 0.0)
    sb = jnp.where(first, 0.0, sin)

    k = _dot(h, w_ref[:, D:D + ATT_KVW])
    for i in range(ATT_KVW // 128):
        u = k[:, i * 128:(i + 1) * 128]
        r = u * cos + pltpu.roll(u, 128 - 16, 1) * sa + pltpu.roll(u, 16, 1) * sb
        k_o[:, i * 128:(i + 1) * 128] = r.astype(BF16)

    cosT, saT, sbT = cos.T, sa.T, sb.T
    qT = _dot_nt(wqT_s[...], h) * (LOG2E * ATT_DH ** -0.5)
    for i in range(D // 128):
        u = qT[i * 128:(i + 1) * 128, :]
        r = (u * cosT + pltpu.roll(u, 128 - 16, 0) * saT + pltpu.roll(u, 16, 0) * sbT).astype(BF16)
        for s in range(nsl):
            qT_o[s, i * 128:(i + 1) * 128, :] = r[:, s * SP:(s + 1) * SP]

    vT = _dot_nt(wvT_s[...], h).astype(BF16)
    for s in range(nsl):
        vT_o[s] = vT[:, s * SP:(s + 1) * SP]


def _in_proj1(xs, mods, cos_tab, sin_tab, wqkv):
    tile = lambda t: jnp.where(t == N_LAT_TILES, TILES_PER_BATCH, t % TILES_PER_BATCH)
    row = lambda t: (t, 0)
    slab = lambda t: (t, 0, 0)
    nsl = TM // SP
    tab_spec = pl.BlockSpec((TM, 128), lambda t: (tile(t), 0))
    return pl.pallas_call(
        _in1_kernel,
        grid=(N_TILES,),
        in_specs=[pl.BlockSpec((TM, D), row), _mod_spec(1), tab_spec, tab_spec,
                  pl.BlockSpec((None,) + wqkv.shape[1:], lambda t: (0, 0, 0), pipeline_mode=pl.Buffered(1))],
        out_specs=[pl.BlockSpec((nsl, D, SP), slab), pl.BlockSpec((TM, ATT_KVW), row),
                   pl.BlockSpec((nsl, ATT_KVW, SP), slab)],
        out_shape=[jax.ShapeDtypeStruct((N_SLABS, D, SP), BF16), jax.ShapeDtypeStruct((ROWS, ATT_KVW), BF16),
                   jax.ShapeDtypeStruct((N_SLABS, ATT_KVW, SP), BF16)],
        scratch_shapes=[pltpu.VMEM((D, D), BF16), pltpu.VMEM((ATT_KVW, D), BF16)],
        compiler_params=pltpu.CompilerParams(
            dimension_semantics=("arbitrary",), vmem_limit_bytes=VMEM_LIMIT),
        name="in_proj1",
    )(xs, mods, cos_tab, sin_tab, wqkv)


def _attn_kernel(sink_ref, qT_ref, k_ref, vT_ref, kc_ref, vcT_ref, o_ref, bias_ref):
    n = pl.program_id(1)
    nql = ATT_GROUP * QB
    lane_g = lax.broadcasted_iota(jnp.int32, (1, nql), 1) // QB
    zero_half = jnp.zeros((ATT_DH, nql), BF16)
    ones_rows = jnp.ones((16, KWIN + L), BF16)

    @pl.when((pl.program_id(0) == 0) & (n == 0))
    def _():
        kj = lax.broadcasted_iota(jnp.int32, (KWIN, QB), 0)
        qi = lax.broadcasted_iota(jnp.int32, (KWIN, QB), 1)
        for case, delta in enumerate((-QB, 0, -2 * QB)):
            d = kj - qi + delta
            bias_ref[case] = jnp.where((d >= -WINDOW) & (d <= WINDOW), 0.0, -jnp.inf)

    def scores(i, kh):
        n0 = (n * (TQ // QB) + i) * QB
        start = pl.multiple_of(jnp.clip(n0 - QB, 0, T - KWIN), QB)
        bias1 = bias_ref[jnp.where(n0 == 0, 1, jnp.where(n0 == T - QB, 2, 0))]
        bias = jnp.concatenate([bias1] * ATT_GROUP, axis=1)
        pair = slice((kh // 2) * 128, (kh // 2 + 1) * 128)
        qT = jnp.concatenate(
            [qT_ref[i, (kh * ATT_GROUP + g) * ATT_DH:(kh * ATT_GROUP + g + 1) * ATT_DH, :]
             for g in range(ATT_GROUP)], axis=1)
        q_pad = jnp.concatenate([zero_half, qT] if kh % 2 else [qT, zero_half], axis=0)
        s_loc = _dot(k_ref[pl.ds(start, KWIN), pair], q_pad) + bias
        s_ctx = _dot(kc_ref[:, pair], q_pad)
        return s_loc, s_ctx, start // SP

    def finish(i, kh, s_loc, s_ctx, slab0):
        drows = slice(kh * ATT_DH, (kh + 1) * ATT_DH)
        sink = jnp.zeros((1, nql), F32)
        for g in range(ATT_GROUP):
            sink = jnp.where(lane_g == g, sink_ref[0, kh * ATT_GROUP + g] * LOG2E, sink)
        m = jnp.maximum(jnp.maximum(jnp.max(s_loc, axis=0, keepdims=True),
                                    jnp.max(s_ctx, axis=0, keepdims=True)), sink)
        pT = jnp.concatenate([jnp.exp2(s_loc - m).astype(BF16),
                              jnp.exp2(s_ctx - m).astype(BF16)], axis=0)
        vT = jnp.concatenate([vT_ref[slab0 + t, drows, :] for t in range(KWIN // SP)]
                             + [vcT_ref[t, drows, :] for t in range(L // SP)], axis=1)
        o_ext = _dot(jnp.concatenate([vT, ones_rows], axis=0), pT)
        denom = o_ext[ATT_DH:ATT_DH + 1, :] + jnp.exp2(sink - m)
        oT = o_ext[0:ATT_DH, :] / denom
        for g in range(0, ATT_GROUP, 2):
            two = jnp.concatenate([oT[:, g * QB:(g + 1) * QB], oT[:, (g + 1) * QB:(g + 2) * QB]], axis=0)
            c0 = (kh * ATT_GROUP + g) * ATT_DH
            o_ref[i * QB:(i + 1) * QB, c0:c0 + 2 * ATT_DH] = two.T.astype(BF16)

    items = [(i, kh) for i in range(TQ // QB) for kh in range(ATT_KVH)]
    pending = scores(*items[0])
    for t, item in enumerate(items):
        nxt = scores(*items[t + 1]) if t + 1 < len(items) else None
        finish(*item, *pending)
        pending = nxt


def _attention(sink, qT, k, vT):
    nq = T // TQ
    return pl.pallas_call(
        _attn_kernel,
        grid=(B, nq),
        in_specs=[
            pl.BlockSpec(memory_space=pltpu.SMEM),
            pl.BlockSpec((TQ // SP, D, SP), lambda b, n: (b * nq + n, 0, 0)),
            pl.BlockSpec((T, ATT_KVW), lambda b, n: (b, 0)),
            pl.BlockSpec((T // SP, ATT_KVW, SP), lambda b, n: (b, 0, 0)),
            pl.BlockSpec((L, ATT_KVW), lambda b, n: (LAT_ROWS // L + b, 0)),
            pl.BlockSpec((L // SP, ATT_KVW, SP), lambda b, n: (LAT_ROWS // L + b, 0, 0)),
        ],
        out_specs=pl.BlockSpec((TQ, D), lambda b, n: (b * nq + n, 0)),
        out_shape=jax.ShapeDtypeStruct((LAT_ROWS, D), BF16),
        compiler_params=pltpu.CompilerParams(
            dimension_semantics=("arbitrary", "arbitrary"), vmem_limit_bytes=VMEM_LIMIT),
        scratch_shapes=[pltpu.VMEM((3, KWIN, QB), F32)],
        name="window_attn",
    )(sink, qT, k, vT, k, vT)


def _rope_tables():
    half = ATT_DH // 2
    inv_freq = np.power(np.float32(ROPE_BASE), -np.arange(0, half, 2, dtype=np.float32) / np.float32(half))
    inv_freq = inv_freq.astype(np.float32)
    pos = np.arange(T)
    ang_r = (pos // GRID_W).astype(np.float32)[:, None] * inv_freq[None, :]
    ang_c = (pos % GRID_W).astype(np.float32)[:, None] * inv_freq[None, :]
    ang = np.concatenate([ang_r, ang_r, ang_c, ang_c], axis=1)
    ang = np.concatenate([ang, np.zeros((TM, ATT_DH), np.float32)], axis=0)
    ang = np.concatenate([ang, ang], axis=1)
    return np.cos(ang).astype(np.float32), np.sin(ang).astype(np.float32)


def kernel(x, c, ctx, c_ctx, w_mod, b_mod, ln_g, ln_b, mlp_w1, mlp_w2, ev_w_in, ev_ret_theta, ev_gla_gk_w,
           ev_gla_gk_b, ev_gla_norm_g, ev_w_out, od_w_qkv, od_sink, od_w_out):
    x2 = x.reshape(LAT_ROWS, D)
    ctx2 = ctx.reshape(B * L, D)

    cs = jnp.concatenate([c, c_ctx[None, :], jnp.zeros((8 - B - 1, D), F32)], axis=0)
    mods = _modulation(cs, w_mod, b_mod).reshape(DEPTH * 8, 1, 6 * D)
    ln = (ln_g.reshape(2 * DEPTH, D), ln_b.reshape(2 * DEPTH, D))
    w1 = mlp_w1.astype(BF16)
    w2 = mlp_w2.astype(BF16)

    gk_w = ev_gla_gk_w[0]
    zeros = jnp.zeros((GATE_RANK, QK_W), F32)
    g2 = jnp.concatenate([jnp.concatenate([gk_w[0], zeros], axis=1),
                          jnp.concatenate([zeros, gk_w[1]], axis=1)], axis=0)
    qa, kaT, va, ga, qb, kbT, vb, gbv, lfT, lbT = _in_proj0(
        x2, ctx2, mods, ev_w_in.astype(BF16), g2.T.astype(BF16), ev_gla_gk_b[0].reshape(2 * QK_W, 1))

    y_ret, yc_ret = _scan_group(qa, kaT, va, ga, (ev_ret_theta,), gla=False)
    y_gla, yc_gla = _scan_group(qb, kbT, vb, gbv, (lfT, lbT, ev_gla_norm_g), gla=True)

    xs = _out_mlp((y_ret, y_gla, yc_ret, yc_gla), x2, ctx2, mods, ln, ev_w_out.astype(BF16), w1, w2,
                  layer=0, n_tiles=N_TILES, split_ctx=True)

    cos_tab, sin_tab = _rope_tables()
    q1T, k1, v1T = _in_proj1(xs, mods, cos_tab, sin_tab, od_w_qkv.astype(BF16))
    att = _attention(od_sink, q1T, k1, v1T)
    out = _out_mlp((att,), xs, None, mods, ln, od_w_out.astype(BF16), w1, w2,
                   layer=1, n_tiles=N_LAT_TILES, split_ctx=False)
    return out.reshape(B, T, D)
```

```python
import functools

import jax
import jax.numpy as jnp
import numpy as np
from jax import lax
from jax.experimental import pallas as pl
from jax.experimental.pallas import tpu as pltpu

F32 = jnp.float32
BF16 = jnp.bfloat16

D = 1024
B = 2
T = 8192
L = 256
DEPTH = 2
GRID_W = 64
D_FF = 4 * D
HEAD_DV = 128
HEAD_DK = 64
GATE_RANK = 16
GATE_TAU = 16.0
QK_W = 256
V_W = 512
ATT_DH = 64
ATT_QH = 16
ATT_KVH = 4
ATT_GROUP = 4
ATT_KVW = ATT_KVH * ATT_DH
WINDOW = 128
ROPE_BASE = 10000.0
ALPHA = (2.0 * DEPTH) ** 0.25
LN_EPS = 1e-5
RMS_EPS = 1e-6

TM = 512
LAT_ROWS = B * T
ROWS = LAT_ROWS + B * L
N_LAT_TILES = LAT_ROWS // TM
N_TILES = ROWS // TM
TILES_PER_BATCH = T // TM

CH = 64
SP = 2 * CH
TB = 1024
NP = TB // SP
NP_CTX = L // SP
NBLK = T // TB
NP_ALL = NP_CTX + T // SP
N_SLABS = ROWS // SP
KV_UNROLL = 8
OUT_UNROLL = 8

TQ = 512
QB = 128
KWIN = 3 * QB
LOG2E = 1.4426950408889634

VMEM_LIMIT = 56 * 1024 * 1024


def _dot(a, b):
    return jnp.dot(a, b, preferred_element_type=F32)


def _dot_nt(a, b):
    return lax.dot_general(a, b, (((1,), (1,)), ((), ())), preferred_element_type=F32)


def _full_spec(shape):
    nd = len(shape)
    return pl.BlockSpec(shape, lambda *_: (0,) * nd, pipeline_mode=pl.Buffered(1))


def _mod_row(t):
    return jnp.minimum(t // TILES_PER_BATCH, B)


def _mod_spec(layer):
    return pl.BlockSpec((None, 1, 6 * D), lambda t: (layer * 8 + _mod_row(t), 0, 0))


def _layer_norm(x, g, b):
    mu = jnp.mean(x, axis=-1, keepdims=True)
    xc = x - mu
    var = jnp.mean(xc * xc, axis=-1, keepdims=True)
    return xc * lax.rsqrt(var + LN_EPS) * g + b


def _log_sigmoid(z):
    return jnp.minimum(z, 0.0) - jnp.log1p(jnp.exp(-jnp.abs(z)))


MOD_TN = 1536


def _mod_kernel(c_ref, w_ref, b_ref, o_ref):
    s = jax.nn.silu(c_ref[...])
    s_hi = s.astype(BF16)
    s_lo = (s - s_hi.astype(F32)).astype(BF16)
    w = w_ref[...].astype(BF16)
    o_ref[...] = _dot(s_hi, w) + _dot(s_lo, w) + b_ref[...]


def _modulation(cs, w_mod, b_mod):
    return pl.pallas_call(
        _mod_kernel,
        grid=(DEPTH, 6 * D // MOD_TN),
        in_specs=[
            pl.BlockSpec((8, D), lambda i, n: (0, 0)),
            pl.BlockSpec((None, D, MOD_TN), lambda i, n: (i, 0, n)),
            pl.BlockSpec((None, 1, MOD_TN), lambda i, n: (i, 0, n)),
        ],
        out_specs=pl.BlockSpec((None, 8, MOD_TN), lambda i, n: (i, 0, n)),
        out_shape=jax.ShapeDtypeStruct((DEPTH, 8, 6 * D), F32),
        compiler_params=pltpu.CompilerParams(
            dimension_semantics=("arbitrary", "arbitrary"), vmem_limit_bytes=VMEM_LIMIT),
        name="modulation",
    )(cs, w_mod, b_mod.reshape(DEPTH, 1, 6 * D))


IN0_OFF = (0, 256, 512, 1024, 1536, 1792, 2048, 2560, 3072, 3104)


def _transpose_bf16(w):
    return w.astype(F32).T.astype(BF16)


def _in0_kernel(x_ref, ctx_ref, mod_ref, w_ref, g2T, gbc,
                qa_o, kaT_o, va_o, ga_o, qb_o, kbT_o, vb_o, gb_o, lfT_o, lbT_o, wb_s, wkT_s):
    t = pl.program_id(0)
    cols = lambda i: slice(IN0_OFF[i], IN0_OFF[i + 1])
    piece = lambda i: wb_s[:, cols(i)]

    @pl.when(t == 0)
    def _():
        for i in range(len(IN0_OFF) - 1):
            wb_s[:, cols(i)] = w_ref[:, cols(i)].astype(BF16)
        wkT_s[0] = _transpose_bf16(w_ref[:, cols(1)])
        wkT_s[1] = _transpose_bf16(w_ref[:, cols(5)])

    xt = jnp.where(t == N_LAT_TILES, ctx_ref[...], x_ref[...])
    sh1 = mod_ref[:, 0:D]
    sc1 = mod_ref[:, D:2 * D]
    h = (xt * (1.0 + sc1) + sh1).astype(BF16)
    qk_scale = HEAD_DK ** -0.5

    def put_slabs(o_ref, val):
        for i in range(TM // SP):
            o_ref[i] = val[:, i * SP:(i + 1) * SP].astype(o_ref.dtype)

    qa_o[...] = _dot(h, piece(0)).astype(BF16)
    put_slabs(kaT_o, _dot_nt(wkT_s[0], h) * qk_scale)
    va_o[...] = _dot(h, piece(2)).astype(BF16)
    ga_o[...] = jax.nn.silu(_dot(h, piece(3))).astype(BF16)

    qb_o[...] = (_dot(h, piece(4)) * qk_scale).astype(BF16)
    put_slabs(kbT_o, _dot_nt(wkT_s[1], h))
    vb_o[...] = _dot(h, piece(6)).astype(BF16)
    gb_o[...] = jax.nn.silu(_dot(h, piece(7))).astype(BF16)

    lr = _dot(h, piece(8)).astype(BF16)
    lsT = _log_sigmoid(_dot_nt(g2T[...], lr) + gbc[...]) * (1.0 / GATE_TAU)
    put_slabs(lfT_o, lsT[0:QK_W, :])
    put_slabs(lbT_o, lsT[QK_W:2 * QK_W, :])


def _in_proj0(x2, ctx2, mods, w_in, g2T, gbc):
    row = lambda t: (t, 0)
    slab = lambda t: (t, 0, 0)
    nsl = TM // SP
    row_out = lambda width: (jax.ShapeDtypeStruct((ROWS, width), BF16), pl.BlockSpec((TM, width), row))
    slab_out = lambda dt: (jax.ShapeDtypeStruct((N_SLABS, QK_W, SP), dt), pl.BlockSpec((nsl, QK_W, SP), slab))
    outs = [row_out(QK_W), slab_out(BF16), row_out(V_W), row_out(V_W),
            row_out(QK_W), slab_out(BF16), row_out(V_W), row_out(V_W),
            slab_out(BF16), slab_out(BF16)]
    in_specs = [
        pl.BlockSpec((TM, D), lambda t: (jnp.minimum(t, N_LAT_TILES - 1), 0)),
        _full_spec((B * L, D)),
        _mod_spec(0),
        pl.BlockSpec((None,) + w_in.shape[1:], lambda t: (0, 0, 0), pipeline_mode=pl.Buffered(1)),
        _full_spec(g2T.shape), _full_spec(gbc.shape),
    ]
    return pl.pallas_call(
        _in0_kernel,
        grid=(N_TILES,),
        in_specs=in_specs,
        out_specs=[o[1] for o in outs],
        out_shape=[o[0] for o in outs],
        scratch_shapes=[pltpu.VMEM(w_in.shape[1:], BF16), pltpu.VMEM((2, QK_W, D), BF16)],
        compiler_params=pltpu.CompilerParams(
            dimension_semantics=("arbitrary",), vmem_limit_bytes=VMEM_LIMIT),
        name="in_proj0",
    )(x2, ctx2, mods, w_in, g2T, gbc)


def _chunk_diag(kv, c):
    r0 = c * 2 * HEAD_DK
    return jnp.concatenate([kv[r0:r0 + HEAD_DK, 0:HEAD_DV],
                            kv[r0 + HEAD_DK:r0 + 2 * HEAD_DK, HEAD_DV:2 * HEAD_DV]], axis=0)


def _scan_kernel(*refs, gla):
    if gla:
        (q_ref, kT_ref, v_ref, g_ref, lfT_ref, lbT_ref,
         qc_ref, kTc_ref, vc_ref, gc_ref, lfTc_ref, lbTc_ref, ng_ref, wsrc_ref,
         y_ref, yc_ref, wdst_ref, s_ref, r_ref, rst_ref, sst_ref, kv_ref, dec_ref,
         kd_ref, lhs_ref, pm_ref) = refs
        lat = (q_ref, kT_ref, v_ref, g_ref, lfT_ref, lbT_ref)
        cxt = (qc_ref, kTc_ref, vc_ref, gc_ref, lfTc_ref, lbTc_ref)
    else:
        (q_ref, kT_ref, v_ref, g_ref, qc_ref, kTc_ref, vc_ref, gc_ref, th_ref, wsrc_ref,
         y_ref, yc_ref, wdst_ref, s_ref, r_ref, rst_ref, sst_ref, kv_ref, dec_ref,
         kd_ref, lhs_ref, pm_ref) = refs
        lat = (q_ref, kT_ref, v_ref, g_ref, None, None)
        cxt = (qc_ref, kTc_ref, vc_ref, gc_ref, None, None)

    phase = pl.program_id(2)
    j = pl.program_id(3)

    wdst_ref[...] = wsrc_ref[...].astype(BF16)

    ri = lax.broadcasted_iota(jnp.int32, (SP, SP), 0)
    ci = lax.broadcasted_iota(jnp.int32, (SP, SP), 1)
    same = (ri // CH) == (ci // CH)
    first_lane = ci < CH
    head_a = ci < HEAD_DK

    if gla:
        as_w = lambda m: m.astype(BF16)
        tot = jnp.concatenate([jnp.broadcast_to(ri < CH, (SP, SP)), jnp.broadcast_to(ri >= CH, (SP, SP))], axis=1)
        w_end_f = jnp.concatenate([as_w(same & (ri > ci)), as_w(tot)], axis=1)
        w_end_b = jnp.concatenate([as_w(same & (ri < ci)), as_w(tot)], axis=1)
        w_cum_f = as_w(same & (ri <= ci))
        w_cum_b = as_w(same & (ri >= ci))
    else:
        hp = pl.program_id(1)
        th = [[th_ref[0, dr, 2 * hp + hd] for hd in range(2)] for dr in range(2)]
        lane1 = lax.broadcasted_iota(jnp.int32, (1, SP), 1) < HEAD_DK
        row1 = lax.broadcasted_iota(jnp.int32, (SP, 1), 0) < HEAD_DK
        lg_row = [jnp.log1p(-jnp.exp(jnp.where(lane1, th[dr][0], th[dr][1]))) for dr in range(2)]
        lg_col = [jnp.log1p(-jnp.exp(jnp.where(row1, th[dr][0], th[dr][1]))) for dr in range(2)]
        it = (ci % CH).astype(F32)
        ir = (ri % CH).astype(F32)
        ret_end_f = jnp.exp((CH - 1.0 - it) * lg_col[0])
        ret_end_b = jnp.exp(it * lg_col[1])
        ret_dec_f = jnp.exp(jnp.broadcast_to(CH * lg_col[0], (SP, SP)))
        ret_dec_b = jnp.exp(jnp.broadcast_to(CH * lg_col[1], (SP, SP)))
        ret_ebTi = jnp.exp(-(it + 1.0) * lg_col[0])
        ret_erTi = jnp.exp(-(CH - it) * lg_col[1])
        ret_eb = jnp.exp((ir + 1.0) * lg_row[0])
        ret_er = jnp.exp((CH - ir) * lg_row[1])

    def kv_stage(blk, n, fwd):
        _, kT_r, v_r, _, lfT_r, lbT_r = blk

        def body(p, carry):
            kT = kT_r[p].astype(F32)
            v = v_r[pl.ds(pl.multiple_of(p * SP, SP), SP), :]
            if gla:
                res = _dot((lfT_r if fwd else lbT_r)[p], w_end_f if fwd else w_end_b)
                e_end = jnp.exp(res[:, 0:SP])
                dec0 = jnp.exp(res[:, SP:2 * SP])
                dec1 = jnp.exp(res[:, 2 * SP:3 * SP])
            else:
                e_end = ret_end_f if fwd else ret_end_b
                dec0 = dec1 = ret_dec_f if fwd else ret_dec_b
            ke = kT * e_end
            lhs = jnp.concatenate([jnp.where(first_lane, ke, 0.0), jnp.where(first_lane, 0.0, ke)],
                                  axis=0).astype(BF16)
            kv = _dot(lhs, v)
            kv_ref[2 * p] = _chunk_diag(kv, 0)
            kv_ref[2 * p + 1] = _chunk_diag(kv, 1)
            dec_ref[2 * p] = dec0
            dec_ref[2 * p + 1] = dec1
            return carry

        lax.fori_loop(0, n, body, 0, unroll=min(n, KV_UNROLL))

    def phase0_block(blk, n, slot0):
        kv_stage(blk, n, fwd=False)

        def body(i, r_state):
            p = n - 1 - i
            rst_ref[slot0 + p, :, HEAD_DV:2 * HEAD_DV] = r_state.astype(BF16)
            r_state = dec_ref[2 * p + 1] * r_state + kv_ref[2 * p + 1]
            rst_ref[slot0 + p, :, 0:HEAD_DV] = r_state.astype(BF16)
            return dec_ref[2 * p] * r_state + kv_ref[2 * p]

        r_ref[...] = lax.fori_loop(0, n, body, r_ref[...])

    def phase1_block(blk, n, slot0, out_ref):
        q_r, kT_r, v_r, g_r, lfT_r, lbT_r = blk
        kv_stage(blk, n, fwd=True)

        def rec(p, s_state):
            sst_ref[p, :, 0:HEAD_DV] = s_state.astype(BF16)
            s_state = dec_ref[2 * p] * s_state + kv_ref[2 * p]
            sst_ref[p, :, HEAD_DV:2 * HEAD_DV] = s_state.astype(BF16)
            return dec_ref[2 * p + 1] * s_state + kv_ref[2 * p + 1]

        s_ref[...] = lax.fori_loop(0, n, rec, s_ref[...])

        r2 = lax.broadcasted_iota(jnp.int32, (2 * SP, SP), 0) % SP
        c2 = lax.broadcasted_iota(jnp.int32, (2 * SP, SP), 1)
        same2 = (r2 // CH) == (c2 // CH)
        mask_f = same2 & (r2 >= c2)
        mask_b = same2 & (r2 < c2)

        def prep(p, carry):
            rows = pl.ds(pl.multiple_of(p * SP, SP), SP)
            q = q_r[rows, :].astype(F32)
            kT = kT_r[p].astype(F32)
            if gla:
                lfT = lfT_r[p]
                lbT = lbT_r[p]
                ebTi = jnp.exp(-_dot(lfT, w_cum_f))
                erTi = jnp.exp(-_dot(lbT, w_cum_b))
                e_b = jnp.exp(_dot_nt(w_cum_b, lfT))
                e_r = jnp.exp(_dot_nt(w_cum_f, lbT))
            else:
                ebTi, erTi, e_b, e_r = ret_ebTi, ret_erTi, ret_eb, ret_er
            kd_ref[p, 0:SP, 0:SP] = (kT * ebTi).astype(BF16)
            kd_ref[p, SP:2 * SP, SP:2 * SP] = (kT * erTi).astype(BF16)
            qf = q * e_b
            qb = q * e_r
            lhs_ref[p, :, 0:SP] = jnp.concatenate(
                [jnp.where(head_a, qf, 0.0), jnp.where(head_a, 0.0, qf)], axis=0).astype(BF16)
            lhs_ref[p, :, SP:2 * SP] = jnp.concatenate(
                [jnp.where(head_a, qb, 0.0), jnp.where(head_a, 0.0, qb)], axis=0).astype(BF16)
            return carry

        def score(p, carry):
            sc = _dot(lhs_ref[p], kd_ref[p])
            pm_ref[p] = jnp.where(mask_f, sc[:, 0:SP], jnp.where(mask_b, sc[:, SP:2 * SP], 0.0)).astype(BF16)
            return carry

        def emit(p, carry):
            rows = pl.ds(pl.multiple_of(p * SP, SP), SP)
            v = v_r[rows, :]
            states = jnp.concatenate([sst_ref[p], rst_ref[slot0 + p]], axis=0)
            o_int = _dot(lhs_ref[p], states)
            o_a = _dot(pm_ref[p, 0:SP, :], v[:, 0:HEAD_DV]) + jnp.concatenate(
                [o_int[0:CH, 0:HEAD_DV], o_int[CH:SP, HEAD_DV:2 * HEAD_DV]], axis=0)
            o_b = _dot(pm_ref[p, SP:2 * SP, :], v[:, HEAD_DV:2 * HEAD_DV]) + jnp.concatenate(
                [o_int[SP:SP + CH, 0:HEAD_DV], o_int[SP + CH:2 * SP, HEAD_DV:2 * HEAD_DV]], axis=0)

            def nrm(o):
                y = o * lax.rsqrt(jnp.mean(o * o, axis=-1, keepdims=True) + RMS_EPS)
                return y * ng_ref[...] if gla else y

            y = jnp.concatenate([nrm(o_a), nrm(o_b)], axis=1) * g_r[rows, :].astype(F32)
            out_ref[rows, :] = y.astype(BF16)
            return carry

        lax.fori_loop(0, n, prep, 0, unroll=min(n, OUT_UNROLL))
        lax.fori_loop(0, n, score, 0, unroll=min(n, OUT_UNROLL))
        lax.fori_loop(0, n, emit, 0, unroll=min(n, OUT_UNROLL))

    @pl.when((pl.program_id(0) == 0) & (pl.program_id(1) == 0) & (phase == 0) & (j == 0))
    def _():
        kd_ref[...] = jnp.zeros_like(kd_ref)

    @pl.when(phase == 0)
    def _():
        @pl.when(j == 0)
        def _():
            r_ref[...] = jnp.zeros_like(r_ref)
            phase0_block(cxt, NP_CTX, 0)

        phase0_block(lat, NP, NP_CTX + (NBLK - 1 - j) * NP)

    @pl.when(phase == 1)
    def _():
        @pl.when(j == 0)
        def _():
            s_ref[...] = jnp.zeros_like(s_ref)
            phase1_block(cxt, NP_CTX, 0, yc_ref)

        phase1_block(lat, NP, NP_CTX + j * NP, y_ref)


SCAN_STEPS = B * 2 * 2 * NBLK


def _scan_group(q, kT, v, g, extra, wsrc, *, gla):
    w_spec = pl.BlockSpec((None,) + wsrc.shape[1:],
                          lambda b, p, ph, j: (((b * 2 + p) * 2 + ph) * NBLK + j, 0, 0))

    def blk(b, ph, j, used_in_phase0):
        jj = jnp.where(ph == 0, NBLK - 1 - j, j)
        if not used_in_phase0:
            jj = jnp.where(ph == 0, 0, jj)
        return b * NBLK + jj

    def lat_specs(used0):
        return dict(
            row=lambda w: pl.BlockSpec((TB, w), lambda b, p, ph, j: (blk(b, ph, j, used0), p)),
            slab=pl.BlockSpec((NP, SP, SP), lambda b, p, ph, j: (blk(b, ph, j, used0), p, 0)))

    ctx_row = lambda w: pl.BlockSpec((L, w), lambda b, p, ph, j: (LAT_ROWS // L + b, p))
    ctx_slab = pl.BlockSpec((NP_CTX, SP, SP), lambda b, p, ph, j: (LAT_ROWS // L + b, p, 0))
    used, unused = lat_specs(True), lat_specs(False)

    in_specs = [unused["row"](2 * HEAD_DK), used["slab"], used["row"](2 * HEAD_DV), unused["row"](2 * HEAD_DV)]
    ctx_specs = [ctx_row(2 * HEAD_DK), ctx_slab, ctx_row(2 * HEAD_DV), ctx_row(2 * HEAD_DV)]
    if gla:
        lfT, lbT, ng = extra
        in_specs += [unused["slab"], used["slab"]]
        ctx_specs += [ctx_slab, ctx_slab]
        args = (q, kT, v, g, lfT, lbT, q, kT, v, g, lfT, lbT, ng, wsrc)
        in_specs = in_specs + ctx_specs + [pl.BlockSpec((1, HEAD_DV), lambda b, p, ph, j: (0, 0)), w_spec]
    else:
        (theta,) = extra
        args = (q, kT, v, g, q, kT, v, g, theta, wsrc)
        in_specs = in_specs + ctx_specs + [pl.BlockSpec(memory_space=pltpu.SMEM), w_spec]
    return pl.pallas_call(
        functools.partial(_scan_kernel, gla=gla),
        grid=(B, 2, 2, NBLK),
        in_specs=in_specs,
        out_specs=[
            pl.BlockSpec((TB, 2 * HEAD_DV), lambda b, p, ph, j: (b * NBLK + jnp.where(ph == 0, 0, j), p)),
            pl.BlockSpec((L, 2 * HEAD_DV), lambda b, p, ph, j: (b, p)),
            w_spec,
        ],
        out_shape=[jax.ShapeDtypeStruct((LAT_ROWS, V_W), BF16), jax.ShapeDtypeStruct((B * L, V_W), BF16),
                   jax.ShapeDtypeStruct(wsrc.shape, BF16)],
        scratch_shapes=[
            pltpu.VMEM((SP, HEAD_DV), F32),
            pltpu.VMEM((SP, HEAD_DV), F32),
            pltpu.VMEM((NP_ALL, SP, 2 * HEAD_DV), BF16),
            pltpu.VMEM((NP, SP, 2 * HEAD_DV), BF16),
            pltpu.VMEM((2 * NP, SP, HEAD_DV), F32),
            pltpu.VMEM((2 * NP, SP, HEAD_DV), F32),
            pltpu.VMEM((NP, 2 * SP, 2 * SP), BF16),
            pltpu.VMEM((NP, 2 * SP, 2 * SP), BF16),
            pltpu.VMEM((NP, 2 * SP, SP), BF16),
        ],
        compiler_params=pltpu.CompilerParams(
            dimension_semantics=("arbitrary",) * 4, vmem_limit_bytes=VMEM_LIMIT),
        name="scan_gla" if gla else "scan_ret",
    )(*args)


FF_CH = 512


def _out_kernel(*refs, layer, split_ctx):
    if split_ctx:
        (ya_ref, yb_ref, yac_ref, ybc_ref, x_ref, ctx_ref, mod_ref, lng_ref, lnb_ref,
         wo_ref, w1_ref, w2_ref, o_ref, wo_s) = refs
        is_ctx = pl.program_id(0) == N_LAT_TILES
        x = jnp.where(is_ctx, ctx_ref[...], x_ref[...])
        ya = jnp.where(is_ctx, yac_ref[...], ya_ref[...])
        yb = jnp.where(is_ctx, ybc_ref[...], yb_ref[...])
    else:
        ya_ref, yb_ref, x_ref, mod_ref, lng_ref, lnb_ref, wo_ref, w1_ref, w2_ref, o_ref, wo_s = refs
        x = x_ref[...]
        ya = ya_ref[...]
        yb = yb_ref[...]
    g1 = mod_ref[:, 2 * D:3 * D]
    sh2 = mod_ref[:, 3 * D:4 * D]
    sc2 = mod_ref[:, 4 * D:5 * D]
    g2 = mod_ref[:, 5 * D:6 * D]
    ln_g0 = lng_ref[2 * layer:2 * layer + 1, :]
    ln_g1 = lng_ref[2 * layer + 1:2 * layer + 2, :]
    ln_b0 = lnb_ref[2 * layer:2 * layer + 1, :]
    ln_b1 = lnb_ref[2 * layer + 1:2 * layer + 2, :]

    @pl.when(pl.program_id(0) == 0)
    def _():
        wo_s[...] = wo_ref[...].astype(BF16)

    half = D // 2
    y = _dot(ya, wo_s[0:half, :]) + _dot(yb, wo_s[half:D, :])
    x1 = _layer_norm(ALPHA * x + g1 * y, ln_g0, ln_b0)
    h2 = (x1 * (1.0 + sc2) + sh2).astype(BF16)
    acc = jnp.zeros((TM, D), F32)
    for c in range(D_FF // FF_CH):
        cols = slice(c * FF_CH, (c + 1) * FF_CH)
        hc = jnp.maximum(_dot(h2, w1_ref[:, cols]), 0.0)
        acc = acc + _dot((hc * hc).astype(BF16), w2_ref[cols, :])
    o_ref[...] = _layer_norm(ALPHA * x1 + g2 * acc, ln_g1, ln_b1)


def _out_mlp(ys, xs, ctx2, mods, ln, wo, w1, w2, *, layer, n_tiles, split_ctx):
    half = D // 2
    lat_row = lambda t: (jnp.minimum(t, N_LAT_TILES - 1), 0)
    if split_ctx:
        ya, yb, yac, ybc = ys
        in_specs = [pl.BlockSpec((TM, half), lat_row), pl.BlockSpec((TM, half), lat_row),
                    _full_spec((B * L, half)), _full_spec((B * L, half)),
                    pl.BlockSpec((TM, D), lat_row), _full_spec((B * L, D))]
        args = [ya, yb, yac, ybc, xs, ctx2]
    else:
        (att,) = ys
        in_specs = [pl.BlockSpec((TM, half), lambda t: (t, 0)), pl.BlockSpec((TM, half), lambda t: (t, 1)),
                    pl.BlockSpec((TM, D), lambda t: (t, 0))]
        args = [att, att, xs]
    stacked = lambda w, i: pl.BlockSpec((None,) + w.shape[1:], lambda t: (i, 0, 0), pipeline_mode=pl.Buffered(1))
    in_specs += [_mod_spec(layer), _full_spec(ln[0].shape), _full_spec(ln[1].shape),
                 stacked(wo, 0), stacked(w1, layer), stacked(w2, layer)]
    args += [mods, ln[0], ln[1], wo, w1, w2]
    return pl.pallas_call(
        functools.partial(_out_kernel, layer=layer, split_ctx=split_ctx),
        grid=(n_tiles,),
        in_specs=in_specs,
        out_specs=pl.BlockSpec((TM, D), lambda t: (t, 0)),
        out_shape=jax.ShapeDtypeStruct((n_tiles * TM, D), F32),
        compiler_params=pltpu.CompilerParams(
            dimension_semantics=("arbitrary",), vmem_limit_bytes=VMEM_LIMIT),
        scratch_shapes=[pltpu.VMEM((D, D), BF16)],
        name="out_mlp%d" % layer,
    )(*args)


def _in1_kernel(x_ref, mod_ref, cos_ref, sin_ref, w_ref, qT_o, k_o, vT_o, wqT_s, wk_s, wvT_s):
    @pl.when(pl.program_id(0) == 0)
    def _():
        wqT_s[...] = _transpose_bf16(w_ref[:, 0:D])
        wk_s[...] = w_ref[:, D:D + ATT_KVW].astype(BF16)
        wvT_s[...] = _transpose_bf16(w_ref[:, D + ATT_KVW:D + 2 * ATT_KVW])

    sh1 = mod_ref[:, 0:D]
    sc1 = mod_ref[:, D:2 * D]
    h = (x_ref[...] * (1.0 + sc1) + sh1).astype(BF16)
    nsl = TM // SP

    cos = cos_ref[...]
    sin = sin_ref[...]
    first = (lax.broadcasted_iota(jnp.int32, (TM, 128), 1) % 32) < 16
    sa = jnp.where(first, -sin, 0.0)
    sb = jnp.where(first, 0.0, sin)

    k = _dot(h, wk_s[...])
    for i in range(ATT_KVW // 128):
        u = k[:, i * 128:(i + 1) * 128]
        r = u * cos + pltpu.roll(u, 128 - 16, 1) * sa + pltpu.roll(u, 16, 1) * sb
        k_o[:, i * 128:(i + 1) * 128] = r.astype(BF16)

    cosT, saT, sbT = cos.T, sa.T, sb.T
    qT = _dot_nt(wqT_s[...], h) * (LOG2E * ATT_DH ** -0.5)
    for i in range(D // 128):
        u = qT[i * 128:(i + 1) * 128, :]
        r = (u * cosT + pltpu.roll(u, 128 - 16, 0) * saT + pltpu.roll(u, 16, 0) * sbT).astype(BF16)
        for s in range(nsl):
            qT_o[s, i * 128:(i + 1) * 128, :] = r[:, s * SP:(s + 1) * SP]

    vT = _dot_nt(wvT_s[...], h).astype(BF16)
    for s in range(nsl):
        vT_o[s] = vT[:, s * SP:(s + 1) * SP]


def _in_proj1(xs, mods, cos_tab, sin_tab, wqkv):
    tile = lambda t: jnp.where(t == N_LAT_TILES, TILES_PER_BATCH, t % TILES_PER_BATCH)
    row = lambda t: (t, 0)
    slab = lambda t: (t, 0, 0)
    nsl = TM // SP
    tab_spec = pl.BlockSpec((TM, 128), lambda t: (tile(t), 0))
    return pl.pallas_call(
        _in1_kernel,
        grid=(N_TILES,),
        in_specs=[pl.BlockSpec((TM, D), row), _mod_spec(1), tab_spec, tab_spec,
                  pl.BlockSpec((None,) + wqkv.shape[1:], lambda t: (0, 0, 0), pipeline_mode=pl.Buffered(1))],
        out_specs=[pl.BlockSpec((nsl, D, SP), slab), pl.BlockSpec((TM, ATT_KVW), row),
                   pl.BlockSpec((nsl, ATT_KVW, SP), slab)],
        out_shape=[jax.ShapeDtypeStruct((N_SLABS, D, SP), BF16), jax.ShapeDtypeStruct((ROWS, ATT_KVW), BF16),
                   jax.ShapeDtypeStruct((N_SLABS, ATT_KVW, SP), BF16)],
        scratch_shapes=[pltpu.VMEM((D, D), BF16), pltpu.VMEM((D, ATT_KVW), BF16), pltpu.VMEM((ATT_KVW, D), BF16)],
        compiler_params=pltpu.CompilerParams(
            dimension_semantics=("arbitrary",), vmem_limit_bytes=VMEM_LIMIT),
        name="in_proj1",
    )(xs, mods, cos_tab, sin_tab, wqkv)


def _attn_kernel(sink_ref, qT_ref, k_ref, vT_ref, kc_ref, vcT_ref, o_ref, bias_ref):
    n = pl.program_id(1)
    nql = ATT_GROUP * QB
    lane_g = lax.broadcasted_iota(jnp.int32, (1, nql), 1) // QB
    zero_half = jnp.zeros((ATT_DH, nql), BF16)
    ones_rows = jnp.ones((16, KWIN + L), BF16)

    @pl.when((pl.program_id(0) == 0) & (n == 0))
    def _():
        kj = lax.broadcasted_iota(jnp.int32, (KWIN, QB), 0)
        qi = lax.broadcasted_iota(jnp.int32, (KWIN, QB), 1)
        for case, delta in enumerate((-QB, 0, -2 * QB)):
            d = kj - qi + delta
            bias_ref[case] = jnp.where((d >= -WINDOW) & (d <= WINDOW), 0.0, -jnp.inf)

    def scores(i, kh):
        n0 = (n * (TQ // QB) + i) * QB
        start = pl.multiple_of(jnp.clip(n0 - QB, 0, T - KWIN), QB)
        bias1 = bias_ref[jnp.where(n0 == 0, 1, jnp.where(n0 == T - QB, 2, 0))]
        bias = jnp.concatenate([bias1] * ATT_GROUP, axis=1)
        pair = slice((kh // 2) * 128, (kh // 2 + 1) * 128)
        qT = jnp.concatenate(
            [qT_ref[i, (kh * ATT_GROUP + g) * ATT_DH:(kh * ATT_GROUP + g + 1) * ATT_DH, :]
             for g in range(ATT_GROUP)], axis=1)
        q_pad = jnp.concatenate([zero_half, qT] if kh % 2 else [qT, zero_half], axis=0)
        s_loc = _dot(k_ref[pl.ds(start, KWIN), pair], q_pad) + bias
        s_ctx = _dot(kc_ref[:, pair], q_pad)
        return s_loc, s_ctx, start // SP

    def finish(i, kh, s_loc, s_ctx, slab0):
        drows = slice(kh * ATT_DH, (kh + 1) * ATT_DH)
        sink = jnp.zeros((1, nql), F32)
        for g in range(ATT_GROUP):
            sink = jnp.where(lane_g == g, sink_ref[0, kh * ATT_GROUP + g] * LOG2E, sink)
        m = jnp.maximum(jnp.maximum(jnp.max(s_loc, axis=0, keepdims=True),
                                    jnp.max(s_ctx, axis=0, keepdims=True)), sink)
        pT = jnp.concatenate([jnp.exp2(s_loc - m).astype(BF16),
                              jnp.exp2(s_ctx - m).astype(BF16)], axis=0)
        vT = jnp.concatenate([vT_ref[slab0 + t, drows, :] for t in range(KWIN // SP)]
                             + [vcT_ref[t, drows, :] for t in range(L // SP)], axis=1)
        o_ext = _dot(jnp.concatenate([vT, ones_rows], axis=0), pT)
        denom = o_ext[ATT_DH:ATT_DH + 1, :] + jnp.exp2(sink - m)
        oT = o_ext[0:ATT_DH, :] / denom
        for g in range(0, ATT_GROUP, 2):
            two = jnp.concatenate([oT[:, g * QB:(g + 1) * QB], oT[:, (g + 1) * QB:(g + 2) * QB]], axis=0)
            c0 = (kh * ATT_GROUP + g) * ATT_DH
            o_ref[i * QB:(i + 1) * QB, c0:c0 + 2 * ATT_DH] = two.T.astype(BF16)

    items = [(i, kh) for i in range(TQ // QB) for kh in range(ATT_KVH)]
    pending = scores(*items[0])
    for t, item in enumerate(items):
        nxt = scores(*items[t + 1]) if t + 1 < len(items) else None
        finish(*item, *pending)
        pending = nxt


def _attention(sink, qT, k, vT):
    nq = T // TQ
    return pl.pallas_call(
        _attn_kernel,
        grid=(B, nq),
        in_specs=[
            pl.BlockSpec(memory_space=pltpu.SMEM),
            pl.BlockSpec((TQ // SP, D, SP), lambda b, n: (b * nq + n, 0, 0)),
            pl.BlockSpec((T, ATT_KVW), lambda b, n: (b, 0)),
            pl.BlockSpec((T // SP, ATT_KVW, SP), lambda b, n: (b, 0, 0)),
            pl.BlockSpec((L, ATT_KVW), lambda b, n: (LAT_ROWS // L + b, 0)),
            pl.BlockSpec((L // SP, ATT_KVW, SP), lambda b, n: (LAT_ROWS // L + b, 0, 0)),
        ],
        out_specs=pl.BlockSpec((TQ, D), lambda b, n: (b * nq + n, 0)),
        out_shape=jax.ShapeDtypeStruct((LAT_ROWS, D), BF16),
        compiler_params=pltpu.CompilerParams(
            dimension_semantics=("arbitrary", "arbitrary"), vmem_limit_bytes=VMEM_LIMIT),
        scratch_shapes=[pltpu.VMEM((3, KWIN, QB), F32)],
        name="window_attn",
    )(sink, qT, k, vT, k, vT)


def _rope_tables():
    half = ATT_DH // 2
    inv_freq = np.power(np.float32(ROPE_BASE), -np.arange(0, half, 2, dtype=np.float32) / np.float32(half))
    inv_freq = inv_freq.astype(np.float32)
    pos = np.arange(T)
    ang_r = (pos // GRID_W).astype(np.float32)[:, None] * inv_freq[None, :]
    ang_c = (pos % GRID_W).astype(np.float32)[:, None] * inv_freq[None, :]
    ang = np.concatenate([ang_r, ang_r, ang_c, ang_c], axis=1)
    ang = np.concatenate([ang, np.zeros((TM, ATT_DH), np.float32)], axis=0)
    ang = np.concatenate([ang, ang], axis=1)
    return np.cos(ang).astype(np.float32), np.sin(ang).astype(np.float32)


def kernel(x, c, ctx, c_ctx, w_mod, b_mod, ln_g, ln_b, mlp_w1, mlp_w2, ev_w_in, ev_ret_theta, ev_gla_gk_w,
           ev_gla_gk_b, ev_gla_norm_g, ev_w_out, od_w_qkv, od_sink, od_w_out):
    x2 = x.reshape(LAT_ROWS, D)
    ctx2 = ctx.reshape(B * L, D)

    cs = jnp.concatenate([c, c_ctx[None, :], jnp.zeros((8 - B - 1, D), F32)], axis=0)
    mods = _modulation(cs, w_mod, b_mod).reshape(DEPTH * 8, 1, 6 * D)
    ln = (ln_g.reshape(2 * DEPTH, D), ln_b.reshape(2 * DEPTH, D))

    gk_w = ev_gla_gk_w[0]
    zeros = jnp.zeros((GATE_RANK, QK_W), F32)
    g2 = jnp.concatenate([jnp.concatenate([gk_w[0], zeros], axis=1),
                          jnp.concatenate([zeros, gk_w[1]], axis=1)], axis=0)
    qa, kaT, va, ga, qb, kbT, vb, gbv, lfT, lbT = _in_proj0(
        x2, ctx2, mods, ev_w_in, g2.T.astype(BF16), ev_gla_gk_b[0].reshape(2 * QK_W, 1))

    w1_rows = DEPTH * D // SCAN_STEPS
    w2_rows = DEPTH * D_FF // SCAN_STEPS
    y_ret, yc_ret, w1 = _scan_group(qa, kaT, va, ga, (ev_ret_theta,),
                                    mlp_w1.reshape(SCAN_STEPS, w1_rows, D_FF), gla=False)
    y_gla, yc_gla, w2 = _scan_group(qb, kbT, vb, gbv, (lfT, lbT, ev_gla_norm_g),
                                    mlp_w2.reshape(SCAN_STEPS, w2_rows, D), gla=True)
    w1 = w1.reshape(DEPTH, D, D_FF)
    w2 = w2.reshape(DEPTH, D_FF, D)

    xs = _out_mlp((y_ret, y_gla, yc_ret, yc_gla), x2, ctx2, mods, ln, ev_w_out, w1, w2,
                  layer=0, n_tiles=N_TILES, split_ctx=True)

    cos_tab, sin_tab = _rope_tables()
    q1T, k1, v1T = _in_proj1(xs, mods, cos_tab, sin_tab, od_w_qkv)
    att = _attention(od_sink, q1T, k1, v1T)
    out = _out_mlp((att,), xs, None, mods, ln, od_w_out, w1, w2,
                   layer=1, n_tiles=N_LAT_TILES, split_ctx=False)
    return out.reshape(B, T, D)
```

```python
import functools

import jax
import jax.numpy as jnp
import numpy as np
from jax import lax
from jax.experimental import pallas as pl
from jax.experimental.pallas import tpu as pltpu

F32 = jnp.float32
BF16 = jnp.bfloat16

D = 1024
B = 2
T = 8192
L = 256
DEPTH = 2
GRID_W = 64
D_FF = 4 * D
HEAD_DV = 128
HEAD_DK = 64
GATE_RANK = 16
GATE_TAU = 16.0
QK_W = 256
V_W = 512
ATT_DH = 64
ATT_QH = 16
ATT_KVH = 4
ATT_GROUP = 4
ATT_KVW = ATT_KVH * ATT_DH
WINDOW = 128
ROPE_BASE = 10000.0
ALPHA = (2.0 * DEPTH) ** 0.25
LN_EPS = 1e-5
RMS_EPS = 1e-6

TM = 512
LAT_ROWS = B * T
ROWS = LAT_ROWS + B * L
N_LAT_TILES = LAT_ROWS // TM
N_TILES = ROWS // TM
TILES_PER_BATCH = T // TM

CH = 64
SP = 2 * CH
TB = 1024
NP = TB // SP
NP_CTX = L // SP
NBLK = T // TB
NP_ALL = NP_CTX + T // SP
N_SLABS = ROWS // SP
KV_UNROLL = 8
OUT_UNROLL = 8

TQ = 512
QB = 128
KWIN = 3 * QB
LOG2E = 1.4426950408889634

VMEM_LIMIT = 56 * 1024 * 1024


def _dot(a, b):
    return jnp.dot(a, b, preferred_element_type=F32)


def _dot_nt(a, b):
    return lax.dot_general(a, b, (((1,), (1,)), ((), ())), preferred_element_type=F32)


def _full_spec(shape):
    nd = len(shape)
    return pl.BlockSpec(shape, lambda *_: (0,) * nd, pipeline_mode=pl.Buffered(1))


def _mod_row(t):
    return jnp.minimum(t // TILES_PER_BATCH, B)


def _mod_spec(layer):
    return pl.BlockSpec((None, 1, 6 * D), lambda t: (layer * 8 + _mod_row(t), 0, 0))


def _layer_norm(x, g, b):
    mu = jnp.mean(x, axis=-1, keepdims=True)
    xc = x - mu
    var = jnp.mean(xc * xc, axis=-1, keepdims=True)
    return xc * lax.rsqrt(var + LN_EPS) * g + b


def _log_sigmoid(z):
    return jnp.minimum(z, 0.0) - jnp.log1p(jnp.exp(-jnp.abs(z)))


MOD_TN = 1536


def _mod_kernel(c_ref, w_ref, b_ref, o_ref):
    s = jax.nn.silu(c_ref[...])
    s_hi = s.astype(BF16)
    s_lo = (s - s_hi.astype(F32)).astype(BF16)
    w = w_ref[...].astype(BF16)
    o_ref[...] = _dot(s_hi, w) + _dot(s_lo, w) + b_ref[...]


def _modulation(cs, w_mod, b_mod):
    return pl.pallas_call(
        _mod_kernel,
        grid=(DEPTH, 6 * D // MOD_TN),
        in_specs=[
            pl.BlockSpec((8, D), lambda i, n: (0, 0)),
            pl.BlockSpec((None, D, MOD_TN), lambda i, n: (i, 0, n)),
            pl.BlockSpec((None, 1, MOD_TN), lambda i, n: (i, 0, n)),
        ],
        out_specs=pl.BlockSpec((None, 8, MOD_TN), lambda i, n: (i, 0, n)),
        out_shape=jax.ShapeDtypeStruct((DEPTH, 8, 6 * D), F32),
        compiler_params=pltpu.CompilerParams(
            dimension_semantics=("arbitrary", "arbitrary"), vmem_limit_bytes=VMEM_LIMIT),
        name="modulation",
    )(cs, w_mod, b_mod.reshape(DEPTH, 1, 6 * D))


IN0_OFF = (0, 256, 512, 1024, 1536, 1792, 2048, 2560, 3072, 3104)


def _transpose_bf16(w):
    return w.astype(F32).T.astype(BF16)


def _in0_kernel(x_ref, ctx_ref, mod_ref, wT_ref, g2T, gbc,
                qa_o, kaT_o, va_o, ga_o, qb_o, kbT_o, vb_o, gb_o, lfT_o, lbT_o, wbT_s):
    t = pl.program_id(0)
    rows = lambda i: slice(IN0_OFF[i], IN0_OFF[i + 1])
    piece = lambda i: wbT_s[rows(i), :]

    @pl.when(t == 0)
    def _():
        for i in range(len(IN0_OFF) - 1):
            wbT_s[rows(i), :] = wT_ref[rows(i), :].astype(BF16)

    xt = jnp.where(t == N_LAT_TILES, ctx_ref[...], x_ref[...])
    sh1 = mod_ref[:, 0:D]
    sc1 = mod_ref[:, D:2 * D]
    h = (xt * (1.0 + sc1) + sh1).astype(BF16)
    qk_scale = HEAD_DK ** -0.5

    def put_slabs(o_ref, val):
        for i in range(TM // SP):
            o_ref[i] = val[:, i * SP:(i + 1) * SP].astype(o_ref.dtype)

    qa_o[...] = _dot_nt(h, piece(0)).astype(BF16)
    put_slabs(kaT_o, _dot_nt(piece(1), h) * qk_scale)
    va_o[...] = _dot_nt(h, piece(2)).astype(BF16)
    ga_o[...] = jax.nn.silu(_dot_nt(h, piece(3))).astype(BF16)

    qb_o[...] = (_dot_nt(h, piece(4)) * qk_scale).astype(BF16)
    put_slabs(kbT_o, _dot_nt(piece(5), h))
    vb_o[...] = _dot_nt(h, piece(6)).astype(BF16)
    gb_o[...] = jax.nn.silu(_dot_nt(h, piece(7))).astype(BF16)

    lr = _dot_nt(h, piece(8)).astype(BF16)
    lsT = _log_sigmoid(_dot_nt(g2T[...], lr) + gbc[...]) * (1.0 / GATE_TAU)
    put_slabs(lfT_o, lsT[0:QK_W, :])
    put_slabs(lbT_o, lsT[QK_W:2 * QK_W, :])


def _in_proj0(x2, ctx2, mods, w_inT, g2T, gbc):
    row = lambda t: (t, 0)
    slab = lambda t: (t, 0, 0)
    nsl = TM // SP
    row_out = lambda width: (jax.ShapeDtypeStruct((ROWS, width), BF16), pl.BlockSpec((TM, width), row))
    slab_out = lambda dt: (jax.ShapeDtypeStruct((N_SLABS, QK_W, SP), dt), pl.BlockSpec((nsl, QK_W, SP), slab))
    outs = [row_out(QK_W), slab_out(BF16), row_out(V_W), row_out(V_W),
            row_out(QK_W), slab_out(BF16), row_out(V_W), row_out(V_W),
            slab_out(BF16), slab_out(BF16)]
    in_specs = [
        pl.BlockSpec((TM, D), lambda t: (jnp.minimum(t, N_LAT_TILES - 1), 0)),
        _full_spec((B * L, D)),
        _mod_spec(0),
        _full_spec(w_inT.shape), _full_spec(g2T.shape), _full_spec(gbc.shape),
    ]
    return pl.pallas_call(
        _in0_kernel,
        grid=(N_TILES,),
        in_specs=in_specs,
        out_specs=[o[1] for o in outs],
        out_shape=[o[0] for o in outs],
        scratch_shapes=[pltpu.VMEM(w_inT.shape, BF16)],
        compiler_params=pltpu.CompilerParams(
            dimension_semantics=("arbitrary",), vmem_limit_bytes=VMEM_LIMIT),
        name="in_proj0",
    )(x2, ctx2, mods, w_inT, g2T, gbc)


def _chunk_diag(kv, c):
    r0 = c * 2 * HEAD_DK
    return jnp.concatenate([kv[r0:r0 + HEAD_DK, 0:HEAD_DV],
                            kv[r0 + HEAD_DK:r0 + 2 * HEAD_DK, HEAD_DV:2 * HEAD_DV]], axis=0)


def _scan_kernel(*refs, gla):
    if gla:
        (q_ref, kT_ref, v_ref, g_ref, lfT_ref, lbT_ref,
         qc_ref, kTc_ref, vc_ref, gc_ref, lfTc_ref, lbTc_ref, ng_ref, wsrc_ref,
         y_ref, yc_ref, wdst_ref, s_ref, r_ref, rst_ref, sst_ref, kv_ref, dec_ref,
         kd_ref, lhs_ref, pm_ref) = refs
        lat = (q_ref, kT_ref, v_ref, g_ref, lfT_ref, lbT_ref)
        cxt = (qc_ref, kTc_ref, vc_ref, gc_ref, lfTc_ref, lbTc_ref)
    else:
        (q_ref, kT_ref, v_ref, g_ref, qc_ref, kTc_ref, vc_ref, gc_ref, th_ref, wsrc_ref,
         y_ref, yc_ref, wdst_ref, s_ref, r_ref, rst_ref, sst_ref, kv_ref, dec_ref,
         kd_ref, lhs_ref, pm_ref) = refs
        lat = (q_ref, kT_ref, v_ref, g_ref, None, None)
        cxt = (qc_ref, kTc_ref, vc_ref, gc_ref, None, None)

    phase = pl.program_id(2)
    j = pl.program_id(3)

    wdst_ref[...] = wsrc_ref[...].astype(BF16)

    ri = lax.broadcasted_iota(jnp.int32, (SP, SP), 0)
    ci = lax.broadcasted_iota(jnp.int32, (SP, SP), 1)
    same = (ri // CH) == (ci // CH)
    first_lane = ci < CH
    head_a = ci < HEAD_DK

    if gla:
        as_w = lambda m: m.astype(BF16)
        tot = jnp.concatenate([jnp.broadcast_to(ri < CH, (SP, SP)), jnp.broadcast_to(ri >= CH, (SP, SP))], axis=1)
        w_end_f = jnp.concatenate([as_w(same & (ri > ci)), as_w(tot)], axis=1)
        w_end_b = jnp.concatenate([as_w(same & (ri < ci)), as_w(tot)], axis=1)
        w_cum_f = as_w(same & (ri <= ci))
        w_cum_b = as_w(same & (ri >= ci))
    else:
        hp = pl.program_id(1)
        th = [[th_ref[0, dr, 2 * hp + hd] for hd in range(2)] for dr in range(2)]
        lane1 = lax.broadcasted_iota(jnp.int32, (1, SP), 1) < HEAD_DK
        row1 = lax.broadcasted_iota(jnp.int32, (SP, 1), 0) < HEAD_DK
        lg_row = [jnp.log1p(-jnp.exp(jnp.where(lane1, th[dr][0], th[dr][1]))) for dr in range(2)]
        lg_col = [jnp.log1p(-jnp.exp(jnp.where(row1, th[dr][0], th[dr][1]))) for dr in range(2)]
        it = (ci % CH).astype(F32)
        ir = (ri % CH).astype(F32)
        ret_end_f = jnp.exp((CH - 1.0 - it) * lg_col[0])
        ret_end_b = jnp.exp(it * lg_col[1])
        ret_dec_f = jnp.exp(jnp.broadcast_to(CH * lg_col[0], (SP, SP)))
        ret_dec_b = jnp.exp(jnp.broadcast_to(CH * lg_col[1], (SP, SP)))
        ret_ebTi = jnp.exp(-(it + 1.0) * lg_col[0])
        ret_erTi = jnp.exp(-(CH - it) * lg_col[1])
        ret_eb = jnp.exp((ir + 1.0) * lg_row[0])
        ret_er = jnp.exp((CH - ir) * lg_row[1])

    def kv_stage(blk, n, fwd):
        _, kT_r, v_r, _, lfT_r, lbT_r = blk

        def body(p, carry):
            kT = kT_r[p].astype(F32)
            v = v_r[pl.ds(pl.multiple_of(p * SP, SP), SP), :]
            if gla:
                res = _dot((lfT_r if fwd else lbT_r)[p], w_end_f if fwd else w_end_b)
                e_end = jnp.exp(res[:, 0:SP])
                dec0 = jnp.exp(res[:, SP:2 * SP])
                dec1 = jnp.exp(res[:, 2 * SP:3 * SP])
            else:
                e_end = ret_end_f if fwd else ret_end_b
                dec0 = dec1 = ret_dec_f if fwd else ret_dec_b
            ke = kT * e_end
            lhs = jnp.concatenate([jnp.where(first_lane, ke, 0.0), jnp.where(first_lane, 0.0, ke)],
                                  axis=0).astype(BF16)
            kv = _dot(lhs, v)
            kv_ref[2 * p] = _chunk_diag(kv, 0)
            kv_ref[2 * p + 1] = _chunk_diag(kv, 1)
            dec_ref[2 * p] = dec0
            dec_ref[2 * p + 1] = dec1
            return carry

        lax.fori_loop(0, n, body, 0, unroll=min(n, KV_UNROLL))

    def phase0_block(blk, n, slot0):
        kv_stage(blk, n, fwd=False)

        def body(i, r_state):
            p = n - 1 - i
            rst_ref[slot0 + p, :, HEAD_DV:2 * HEAD_DV] = r_state.astype(BF16)
            r_state = dec_ref[2 * p + 1] * r_state + kv_ref[2 * p + 1]
            rst_ref[slot0 + p, :, 0:HEAD_DV] = r_state.astype(BF16)
            return dec_ref[2 * p] * r_state + kv_ref[2 * p]

        r_ref[...] = lax.fori_loop(0, n, body, r_ref[...])

    def phase1_block(blk, n, slot0, out_ref):
        q_r, kT_r, v_r, g_r, lfT_r, lbT_r = blk
        kv_stage(blk, n, fwd=True)

        def rec(p, s_state):
            sst_ref[p, :, 0:HEAD_DV] = s_state.astype(BF16)
            s_state = dec_ref[2 * p] * s_state + kv_ref[2 * p]
            sst_ref[p, :, HEAD_DV:2 * HEAD_DV] = s_state.astype(BF16)
            return dec_ref[2 * p + 1] * s_state + kv_ref[2 * p + 1]

        s_ref[...] = lax.fori_loop(0, n, rec, s_ref[...])

        r2 = lax.broadcasted_iota(jnp.int32, (2 * SP, SP), 0) % SP
        c2 = lax.broadcasted_iota(jnp.int32, (2 * SP, SP), 1)
        same2 = (r2 // CH) == (c2 // CH)
        mask_f = same2 & (r2 >= c2)
        mask_b = same2 & (r2 < c2)

        def prep(p, carry):
            rows = pl.ds(pl.multiple_of(p * SP, SP), SP)
            q = q_r[rows, :].astype(F32)
            kT = kT_r[p].astype(F32)
            if gla:
                lfT = lfT_r[p]
                lbT = lbT_r[p]
                ebTi = jnp.exp(-_dot(lfT, w_cum_f))
                erTi = jnp.exp(-_dot(lbT, w_cum_b))
                e_b = jnp.exp(_dot_nt(w_cum_b, lfT))
                e_r = jnp.exp(_dot_nt(w_cum_f, lbT))
            else:
                ebTi, erTi, e_b, e_r = ret_ebTi, ret_erTi, ret_eb, ret_er
            kd_ref[p, 0:SP, 0:SP] = (kT * ebTi).astype(BF16)
            kd_ref[p, SP:2 * SP, SP:2 * SP] = (kT * erTi).astype(BF16)
            qf = q * e_b
            qb = q * e_r
            lhs_ref[p, :, 0:SP] = jnp.concatenate(
                [jnp.where(head_a, qf, 0.0), jnp.where(head_a, 0.0, qf)], axis=0).astype(BF16)
            lhs_ref[p, :, SP:2 * SP] = jnp.concatenate(
                [jnp.where(head_a, qb, 0.0), jnp.where(head_a, 0.0, qb)], axis=0).astype(BF16)
            return carry

        def score(p, carry):
            sc = _dot(lhs_ref[p], kd_ref[p])
            pm_ref[p] = jnp.where(mask_f, sc[:, 0:SP], jnp.where(mask_b, sc[:, SP:2 * SP], 0.0)).astype(BF16)
            return carry

        def emit(p, carry):
            rows = pl.ds(pl.multiple_of(p * SP, SP), SP)
            v = v_r[rows, :]
            states = jnp.concatenate([sst_ref[p], rst_ref[slot0 + p]], axis=0)
            o_int = _dot(lhs_ref[p], states)
            o_a = _dot(pm_ref[p, 0:SP, :], v[:, 0:HEAD_DV]) + jnp.concatenate(
                [o_int[0:CH, 0:HEAD_DV], o_int[CH:SP, HEAD_DV:2 * HEAD_DV]], axis=0)
            o_b = _dot(pm_ref[p, SP:2 * SP, :], v[:, HEAD_DV:2 * HEAD_DV]) + jnp.concatenate(
                [o_int[SP:SP + CH, 0:HEAD_DV], o_int[SP + CH:2 * SP, HEAD_DV:2 * HEAD_DV]], axis=0)

            def nrm(o):
                y = o * lax.rsqrt(jnp.mean(o * o, axis=-1, keepdims=True) + RMS_EPS)
                return y * ng_ref[...] if gla else y

            y = jnp.concatenate([nrm(o_a), nrm(o_b)], axis=1) * g_r[rows, :].astype(F32)
            out_ref[rows, :] = y.astype(BF16)
            return carry

        lax.fori_loop(0, n, prep, 0, unroll=min(n, OUT_UNROLL))
        lax.fori_loop(0, n, score, 0, unroll=min(n, OUT_UNROLL))
        lax.fori_loop(0, n, emit, 0, unroll=min(n, OUT_UNROLL))

    @pl.when((pl.program_id(0) == 0) & (pl.program_id(1) == 0) & (phase == 0) & (j == 0))
    def _():
        kd_ref[...] = jnp.zeros_like(kd_ref)

    @pl.when(phase == 0)
    def _():
        @pl.when(j == 0)
        def _():
            r_ref[...] = jnp.zeros_like(r_ref)
            phase0_block(cxt, NP_CTX, 0)

        phase0_block(lat, NP, NP_CTX + (NBLK - 1 - j) * NP)

    @pl.when(phase == 1)
    def _():
        @pl.when(j == 0)
        def _():
            s_ref[...] = jnp.zeros_like(s_ref)
            phase1_block(cxt, NP_CTX, 0, yc_ref)

        phase1_block(lat, NP, NP_CTX + j * NP, y_ref)


SCAN_STEPS = B * 2 * 2 * NBLK


def _scan_group(q, kT, v, g, extra, wsrc, *, gla):
    w_spec = pl.BlockSpec((None,) + wsrc.shape[1:],
                          lambda b, p, ph, j: (((b * 2 + p) * 2 + ph) * NBLK + j, 0, 0))

    def blk(b, ph, j, used_in_phase0):
        jj = jnp.where(ph == 0, NBLK - 1 - j, j)
        if not used_in_phase0:
            jj = jnp.where(ph == 0, 0, jj)
        return b * NBLK + jj

    def lat_specs(used0):
        return dict(
            row=lambda w: pl.BlockSpec((TB, w), lambda b, p, ph, j: (blk(b, ph, j, used0), p)),
            slab=pl.BlockSpec((NP, SP, SP), lambda b, p, ph, j: (blk(b, ph, j, used0), p, 0)))

    ctx_row = lambda w: pl.BlockSpec((L, w), lambda b, p, ph, j: (LAT_ROWS // L + b, p))
    ctx_slab = pl.BlockSpec((NP_CTX, SP, SP), lambda b, p, ph, j: (LAT_ROWS // L + b, p, 0))
    used, unused = lat_specs(True), lat_specs(False)

    in_specs = [unused["row"](2 * HEAD_DK), used["slab"], used["row"](2 * HEAD_DV), unused["row"](2 * HEAD_DV)]
    ctx_specs = [ctx_row(2 * HEAD_DK), ctx_slab, ctx_row(2 * HEAD_DV), ctx_row(2 * HEAD_DV)]
    if gla:
        lfT, lbT, ng = extra
        in_specs += [unused["slab"], used["slab"]]
        ctx_specs += [ctx_slab, ctx_slab]
        args = (q, kT, v, g, lfT, lbT, q, kT, v, g, lfT, lbT, ng, wsrc)
        in_specs = in_specs + ctx_specs + [pl.BlockSpec((1, HEAD_DV), lambda b, p, ph, j: (0, 0)), w_spec]
    else:
        (theta,) = extra
        args = (q, kT, v, g, q, kT, v, g, theta, wsrc)
        in_specs = in_specs + ctx_specs + [pl.BlockSpec(memory_space=pltpu.SMEM), w_spec]
    return pl.pallas_call(
        functools.partial(_scan_kernel, gla=gla),
        grid=(B, 2, 2, NBLK),
        in_specs=in_specs,
        out_specs=[
            pl.BlockSpec((TB, 2 * HEAD_DV), lambda b, p, ph, j: (b * NBLK + jnp.where(ph == 0, 0, j), p)),
            pl.BlockSpec((L, 2 * HEAD_DV), lambda b, p, ph, j: (b, p)),
            w_spec,
        ],
        out_shape=[jax.ShapeDtypeStruct((LAT_ROWS, V_W), BF16), jax.ShapeDtypeStruct((B * L, V_W), BF16),
                   jax.ShapeDtypeStruct(wsrc.shape, BF16)],
        scratch_shapes=[
            pltpu.VMEM((SP, HEAD_DV), F32),
            pltpu.VMEM((SP, HEAD_DV), F32),
            pltpu.VMEM((NP_ALL, SP, 2 * HEAD_DV), BF16),
            pltpu.VMEM((NP, SP, 2 * HEAD_DV), BF16),
            pltpu.VMEM((2 * NP, SP, HEAD_DV), F32),
            pltpu.VMEM((2 * NP, SP, HEAD_DV), F32),
            pltpu.VMEM((NP, 2 * SP, 2 * SP), BF16),
            pltpu.VMEM((NP, 2 * SP, 2 * SP), BF16),
            pltpu.VMEM((NP, 2 * SP, SP), BF16),
        ],
        compiler_params=pltpu.CompilerParams(
            dimension_semantics=("arbitrary",) * 4, vmem_limit_bytes=VMEM_LIMIT),
        name="scan_gla" if gla else "scan_ret",
    )(*args)


FF_CH = 512


def _out_kernel(*refs, layer, split_ctx):
    if split_ctx:
        (ya_ref, yb_ref, yac_ref, ybc_ref, x_ref, ctx_ref, mod_ref, lng_ref, lnb_ref,
         wo_ref, w1_ref, w2_ref, o_ref, wo_s) = refs
        is_ctx = pl.program_id(0) == N_LAT_TILES
        x = jnp.where(is_ctx, ctx_ref[...], x_ref[...])
        ya = jnp.where(is_ctx, yac_ref[...], ya_ref[...])
        yb = jnp.where(is_ctx, ybc_ref[...], yb_ref[...])
    else:
        ya_ref, yb_ref, x_ref, mod_ref, lng_ref, lnb_ref, wo_ref, w1_ref, w2_ref, o_ref, wo_s = refs
        x = x_ref[...]
        ya = ya_ref[...]
        yb = yb_ref[...]
    g1 = mod_ref[:, 2 * D:3 * D]
    sh2 = mod_ref[:, 3 * D:4 * D]
    sc2 = mod_ref[:, 4 * D:5 * D]
    g2 = mod_ref[:, 5 * D:6 * D]
    ln_g0 = lng_ref[2 * layer:2 * layer + 1, :]
    ln_g1 = lng_ref[2 * layer + 1:2 * layer + 2, :]
    ln_b0 = lnb_ref[2 * layer:2 * layer + 1, :]
    ln_b1 = lnb_ref[2 * layer + 1:2 * layer + 2, :]

    @pl.when(pl.program_id(0) == 0)
    def _():
        wo_s[...] = wo_ref[...].astype(BF16)

    half = D // 2
    y = _dot(ya, wo_s[0:half, :]) + _dot(yb, wo_s[half:D, :])
    x1 = _layer_norm(ALPHA * x + g1 * y, ln_g0, ln_b0)
    h2 = (x1 * (1.0 + sc2) + sh2).astype(BF16)
    acc = jnp.zeros((TM, D), F32)
    for c in range(D_FF // FF_CH):
        cols = slice(c * FF_CH, (c + 1) * FF_CH)
        hc = jnp.maximum(_dot(h2, w1_ref[:, cols]), 0.0)
        acc = acc + _dot((hc * hc).astype(BF16), w2_ref[cols, :])
    o_ref[...] = _layer_norm(ALPHA * x1 + g2 * acc, ln_g1, ln_b1)


def _out_mlp(ys, xs, ctx2, mods, ln, wo, w1, w2, *, layer, n_tiles, split_ctx):
    half = D // 2
    lat_row = lambda t: (jnp.minimum(t, N_LAT_TILES - 1), 0)
    if split_ctx:
        ya, yb, yac, ybc = ys
        in_specs = [pl.BlockSpec((TM, half), lat_row), pl.BlockSpec((TM, half), lat_row),
                    _full_spec((B * L, half)), _full_spec((B * L, half)),
                    pl.BlockSpec((TM, D), lat_row), _full_spec((B * L, D))]
        args = [ya, yb, yac, ybc, xs, ctx2]
    else:
        (att,) = ys
        in_specs = [pl.BlockSpec((TM, half), lambda t: (t, 0)), pl.BlockSpec((TM, half), lambda t: (t, 1)),
                    pl.BlockSpec((TM, D), lambda t: (t, 0))]
        args = [att, att, xs]
    stacked = lambda w, i: pl.BlockSpec((None,) + w.shape[1:], lambda t: (i, 0, 0), pipeline_mode=pl.Buffered(1))
    in_specs += [_mod_spec(layer), _full_spec(ln[0].shape), _full_spec(ln[1].shape),
                 stacked(wo, 0), stacked(w1, layer), stacked(w2, layer)]
    args += [mods, ln[0], ln[1], wo, w1, w2]
    return pl.pallas_call(
        functools.partial(_out_kernel, layer=layer, split_ctx=split_ctx),
        grid=(n_tiles,),
        in_specs=in_specs,
        out_specs=pl.BlockSpec((TM, D), lambda t: (t, 0)),
        out_shape=jax.ShapeDtypeStruct((n_tiles * TM, D), F32),
        compiler_params=pltpu.CompilerParams(
            dimension_semantics=("arbitrary",), vmem_limit_bytes=VMEM_LIMIT),
        scratch_shapes=[pltpu.VMEM((D, D), BF16)],
        name="out_mlp%d" % layer,
    )(*args)


def _in1_kernel(x_ref, mod_ref, cos_ref, sin_ref, w_ref, qT_o, k_o, vT_o, wqT_s, wk_s, wvT_s):
    @pl.when(pl.program_id(0) == 0)
    def _():
        wqT_s[...] = _transpose_bf16(w_ref[:, 0:D])
        wk_s[...] = w_ref[:, D:D + ATT_KVW].astype(BF16)
        wvT_s[...] = _transpose_bf16(w_ref[:, D + ATT_KVW:D + 2 * ATT_KVW])

    sh1 = mod_ref[:, 0:D]
    sc1 = mod_ref[:, D:2 * D]
    h = (x_ref[...] * (1.0 + sc1) + sh1).astype(BF16)
    nsl = TM // SP

    cos = cos_ref[...]
    sin = sin_ref[...]
    first = (lax.broadcasted_iota(jnp.int32, (TM, 128), 1) % 32) < 16
    sa = jnp.where(first, -sin, 0.0)
    sb = jnp.where(first, 0.0, sin)

    k = _dot(h, wk_s[...])
    for i in range(ATT_KVW // 128):
        u = k[:, i * 128:(i + 1) * 128]
        r = u * cos + pltpu.roll(u, 128 - 16, 1) * sa + pltpu.roll(u, 16, 1) * sb
        k_o[:, i * 128:(i + 1) * 128] = r.astype(BF16)

    cosT, saT, sbT = cos.T, sa.T, sb.T
    qT = _dot_nt(wqT_s[...], h) * (LOG2E * ATT_DH ** -0.5)
    for i in range(D // 128):
        u = qT[i * 128:(i + 1) * 128, :]
        r = (u * cosT + pltpu.roll(u, 128 - 16, 0) * saT + pltpu.roll(u, 16, 0) * sbT).astype(BF16)
        for s in range(nsl):
            qT_o[s, i * 128:(i + 1) * 128, :] = r[:, s * SP:(s + 1) * SP]

    vT = _dot_nt(wvT_s[...], h).astype(BF16)
    for s in range(nsl):
        vT_o[s] = vT[:, s * SP:(s + 1) * SP]


def _in_proj1(xs, mods, cos_tab, sin_tab, wqkv):
    tile = lambda t: jnp.where(t == N_LAT_TILES, TILES_PER_BATCH, t % TILES_PER_BATCH)
    row = lambda t: (t, 0)
    slab = lambda t: (t, 0, 0)
    nsl = TM // SP
    tab_spec = pl.BlockSpec((TM, 128), lambda t: (tile(t), 0))
    return pl.pallas_call(
        _in1_kernel,
        grid=(N_TILES,),
        in_specs=[pl.BlockSpec((TM, D), row), _mod_spec(1), tab_spec, tab_spec,
                  pl.BlockSpec((None,) + wqkv.shape[1:], lambda t: (0, 0, 0), pipeline_mode=pl.Buffered(1))],
        out_specs=[pl.BlockSpec((nsl, D, SP), slab), pl.BlockSpec((TM, ATT_KVW), row),
                   pl.BlockSpec((nsl, ATT_KVW, SP), slab)],
        out_shape=[jax.ShapeDtypeStruct((N_SLABS, D, SP), BF16), jax.ShapeDtypeStruct((ROWS, ATT_KVW), BF16),
                   jax.ShapeDtypeStruct((N_SLABS, ATT_KVW, SP), BF16)],
        scratch_shapes=[pltpu.VMEM((D, D), BF16), pltpu.VMEM((D, ATT_KVW), BF16), pltpu.VMEM((ATT_KVW, D), BF16)],
        compiler_params=pltpu.CompilerParams(
            dimension_semantics=("arbitrary",), vmem_limit_bytes=VMEM_LIMIT),
        name="in_proj1",
    )(xs, mods, cos_tab, sin_tab, wqkv)


def _attn_kernel(sink_ref, qT_ref, k_ref, vT_ref, kc_ref, vcT_ref, o_ref, bias_ref):
    n = pl.program_id(1)
    nql = ATT_GROUP * QB
    lane_g = lax.broadcasted_iota(jnp.int32, (1, nql), 1) // QB
    zero_half = jnp.zeros((ATT_DH, nql), BF16)
    ones_rows = jnp.ones((16, KWIN + L), BF16)

    @pl.when((pl.program_id(0) == 0) & (n == 0))
    def _():
        kj = lax.broadcasted_iota(jnp.int32, (KWIN, QB), 0)
        qi = lax.broadcasted_iota(jnp.int32, (KWIN, QB), 1)
        for case, delta in enumerate((-QB, 0, -2 * QB)):
            d = kj - qi + delta
            bias_ref[case] = jnp.where((d >= -WINDOW) & (d <= WINDOW), 0.0, -jnp.inf)

    def scores(i, kh):
        n0 = (n * (TQ // QB) + i) * QB
        start = pl.multiple_of(jnp.clip(n0 - QB, 0, T - KWIN), QB)
        bias1 = bias_ref[jnp.where(n0 == 0, 1, jnp.where(n0 == T - QB, 2, 0))]
        bias = jnp.concatenate([bias1] * ATT_GROUP, axis=1)
        pair = slice((kh // 2) * 128, (kh // 2 + 1) * 128)
        qT = jnp.concatenate(
            [qT_ref[i, (kh * ATT_GROUP + g) * ATT_DH:(kh * ATT_GROUP + g + 1) * ATT_DH, :]
             for g in range(ATT_GROUP)], axis=1)
        q_pad = jnp.concatenate([zero_half, qT] if kh % 2 else [qT, zero_half], axis=0)
        s_loc = _dot(k_ref[pl.ds(start, KWIN), pair], q_pad) + bias
        s_ctx = _dot(kc_ref[:, pair], q_pad)
        return s_loc, s_ctx, start // SP

    def finish(i, kh, s_loc, s_ctx, slab0):
        drows = slice(kh * ATT_DH, (kh + 1) * ATT_DH)
        sink = jnp.zeros((1, nql), F32)
        for g in range(ATT_GROUP):
            sink = jnp.where(lane_g == g, sink_ref[0, kh * ATT_GROUP + g] * LOG2E, sink)
        m = jnp.maximum(jnp.maximum(jnp.max(s_loc, axis=0, keepdims=True),
                                    jnp.max(s_ctx, axis=0, keepdims=True)), sink)
        pT = jnp.concatenate([jnp.exp2(s_loc - m).astype(BF16),
                              jnp.exp2(s_ctx - m).astype(BF16)], axis=0)
        vT = jnp.concatenate([vT_ref[slab0 + t, drows, :] for t in range(KWIN // SP)]
                             + [vcT_ref[t, drows, :] for t in range(L // SP)], axis=1)
        o_ext = _dot(jnp.concatenate([vT, ones_rows], axis=0), pT)
        denom = o_ext[ATT_DH:ATT_DH + 1, :] + jnp.exp2(sink - m)
        oT = o_ext[0:ATT_DH, :] / denom
        for g in range(0, ATT_GROUP, 2):
            two = jnp.concatenate([oT[:, g * QB:(g + 1) * QB], oT[:, (g + 1) * QB:(g + 2) * QB]], axis=0)
            c0 = (kh * ATT_GROUP + g) * ATT_DH
            o_ref[i * QB:(i + 1) * QB, c0:c0 + 2 * ATT_DH] = two.T.astype(BF16)

    items = [(i, kh) for i in range(TQ // QB) for kh in range(ATT_KVH)]
    pending = scores(*items[0])
    for t, item in enumerate(items):
        nxt = scores(*items[t + 1]) if t + 1 < len(items) else None
        finish(*item, *pending)
        pending = nxt


def _attention(sink, qT, k, vT):
    nq = T // TQ
    return pl.pallas_call(
        _attn_kernel,
        grid=(B, nq),
        in_specs=[
            pl.BlockSpec(memory_space=pltpu.SMEM),
            pl.BlockSpec((TQ // SP, D, SP), lambda b, n: (b * nq + n, 0, 0)),
            pl.BlockSpec((T, ATT_KVW), lambda b, n: (b, 0)),
            pl.BlockSpec((T // SP, ATT_KVW, SP), lambda b, n: (b, 0, 0)),
            pl.BlockSpec((L, ATT_KVW), lambda b, n: (LAT_ROWS // L + b, 0)),
            pl.BlockSpec((L // SP, ATT_KVW, SP), lambda b, n: (LAT_ROWS // L + b, 0, 0)),
        ],
        out_specs=pl.BlockSpec((TQ, D), lambda b, n: (b * nq + n, 0)),
        out_shape=jax.ShapeDtypeStruct((LAT_ROWS, D), BF16),
        compiler_params=pltpu.CompilerParams(
            dimension_semantics=("arbitrary", "arbitrary"), vmem_limit_bytes=VMEM_LIMIT),
        scratch_shapes=[pltpu.VMEM((3, KWIN, QB), F32)],
        name="window_attn",
    )(sink, qT, k, vT, k, vT)


def _rope_tables():
    half = ATT_DH // 2
    inv_freq = np.power(np.float32(ROPE_BASE), -np.arange(0, half, 2, dtype=np.float32) / np.float32(half))
    inv_freq = inv_freq.astype(np.float32)
    pos = np.arange(T)
    ang_r = (pos // GRID_W).astype(np.float32)[:, None] * inv_freq[None, :]
    ang_c = (pos % GRID_W).astype(np.float32)[:, None] * inv_freq[None, :]
    ang = np.concatenate([ang_r, ang_r, ang_c, ang_c], axis=1)
    ang = np.concatenate([ang, np.zeros((TM, ATT_DH), np.float32)], axis=0)
    ang = np.concatenate([ang, ang], axis=1)
    return np.cos(ang).astype(np.float32), np.sin(ang).astype(np.float32)


def kernel(x, c, ctx, c_ctx, w_mod, b_mod, ln_g, ln_b, mlp_w1, mlp_w2, ev_w_in, ev_ret_theta, ev_gla_gk_w,
           ev_gla_gk_b, ev_gla_norm_g, ev_w_out, od_w_qkv, od_sink, od_w_out):
    x2 = x.reshape(LAT_ROWS, D)
    ctx2 = ctx.reshape(B * L, D)

    cs = jnp.concatenate([c, c_ctx[None, :], jnp.zeros((8 - B - 1, D), F32)], axis=0)
    mods = _modulation(cs, w_mod, b_mod).reshape(DEPTH * 8, 1, 6 * D)
    ln = (ln_g.reshape(2 * DEPTH, D), ln_b.reshape(2 * DEPTH, D))

    gk_w = ev_gla_gk_w[0]
    zeros = jnp.zeros((GATE_RANK, QK_W), F32)
    g2 = jnp.concatenate([jnp.concatenate([gk_w[0], zeros], axis=1),
                          jnp.concatenate([zeros, gk_w[1]], axis=1)], axis=0)
    qa, kaT, va, ga, qb, kbT, vb, gbv, lfT, lbT = _in_proj0(
        x2, ctx2, mods, ev_w_in[0].T, g2.T.astype(BF16), ev_gla_gk_b[0].reshape(2 * QK_W, 1))

    w1_rows = DEPTH * D // SCAN_STEPS
    w2_rows = DEPTH * D_FF // SCAN_STEPS
    y_ret, yc_ret, w1 = _scan_group(qa, kaT, va, ga, (ev_ret_theta,),
                                    mlp_w1.reshape(SCAN_STEPS, w1_rows, D_FF), gla=False)
    y_gla, yc_gla, w2 = _scan_group(qb, kbT, vb, gbv, (lfT, lbT, ev_gla_norm_g),
                                    mlp_w2.reshape(SCAN_STEPS, w2_rows, D), gla=True)
    w1 = w1.reshape(DEPTH, D, D_FF)
    w2 = w2.reshape(DEPTH, D_FF, D)

    xs = _out_mlp((y_ret, y_gla, yc_ret, yc_gla), x2, ctx2, mods, ln, ev_w_out, w1, w2,
                  layer=0, n_tiles=N_TILES, split_ctx=True)

    cos_tab, sin_tab = _rope_tables()
    q1T, k1, v1T = _in_proj1(xs, mods, cos_tab, sin_tab, od_w_qkv)
    att = _attention(od_sink, q1T, k1, v1T)
    out = _out_mlp((att,), xs, None, mods, ln, od_w_out, w1, w2,
                   layer=1, n_tiles=N_LAT_TILES, split_ctx=False)
    return out.reshape(B, T, D)
```

```python
import functools

import jax
import jax.numpy as jnp
import numpy as np
from jax import lax
from jax.experimental import pallas as pl
from jax.experimental.pallas import tpu as pltpu

F32 = jnp.float32
BF16 = jnp.bfloat16

D = 1024
B = 2
T = 8192
L = 256
DEPTH = 2
GRID_W = 64
D_FF = 4 * D
HEAD_DV = 128
HEAD_DK = 64
GATE_RANK = 16
GATE_TAU = 16.0
QK_W = 256
V_W = 512
ATT_DH = 64
ATT_QH = 16
ATT_KVH = 4
ATT_GROUP = 4
ATT_KVW = ATT_KVH * ATT_DH
WINDOW = 128
ROPE_BASE = 10000.0
ALPHA = (2.0 * DEPTH) ** 0.25
LN_EPS = 1e-5
RMS_EPS = 1e-6

TM = 512
LAT_ROWS = B * T
ROWS = LAT_ROWS + B * L
N_LAT_TILES = LAT_ROWS // TM
N_TILES = ROWS // TM
TILES_PER_BATCH = T // TM

CH = 64
SP = 2 * CH
TB = 1024
NP = TB // SP
NP_CTX = L // SP
NBLK = T // TB
NP_ALL = NP_CTX + T // SP
N_SLABS = ROWS // SP
KV_UNROLL = 8
OUT_UNROLL = 8

TQ = 512
QB = 128
KWIN = 3 * QB
LOG2E = 1.4426950408889634

VMEM_LIMIT = 56 * 1024 * 1024


def _dot(a, b):
    return jnp.dot(a, b, preferred_element_type=F32)


def _dot_nt(a, b):
    return lax.dot_general(a, b, (((1,), (1,)), ((), ())), preferred_element_type=F32)


def _full_spec(shape):
    nd = len(shape)
    return pl.BlockSpec(shape, lambda *_: (0,) * nd, pipeline_mode=pl.Buffered(1))


def _mod_row(t):
    return jnp.minimum(t // TILES_PER_BATCH, B)


def _mod_spec(layer):
    return pl.BlockSpec((None, 1, 6 * D), lambda t: (layer * 8 + _mod_row(t), 0, 0))


def _layer_norm(x, g, b):
    mu = jnp.mean(x, axis=-1, keepdims=True)
    xc = x - mu
    var = jnp.mean(xc * xc, axis=-1, keepdims=True)
    return xc * lax.rsqrt(var + LN_EPS) * g + b


def _log_sigmoid(z):
    return jnp.minimum(z, 0.0) - jnp.log1p(jnp.exp(-jnp.abs(z)))


MOD_TN = 1536


def _mod_kernel(c_ref, w_ref, b_ref, o_ref):
    s = jax.nn.silu(c_ref[...])
    s_hi = s.astype(BF16)
    s_lo = (s - s_hi.astype(F32)).astype(BF16)
    w = w_ref[...].astype(BF16)
    o_ref[...] = _dot(s_hi, w) + _dot(s_lo, w) + b_ref[...]


def _modulation(cs, w_mod, b_mod):
    return pl.pallas_call(
        _mod_kernel,
        grid=(DEPTH, 6 * D // MOD_TN),
        in_specs=[
            pl.BlockSpec((8, D), lambda i, n: (0, 0)),
            pl.BlockSpec((None, D, MOD_TN), lambda i, n: (i, 0, n)),
            pl.BlockSpec((None, 1, MOD_TN), lambda i, n: (i, 0, n)),
        ],
        out_specs=pl.BlockSpec((None, 8, MOD_TN), lambda i, n: (i, 0, n)),
        out_shape=jax.ShapeDtypeStruct((DEPTH, 8, 6 * D), F32),
        compiler_params=pltpu.CompilerParams(
            dimension_semantics=("arbitrary", "arbitrary"), vmem_limit_bytes=VMEM_LIMIT),
        name="modulation",
    )(cs, w_mod, b_mod.reshape(DEPTH, 1, 6 * D))


IN0_OFF = (0, 256, 512, 1024, 1536, 1792, 2048, 2560, 3072, 3104)


def _transpose_bf16(w):
    return w.astype(F32).T.astype(BF16)


def _in0_kernel(x_ref, ctx_ref, mod_ref, wT_ref, g2T, gbc,
                qa_o, kaT_o, va_o, ga_o, qb_o, kbT_o, vb_o, gb_o, lfT_o, lbT_o, wbT_s):
    t = pl.program_id(0)
    rows = lambda i: slice(IN0_OFF[i], IN0_OFF[i + 1])
    piece = lambda i: wbT_s[rows(i), :]

    @pl.when(t == 0)
    def _():
        for i in range(len(IN0_OFF) - 1):
            wbT_s[rows(i), :] = wT_ref[rows(i), :].astype(BF16)

    xt = jnp.where(t == N_LAT_TILES, ctx_ref[...], x_ref[...])
    sh1 = mod_ref[:, 0:D]
    sc1 = mod_ref[:, D:2 * D]
    h = (xt * (1.0 + sc1) + sh1).astype(BF16)
    qk_scale = HEAD_DK ** -0.5

    def put_slabs(o_ref, val):
        for i in range(TM // SP):
            o_ref[i] = val[:, i * SP:(i + 1) * SP].astype(o_ref.dtype)

    qa_o[...] = _dot_nt(h, piece(0)).astype(BF16)
    put_slabs(kaT_o, _dot_nt(piece(1), h) * qk_scale)
    va_o[...] = _dot_nt(h, piece(2)).astype(BF16)
    ga_o[...] = jax.nn.silu(_dot_nt(h, piece(3))).astype(BF16)

    qb_o[...] = (_dot_nt(h, piece(4)) * qk_scale).astype(BF16)
    put_slabs(kbT_o, _dot_nt(piece(5), h))
    vb_o[...] = _dot_nt(h, piece(6)).astype(BF16)
    gb_o[...] = jax.nn.silu(_dot_nt(h, piece(7))).astype(BF16)

    lr = _dot_nt(h, piece(8)).astype(BF16)
    lsT = _log_sigmoid(_dot_nt(g2T[...], lr) + gbc[...]) * (1.0 / GATE_TAU)
    put_slabs(lfT_o, lsT[0:QK_W, :])
    put_slabs(lbT_o, lsT[QK_W:2 * QK_W, :])


def _in_proj0(x2, ctx2, mods, w_inT, g2T, gbc):
    row = lambda t: (t, 0)
    slab = lambda t: (t, 0, 0)
    nsl = TM // SP
    row_out = lambda width: (jax.ShapeDtypeStruct((ROWS, width), BF16), pl.BlockSpec((TM, width), row))
    slab_out = lambda dt: (jax.ShapeDtypeStruct((N_SLABS, QK_W, SP), dt), pl.BlockSpec((nsl, QK_W, SP), slab))
    outs = [row_out(QK_W), slab_out(BF16), row_out(V_W), row_out(V_W),
            row_out(QK_W), slab_out(BF16), row_out(V_W), row_out(V_W),
            slab_out(BF16), slab_out(BF16)]
    in_specs = [
        pl.BlockSpec((TM, D), lambda t: (jnp.minimum(t, N_LAT_TILES - 1), 0)),
        _full_spec((B * L, D)),
        _mod_spec(0),
        _full_spec(w_inT.shape), _full_spec(g2T.shape), _full_spec(gbc.shape),
    ]
    return pl.pallas_call(
        _in0_kernel,
        grid=(N_TILES,),
        in_specs=in_specs,
        out_specs=[o[1] for o in outs],
        out_shape=[o[0] for o in outs],
        scratch_shapes=[pltpu.VMEM(w_inT.shape, BF16)],
        compiler_params=pltpu.CompilerParams(
            dimension_semantics=("arbitrary",), vmem_limit_bytes=VMEM_LIMIT),
        name="in_proj0",
    )(x2, ctx2, mods, w_inT, g2T, gbc)


def _chunk_diag(kv, c):
    r0 = c * 2 * HEAD_DK
    return jnp.concatenate([kv[r0:r0 + HEAD_DK, 0:HEAD_DV],
                            kv[r0 + HEAD_DK:r0 + 2 * HEAD_DK, HEAD_DV:2 * HEAD_DV]], axis=0)


def _scan_kernel(*refs, gla):
    if gla:
        (q_ref, kT_ref, v_ref, g_ref, lfT_ref, lbT_ref,
         qc_ref, kTc_ref, vc_ref, gc_ref, lfTc_ref, lbTc_ref, ng_ref, wsrc_ref,
         y_ref, yc_ref, wdst_ref, s_ref, r_ref, rst_ref, sst_ref, kv_ref, dec_ref,
         kd_ref, lhs_ref, pm_ref) = refs
        lat = (q_ref, kT_ref, v_ref, g_ref, lfT_ref, lbT_ref)
        cxt = (qc_ref, kTc_ref, vc_ref, gc_ref, lfTc_ref, lbTc_ref)
    else:
        (q_ref, kT_ref, v_ref, g_ref, qc_ref, kTc_ref, vc_ref, gc_ref, th_ref, wsrc_ref,
         y_ref, yc_ref, wdst_ref, s_ref, r_ref, rst_ref, sst_ref, kv_ref, dec_ref,
         kd_ref, lhs_ref, pm_ref) = refs
        lat = (q_ref, kT_ref, v_ref, g_ref, None, None)
        cxt = (qc_ref, kTc_ref, vc_ref, gc_ref, None, None)

    phase = pl.program_id(2)
    j = pl.program_id(3)

    wdst_ref[...] = wsrc_ref[...].astype(BF16)

    ri = lax.broadcasted_iota(jnp.int32, (SP, SP), 0)
    ci = lax.broadcasted_iota(jnp.int32, (SP, SP), 1)
    same = (ri // CH) == (ci // CH)
    first_lane = ci < CH
    head_a = ci < HEAD_DK

    if gla:
        as_w = lambda m: m.astype(BF16)
        tot = jnp.concatenate([jnp.broadcast_to(ri < CH, (SP, SP)), jnp.broadcast_to(ri >= CH, (SP, SP))], axis=1)
        w_end_f = jnp.concatenate([as_w(same & (ri > ci)), as_w(tot)], axis=1)
        w_end_b = jnp.concatenate([as_w(same & (ri < ci)), as_w(tot)], axis=1)
        w_cum_f = as_w(same & (ri <= ci))
        w_cum_b = as_w(same & (ri >= ci))
    else:
        hp = pl.program_id(1)
        th = [[th_ref[0, dr, 2 * hp + hd] for hd in range(2)] for dr in range(2)]
        lane1 = lax.broadcasted_iota(jnp.int32, (1, SP), 1) < HEAD_DK
        row1 = lax.broadcasted_iota(jnp.int32, (SP, 1), 0) < HEAD_DK
        lg_row = [jnp.log1p(-jnp.exp(jnp.where(lane1, th[dr][0], th[dr][1]))) for dr in range(2)]
        lg_col = [jnp.log1p(-jnp.exp(jnp.where(row1, th[dr][0], th[dr][1]))) for dr in range(2)]
        it = (ci % CH).astype(F32)
        ir = (ri % CH).astype(F32)
        ret_end_f = jnp.exp((CH - 1.0 - it) * lg_col[0])
        ret_end_b = jnp.exp(it * lg_col[1])
        ret_dec_f = jnp.exp(jnp.broadcast_to(CH * lg_col[0], (SP, SP)))
        ret_dec_b = jnp.exp(jnp.broadcast_to(CH * lg_col[1], (SP, SP)))
        ret_ebTi = jnp.exp(-(it + 1.0) * lg_col[0])
        ret_erTi = jnp.exp(-(CH - it) * lg_col[1])
        ret_eb = jnp.exp((ir + 1.0) * lg_row[0])
        ret_er = jnp.exp((CH - ir) * lg_row[1])

    def kv_stage(blk, n, fwd):
        _, kT_r, v_r, _, lfT_r, lbT_r = blk

        def body(p, carry):
            kT = kT_r[p].astype(F32)
            v = v_r[pl.ds(pl.multiple_of(p * SP, SP), SP), :]
            if gla:
                res = _dot((lfT_r if fwd else lbT_r)[p], w_end_f if fwd else w_end_b)
                e_end = jnp.exp(res[:, 0:SP])
                dec0 = jnp.exp(res[:, SP:2 * SP])
                dec1 = jnp.exp(res[:, 2 * SP:3 * SP])
            else:
                e_end = ret_end_f if fwd else ret_end_b
                dec0 = dec1 = ret_dec_f if fwd else ret_dec_b
            ke = kT * e_end
            lhs = jnp.concatenate([jnp.where(first_lane, ke, 0.0), jnp.where(first_lane, 0.0, ke)],
                                  axis=0).astype(BF16)
            kv = _dot(lhs, v)
            kv_ref[2 * p] = _chunk_diag(kv, 0)
            kv_ref[2 * p + 1] = _chunk_diag(kv, 1)
            dec_ref[2 * p] = dec0
            dec_ref[2 * p + 1] = dec1
            return carry

        lax.fori_loop(0, n, body, 0, unroll=min(n, KV_UNROLL))

    def phase0_block(blk, n, slot0):
        kv_stage(blk, n, fwd=False)

        def body(i, r_state):
            p = n - 1 - i
            rst_ref[slot0 + p, :, HEAD_DV:2 * HEAD_DV] = r_state.astype(BF16)
            r_state = dec_ref[2 * p + 1] * r_state + kv_ref[2 * p + 1]
            rst_ref[slot0 + p, :, 0:HEAD_DV] = r_state.astype(BF16)
            return dec_ref[2 * p] * r_state + kv_ref[2 * p]

        r_ref[...] = lax.fori_loop(0, n, body, r_ref[...])

    def phase1_block(blk, n, slot0, out_ref):
        q_r, kT_r, v_r, g_r, lfT_r, lbT_r = blk
        kv_stage(blk, n, fwd=True)

        def rec(p, s_state):
            sst_ref[p, :, 0:HEAD_DV] = s_state.astype(BF16)
            s_state = dec_ref[2 * p] * s_state + kv_ref[2 * p]
            sst_ref[p, :, HEAD_DV:2 * HEAD_DV] = s_state.astype(BF16)
            return dec_ref[2 * p + 1] * s_state + kv_ref[2 * p + 1]

        s_ref[...] = lax.fori_loop(0, n, rec, s_ref[...])

        r2 = lax.broadcasted_iota(jnp.int32, (2 * SP, SP), 0) % SP
        c2 = lax.broadcasted_iota(jnp.int32, (2 * SP, SP), 1)
        same2 = (r2 // CH) == (c2 // CH)
        mask_f = same2 & (r2 >= c2)
        mask_b = same2 & (r2 < c2)

        def prep(p, carry):
            rows = pl.ds(pl.multiple_of(p * SP, SP), SP)
            q = q_r[rows, :].astype(F32)
            kT = kT_r[p].astype(F32)
            if gla:
                lfT = lfT_r[p]
                lbT = lbT_r[p]
                ebTi = jnp.exp(-_dot(lfT, w_cum_f))
                erTi = jnp.exp(-_dot(lbT, w_cum_b))
                e_b = jnp.exp(_dot_nt(w_cum_b, lfT))
                e_r = jnp.exp(_dot_nt(w_cum_f, lbT))
            else:
                ebTi, erTi, e_b, e_r = ret_ebTi, ret_erTi, ret_eb, ret_er
            kd_ref[p, 0:SP, 0:SP] = (kT * ebTi).astype(BF16)
            kd_ref[p, SP:2 * SP, SP:2 * SP] = (kT * erTi).astype(BF16)
            qf = q * e_b
            qb = q * e_r
            lhs_ref[p, :, 0:SP] = jnp.concatenate(
                [jnp.where(head_a, qf, 0.0), jnp.where(head_a, 0.0, qf)], axis=0).astype(BF16)
            lhs_ref[p, :, SP:2 * SP] = jnp.concatenate(
                [jnp.where(head_a, qb, 0.0), jnp.where(head_a, 0.0, qb)], axis=0).astype(BF16)
            return carry

        def score(p, carry):
            sc = _dot(lhs_ref[p], kd_ref[p])
            pm_ref[p] = jnp.where(mask_f, sc[:, 0:SP], jnp.where(mask_b, sc[:, SP:2 * SP], 0.0)).astype(BF16)
            return carry

        def emit(p, carry):
            rows = pl.ds(pl.multiple_of(p * SP, SP), SP)
            v = v_r[rows, :]
            states = jnp.concatenate([sst_ref[p], rst_ref[slot0 + p]], axis=0)
            o_int = _dot(lhs_ref[p], states)
            o_a = _dot(pm_ref[p, 0:SP, :], v[:, 0:HEAD_DV]) + jnp.concatenate(
                [o_int[0:CH, 0:HEAD_DV], o_int[CH:SP, HEAD_DV:2 * HEAD_DV]], axis=0)
            o_b = _dot(pm_ref[p, SP:2 * SP, :], v[:, HEAD_DV:2 * HEAD_DV]) + jnp.concatenate(
                [o_int[SP:SP + CH, 0:HEAD_DV], o_int[SP + CH:2 * SP, HEAD_DV:2 * HEAD_DV]], axis=0)

            def nrm(o):
                y = o * lax.rsqrt(jnp.mean(o * o, axis=-1, keepdims=True) + RMS_EPS)
                return y * ng_ref[...] if gla else y

            y = jnp.concatenate([nrm(o_a), nrm(o_b)], axis=1) * g_r[rows, :].astype(F32)
            out_ref[rows, :] = y.astype(BF16)
            return carry

        lax.fori_loop(0, n, prep, 0, unroll=min(n, OUT_UNROLL))
        lax.fori_loop(0, n, score, 0, unroll=min(n, OUT_UNROLL))
        lax.fori_loop(0, n, emit, 0, unroll=min(n, OUT_UNROLL))

    @pl.when((pl.program_id(0) == 0) & (pl.program_id(1) == 0) & (phase == 0) & (j == 0))
    def _():
        kd_ref[...] = jnp.zeros_like(kd_ref)

    @pl.when(phase == 0)
    def _():
        @pl.when(j == 0)
        def _():
            r_ref[...] = jnp.zeros_like(r_ref)
            phase0_block(cxt, NP_CTX, 0)

        phase0_block(lat, NP, NP_CTX + (NBLK - 1 - j) * NP)

    @pl.when(phase == 1)
    def _():
        @pl.when(j == 0)
        def _():
            s_ref[...] = jnp.zeros_like(s_ref)
            phase1_block(cxt, NP_CTX, 0, yc_ref)

        phase1_block(lat, NP, NP_CTX + j * NP, y_ref)


SCAN_STEPS = B * 2 * 2 * NBLK


def _scan_group(q, kT, v, g, extra, wsrc, *, gla):
    w_spec = pl.BlockSpec((None,) + wsrc.shape[1:],
                          lambda b, p, ph, j: (((b * 2 + p) * 2 + ph) * NBLK + j, 0, 0))

    def blk(b, ph, j, used_in_phase0):
        jj = jnp.where(ph == 0, NBLK - 1 - j, j)
        if not used_in_phase0:
            jj = jnp.where(ph == 0, 0, jj)
        return b * NBLK + jj

    def lat_specs(used0):
        return dict(
            row=lambda w: pl.BlockSpec((TB, w), lambda b, p, ph, j: (blk(b, ph, j, used0), p)),
            slab=pl.BlockSpec((NP, SP, SP), lambda b, p, ph, j: (blk(b, ph, j, used0), p, 0)))

    ctx_row = lambda w: pl.BlockSpec((L, w), lambda b, p, ph, j: (LAT_ROWS // L + b, p))
    ctx_slab = pl.BlockSpec((NP_CTX, SP, SP), lambda b, p, ph, j: (LAT_ROWS // L + b, p, 0))
    used, unused = lat_specs(True), lat_specs(False)

    in_specs = [unused["row"](2 * HEAD_DK), used["slab"], used["row"](2 * HEAD_DV), unused["row"](2 * HEAD_DV)]
    ctx_specs = [ctx_row(2 * HEAD_DK), ctx_slab, ctx_row(2 * HEAD_DV), ctx_row(2 * HEAD_DV)]
    if gla:
        lfT, lbT, ng = extra
        in_specs += [unused["slab"], used["slab"]]
        ctx_specs += [ctx_slab, ctx_slab]
        args = (q, kT, v, g, lfT, lbT, q, kT, v, g, lfT, lbT, ng, wsrc)
        in_specs = in_specs + ctx_specs + [pl.BlockSpec((1, HEAD_DV), lambda b, p, ph, j: (0, 0)), w_spec]
    else:
        (theta,) = extra
        args = (q, kT, v, g, q, kT, v, g, theta, wsrc)
        in_specs = in_specs + ctx_specs + [pl.BlockSpec(memory_space=pltpu.SMEM), w_spec]
    return pl.pallas_call(
        functools.partial(_scan_kernel, gla=gla),
        grid=(B, 2, 2, NBLK),
        in_specs=in_specs,
        out_specs=[
            pl.BlockSpec((TB, 2 * HEAD_DV), lambda b, p, ph, j: (b * NBLK + jnp.where(ph == 0, 0, j), p)),
            pl.BlockSpec((L, 2 * HEAD_DV), lambda b, p, ph, j: (b, p)),
            w_spec,
        ],
        out_shape=[jax.ShapeDtypeStruct((LAT_ROWS, V_W), BF16), jax.ShapeDtypeStruct((B * L, V_W), BF16),
                   jax.ShapeDtypeStruct(wsrc.shape, BF16)],
        scratch_shapes=[
            pltpu.VMEM((SP, HEAD_DV), F32),
            pltpu.VMEM((SP, HEAD_DV), F32),
            pltpu.VMEM((NP_ALL, SP, 2 * HEAD_DV), BF16),
            pltpu.VMEM((NP, SP, 2 * HEAD_DV), BF16),
            pltpu.VMEM((2 * NP, SP, HEAD_DV), F32),
            pltpu.VMEM((2 * NP, SP, HEAD_DV), F32),
            pltpu.VMEM((NP, 2 * SP, 2 * SP), BF16),
            pltpu.VMEM((NP, 2 * SP, 2 * SP), BF16),
            pltpu.VMEM((NP, 2 * SP, SP), BF16),
        ],
        compiler_params=pltpu.CompilerParams(
            dimension_semantics=("arbitrary",) * 4, vmem_limit_bytes=VMEM_LIMIT),
        name="scan_gla" if gla else "scan_ret",
    )(*args)


FF_CH = 512


def _out_kernel(*refs, layer, n_tiles, split_ctx):
    if split_ctx:
        (ya_ref, yb_ref, yac_ref, ybc_ref, x_ref, ctx_ref, mod_a, mod_c, lng_ref, lnb_ref,
         wo_ref, w1_ref, w2_ref, o_ref, wo_s, *slots) = refs
    else:
        (ya_ref, yb_ref, x_ref, mod_a, mod_c, lng_ref, lnb_ref,
         wo_ref, w1_ref, w2_ref, o_ref, wo_s, *slots) = refs
    x1_s, h2_s, acc_s = slots[0:2], slots[2:4], slots[4:6]
    g = pl.program_id(0)
    ln_g0 = lng_ref[2 * layer:2 * layer + 1, :]
    ln_g1 = lng_ref[2 * layer + 1:2 * layer + 2, :]
    ln_b0 = lnb_ref[2 * layer:2 * layer + 1, :]
    ln_b1 = lnb_ref[2 * layer + 1:2 * layer + 2, :]
    half = D // 2

    @pl.when(g == 0)
    def _():
        wo_s[...] = wo_ref[...].astype(BF16)

    def stage_a(slot):
        if split_ctx:
            is_ctx = g == N_LAT_TILES
            x = jnp.where(is_ctx, ctx_ref[...], x_ref[...])
            ya = jnp.where(is_ctx, yac_ref[...], ya_ref[...])
            yb = jnp.where(is_ctx, ybc_ref[...], yb_ref[...])
        else:
            x, ya, yb = x_ref[...], ya_ref[...], yb_ref[...]
        g1 = mod_a[:, 2 * D:3 * D]
        sh2 = mod_a[:, 3 * D:4 * D]
        sc2 = mod_a[:, 4 * D:5 * D]
        y = _dot(ya, wo_s[0:half, :]) + _dot(yb, wo_s[half:D, :])
        x1 = _layer_norm(ALPHA * x + g1 * y, ln_g0, ln_b0)
        x1_s[slot][...] = x1
        h2_s[slot][...] = (x1 * (1.0 + sc2) + sh2).astype(BF16)

    def stage_b(slot):
        h2 = h2_s[slot][...]
        acc = jnp.zeros((TM, D), F32)
        for c in range(D_FF // FF_CH):
            cols = slice(c * FF_CH, (c + 1) * FF_CH)
            hc = jnp.maximum(_dot(h2, w1_ref[:, cols]), 0.0)
            acc = acc + _dot((hc * hc).astype(BF16), w2_ref[cols, :])
        acc_s[slot][...] = acc

    def stage_c(slot):
        g2 = mod_c[:, 5 * D:6 * D]
        o_ref[...] = _layer_norm(ALPHA * x1_s[slot][...] + g2 * acc_s[slot][...], ln_g1, ln_b1)

    @pl.when(g == 0)
    def _():
        stage_a(0)

    @pl.when(g == 1)
    def _():
        stage_a(1)
        stage_b(0)

    def steady(parity):
        n_ch = D_FF // FF_CH
        rs = TM // n_ch
        if split_ctx:
            is_ctx = g == N_LAT_TILES
            x = jnp.where(is_ctx, ctx_ref[...], x_ref[...])
            ya = jnp.where(is_ctx, yac_ref[...], ya_ref[...])
            yb = jnp.where(is_ctx, ybc_ref[...], yb_ref[...])
        else:
            x, ya, yb = x_ref[...], ya_ref[...], yb_ref[...]
        g1 = mod_a[:, 2 * D:3 * D]
        sh2 = mod_a[:, 3 * D:4 * D]
        sc2 = mod_a[:, 4 * D:5 * D]
        g2 = mod_c[:, 5 * D:6 * D]
        y = _dot(ya, wo_s[0:half, :]) + _dot(yb, wo_s[half:D, :])
        h2 = h2_s[1 - parity][...]
        acc = jnp.zeros((TM, D), F32)
        for c in range(n_ch):
            cols = slice(c * FF_CH, (c + 1) * FF_CH)
            hc = jnp.maximum(_dot(h2, w1_ref[:, cols]), 0.0)
            acc = acc + _dot((hc * hc).astype(BF16), w2_ref[cols, :])
            bits = pltpu.bitcast(acc[0:8, 0:128], jnp.uint32)
            zero = pltpu.bitcast((bits >> 16) >> 16, F32)
            zero = jnp.concatenate([jnp.concatenate([zero] * (D // 128), axis=1)] * (rs // 8), axis=0)
            rows = slice(c * rs, (c + 1) * rs)
            x1_old = x1_s[parity][rows, :] + zero
            o_ref[rows, :] = _layer_norm(ALPHA * x1_old + g2 * acc_s[parity][rows, :], ln_g1, ln_b1)
            x1 = _layer_norm(ALPHA * (x[rows] + zero) + g1 * y[rows], ln_g0, ln_b0)
            x1_s[parity][rows, :] = x1
            h2_s[parity][rows, :] = (x1 * (1.0 + sc2) + sh2).astype(BF16)
        acc_s[1 - parity][...] = acc

    for parity in range(2):
        @pl.when((g >= 2) & (g < n_tiles) & (g % 2 == parity))
        def _():
            steady(parity)

    @pl.when(g == n_tiles)
    def _():
        stage_c(n_tiles % 2)
        stage_b((n_tiles - 1) % 2)

    @pl.when(g == n_tiles + 1)
    def _():
        stage_c((n_tiles - 1) % 2)


def _out_mlp(ys, xs, ctx2, mods, ln, wo, w1, w2, *, layer, n_tiles, split_ctx):
    half = D // 2
    tile_a = lambda g: jnp.minimum(g, n_tiles - 1)
    tile_c = lambda g: jnp.maximum(g - 2, 0)
    lat_row = lambda g: (jnp.minimum(g, N_LAT_TILES - 1), 0)
    if split_ctx:
        ya, yb, yac, ybc = ys
        in_specs = [pl.BlockSpec((TM, half), lat_row), pl.BlockSpec((TM, half), lat_row),
                    _full_spec((B * L, half)), _full_spec((B * L, half)),
                    pl.BlockSpec((TM, D), lat_row), _full_spec((B * L, D))]
        args = [ya, yb, yac, ybc, xs, ctx2]
    else:
        (att,) = ys
        in_specs = [pl.BlockSpec((TM, half), lambda g: (tile_a(g), 0)),
                    pl.BlockSpec((TM, half), lambda g: (tile_a(g), 1)),
                    pl.BlockSpec((TM, D), lambda g: (tile_a(g), 0))]
        args = [att, att, xs]
    mod_spec = lambda tile: pl.BlockSpec((None, 1, 6 * D), lambda g: (layer * 8 + _mod_row(tile(g)), 0, 0))
    stacked = lambda w, i: pl.BlockSpec((None,) + w.shape[1:], lambda g: (i, 0, 0), pipeline_mode=pl.Buffered(1))
    in_specs += [mod_spec(tile_a), mod_spec(tile_c), _full_spec(ln[0].shape), _full_spec(ln[1].shape),
                 stacked(wo, 0), stacked(w1, layer), stacked(w2, layer)]
    args += [mods, mods, ln[0], ln[1], wo, w1, w2]
    return pl.pallas_call(
        functools.partial(_out_kernel, layer=layer, n_tiles=n_tiles, split_ctx=split_ctx),
        grid=(n_tiles + 2,),
        in_specs=in_specs,
        out_specs=pl.BlockSpec((TM, D), lambda g: (tile_c(g), 0)),
        out_shape=jax.ShapeDtypeStruct((n_tiles * TM, D), F32),
        compiler_params=pltpu.CompilerParams(
            dimension_semantics=("arbitrary",), vmem_limit_bytes=VMEM_LIMIT),
        scratch_shapes=[pltpu.VMEM((D, D), BF16),
                        pltpu.VMEM((TM, D), F32), pltpu.VMEM((TM, D), F32),
                        pltpu.VMEM((TM, D), BF16), pltpu.VMEM((TM, D), BF16),
                        pltpu.VMEM((TM, D), F32), pltpu.VMEM((TM, D), F32)],
        name="out_mlp%d" % layer,
    )(*args)


def _in1_kernel(x_ref, mod_ref, cos_ref, sin_ref, w_ref, qT_o, k_o, vT_o, wqT_s, wk_s, wvT_s):
    @pl.when(pl.program_id(0) == 0)
    def _():
        wqT_s[...] = _transpose_bf16(w_ref[:, 0:D])
        wk_s[...] = w_ref[:, D:D + ATT_KVW].astype(BF16)
        wvT_s[...] = _transpose_bf16(w_ref[:, D + ATT_KVW:D + 2 * ATT_KVW])

    sh1 = mod_ref[:, 0:D]
    sc1 = mod_ref[:, D:2 * D]
    h = (x_ref[...] * (1.0 + sc1) + sh1).astype(BF16)
    nsl = TM // SP

    cos = cos_ref[...]
    sin = sin_ref[...]
    first = (lax.broadcasted_iota(jnp.int32, (TM, 128), 1) % 32) < 16
    sa = jnp.where(first, -sin, 0.0)
    sb = jnp.where(first, 0.0, sin)

    k = _dot(h, wk_s[...])
    for i in range(ATT_KVW // 128):
        u = k[:, i * 128:(i + 1) * 128]
        r = u * cos + pltpu.roll(u, 128 - 16, 1) * sa + pltpu.roll(u, 16, 1) * sb
        k_o[:, i * 128:(i + 1) * 128] = r.astype(BF16)

    cosT, saT, sbT = cos.T, sa.T, sb.T
    qT = _dot_nt(wqT_s[...], h) * (LOG2E * ATT_DH ** -0.5)
    for i in range(D // 128):
        u = qT[i * 128:(i + 1) * 128, :]
        r = (u * cosT + pltpu.roll(u, 128 - 16, 0) * saT + pltpu.roll(u, 16, 0) * sbT).astype(BF16)
        for s in range(nsl):
            qT_o[s, i * 128:(i + 1) * 128, :] = r[:, s * SP:(s + 1) * SP]

    vT = _dot_nt(wvT_s[...], h).astype(BF16)
    for s in range(nsl):
        vT_o[s] = vT[:, s * SP:(s + 1) * SP]


def _in_proj1(xs, mods, cos_tab, sin_tab, wqkv):
    tile = lambda t: jnp.where(t == N_LAT_TILES, TILES_PER_BATCH, t % TILES_PER_BATCH)
    row = lambda t: (t, 0)
    slab = lambda t: (t, 0, 0)
    nsl = TM // SP
    tab_spec = pl.BlockSpec((TM, 128), lambda t: (tile(t), 0))
    return pl.pallas_call(
        _in1_kernel,
        grid=(N_TILES,),
        in_specs=[pl.BlockSpec((TM, D), row), _mod_spec(1), tab_spec, tab_spec,
                  pl.BlockSpec((None,) + wqkv.shape[1:], lambda t: (0, 0, 0), pipeline_mode=pl.Buffered(1))],
        out_specs=[pl.BlockSpec((nsl, D, SP), slab), pl.BlockSpec((TM, ATT_KVW), row),
                   pl.BlockSpec((nsl, ATT_KVW, SP), slab)],
        out_shape=[jax.ShapeDtypeStruct((N_SLABS, D, SP), BF16), jax.ShapeDtypeStruct((ROWS, ATT_KVW), BF16),
                   jax.ShapeDtypeStruct((N_SLABS, ATT_KVW, SP), BF16)],
        scratch_shapes=[pltpu.VMEM((D, D), BF16), pltpu.VMEM((D, ATT_KVW), BF16), pltpu.VMEM((ATT_KVW, D), BF16)],
        compiler_params=pltpu.CompilerParams(
            dimension_semantics=("arbitrary",), vmem_limit_bytes=VMEM_LIMIT),
        name="in_proj1",
    )(xs, mods, cos_tab, sin_tab, wqkv)


def _attn_kernel(sink_ref, qT_ref, k_ref, vT_ref, kc_ref, vcT_ref, o_ref, bias_ref):
    n = pl.program_id(1)
    nql = ATT_GROUP * QB
    lane_g = lax.broadcasted_iota(jnp.int32, (1, nql), 1) // QB
    zero_half = jnp.zeros((ATT_DH, nql), BF16)
    ones_rows = jnp.ones((16, KWIN + L), BF16)

    @pl.when((pl.program_id(0) == 0) & (n == 0))
    def _():
        kj = lax.broadcasted_iota(jnp.int32, (KWIN, QB), 0)
        qi = lax.broadcasted_iota(jnp.int32, (KWIN, QB), 1)
        for case, delta in enumerate((-QB, 0, -2 * QB)):
            d = kj - qi + delta
            bias_ref[case] = jnp.where((d >= -WINDOW) & (d <= WINDOW), 0.0, -jnp.inf)

    def scores(i, kh):
        n0 = (n * (TQ // QB) + i) * QB
        start = pl.multiple_of(jnp.clip(n0 - QB, 0, T - KWIN), QB)
        bias1 = bias_ref[jnp.where(n0 == 0, 1, jnp.where(n0 == T - QB, 2, 0))]
        bias = jnp.concatenate([bias1] * ATT_GROUP, axis=1)
        pair = slice((kh // 2) * 128, (kh // 2 + 1) * 128)
        qT = jnp.concatenate(
            [qT_ref[i, (kh * ATT_GROUP + g) * ATT_DH:(kh * ATT_GROUP + g + 1) * ATT_DH, :]
             for g in range(ATT_GROUP)], axis=1)
        q_pad = jnp.concatenate([zero_half, qT] if kh % 2 else [qT, zero_half], axis=0)
        s_loc = _dot(k_ref[pl.ds(start, KWIN), pair], q_pad) + bias
        s_ctx = _dot(kc_ref[:, pair], q_pad)
        return s_loc, s_ctx, start // SP

    def finish(i, kh, s_loc, s_ctx, slab0):
        drows = slice(kh * ATT_DH, (kh + 1) * ATT_DH)
        sink = jnp.zeros((1, nql), F32)
        for g in range(ATT_GROUP):
            sink = jnp.where(lane_g == g, sink_ref[0, kh * ATT_GROUP + g] * LOG2E, sink)
        m = jnp.maximum(jnp.maximum(jnp.max(s_loc, axis=0, keepdims=True),
                                    jnp.max(s_ctx, axis=0, keepdims=True)), sink)
        pT = jnp.concatenate([jnp.exp2(s_loc - m).astype(BF16),
                              jnp.exp2(s_ctx - m).astype(BF16)], axis=0)
        vT = jnp.concatenate([vT_ref[slab0 + t, drows, :] for t in range(KWIN // SP)]
                             + [vcT_ref[t, drows, :] for t in range(L // SP)], axis=1)
        o_ext = _dot(jnp.concatenate([vT, ones_rows], axis=0), pT)
        denom = o_ext[ATT_DH:ATT_DH + 1, :] + jnp.exp2(sink - m)
        oT = o_ext[0:ATT_DH, :] / denom
        for g in range(0, ATT_GROUP, 2):
            two = jnp.concatenate([oT[:, g * QB:(g + 1) * QB], oT[:, (g + 1) * QB:(g + 2) * QB]], axis=0)
            c0 = (kh * ATT_GROUP + g) * ATT_DH
            o_ref[i * QB:(i + 1) * QB, c0:c0 + 2 * ATT_DH] = two.T.astype(BF16)

    items = [(i, kh) for i in range(TQ // QB) for kh in range(ATT_KVH)]
    pending = scores(*items[0])
    for t, item in enumerate(items):
        nxt = scores(*items[t + 1]) if t + 1 < len(items) else None
        finish(*item, *pending)
        pending = nxt


def _attention(sink, qT, k, vT):
    nq = T // TQ
    return pl.pallas_call(
        _attn_kernel,
        grid=(B, nq),
        in_specs=[
            pl.BlockSpec(memory_space=pltpu.SMEM),
            pl.BlockSpec((TQ // SP, D, SP), lambda b, n: (b * nq + n, 0, 0)),
            pl.BlockSpec((T, ATT_KVW), lambda b, n: (b, 0)),
            pl.BlockSpec((T // SP, ATT_KVW, SP), lambda b, n: (b, 0, 0)),
            pl.BlockSpec((L, ATT_KVW), lambda b, n: (LAT_ROWS // L + b, 0)),
            pl.BlockSpec((L // SP, ATT_KVW, SP), lambda b, n: (LAT_ROWS // L + b, 0, 0)),
        ],
        out_specs=pl.BlockSpec((TQ, D), lambda b, n: (b * nq + n, 0)),
        out_shape=jax.ShapeDtypeStruct((LAT_ROWS, D), BF16),
        compiler_params=pltpu.CompilerParams(
            dimension_semantics=("arbitrary", "arbitrary"), vmem_limit_bytes=VMEM_LIMIT),
        scratch_shapes=[pltpu.VMEM((3, KWIN, QB), F32)],
        name="window_attn",
    )(sink, qT, k, vT, k, vT)


def _rope_tables():
    half = ATT_DH // 2
    inv_freq = np.power(np.float32(ROPE_BASE), -np.arange(0, half, 2, dtype=np.float32) / np.float32(half))
    inv_freq = inv_freq.astype(np.float32)
    pos = np.arange(T)
    ang_r = (pos // GRID_W).astype(np.float32)[:, None] * inv_freq[None, :]
    ang_c = (pos % GRID_W).astype(np.float32)[:, None] * inv_freq[None, :]
    ang = np.concatenate([ang_r, ang_r, ang_c, ang_c], axis=1)
    ang = np.concatenate([ang, np.zeros((TM, ATT_DH), np.float32)], axis=0)
    ang = np.concatenate([ang, ang], axis=1)
    return np.cos(ang).astype(np.float32), np.sin(ang).astype(np.float32)


def kernel(x, c, ctx, c_ctx, w_mod, b_mod, ln_g, ln_b, mlp_w1, mlp_w2, ev_w_in, ev_ret_theta, ev_gla_gk_w,
           ev_gla_gk_b, ev_gla_norm_g, ev_w_out, od_w_qkv, od_sink, od_w_out):
    x2 = x.reshape(LAT_ROWS, D)
    ctx2 = ctx.reshape(B * L, D)

    cs = jnp.concatenate([c, c_ctx[None, :], jnp.zeros((8 - B - 1, D), F32)], axis=0)
    mods = _modulation(cs, w_mod, b_mod).reshape(DEPTH * 8, 1, 6 * D)
    ln = (ln_g.reshape(2 * DEPTH, D), ln_b.reshape(2 * DEPTH, D))

    gk_w = ev_gla_gk_w[0]
    zeros = jnp.zeros((GATE_RANK, QK_W), F32)
    g2 = jnp.concatenate([jnp.concatenate([gk_w[0], zeros], axis=1),
                          jnp.concatenate([zeros, gk_w[1]], axis=1)], axis=0)
    qa, kaT, va, ga, qb, kbT, vb, gbv, lfT, lbT = _in_proj0(
        x2, ctx2, mods, ev_w_in[0].T, g2.T.astype(BF16), ev_gla_gk_b[0].reshape(2 * QK_W, 1))

    w1_rows = DEPTH * D // SCAN_STEPS
    w2_rows = DEPTH * D_FF // SCAN_STEPS
    y_ret, yc_ret, w1 = _scan_group(qa, kaT, va, ga, (ev_ret_theta,),
                                    mlp_w1.reshape(SCAN_STEPS, w1_rows, D_FF), gla=False)
    y_gla, yc_gla, w2 = _scan_group(qb, kbT, vb, gbv, (lfT, lbT, ev_gla_norm_g),
                                    mlp_w2.reshape(SCAN_STEPS, w2_rows, D), gla=True)
    w1 = w1.reshape(DEPTH, D, D_FF)
    w2 = w2.reshape(DEPTH, D_FF, D)

    xs = _out_mlp((y_ret, y_gla, yc_ret, yc_gla), x2, ctx2, mods, ln, ev_w_out, w1, w2,
                  layer=0, n_tiles=N_TILES, split_ctx=True)

    cos_tab, sin_tab = _rope_tables()
    q1T, k1, v1T = _in_proj1(xs, mods, cos_tab, sin_tab, od_w_qkv)
    att = _attention(od_sink, q1T, k1, v1T)
    out = _out_mlp((att,), xs, None, mods, ln, od_w_out, w1, w2,
                   layer=1, n_tiles=N_LAT_TILES, split_ctx=False)
    return out.reshape(B, T, D)
```

```python
import functools

import jax
import jax.numpy as jnp
import numpy as np
from jax import lax
from jax.experimental import pallas as pl
from jax.experimental.pallas import tpu as pltpu

F32 = jnp.float32
BF16 = jnp.bfloat16

D = 1024
B = 2
T = 8192
L = 256
DEPTH = 2
GRID_W = 64
D_FF = 4 * D
HEAD_DV = 128
HEAD_DK = 64
GATE_RANK = 16
GATE_TAU = 16.0
QK_W = 256
V_W = 512
ATT_DH = 64
ATT_QH = 16
ATT_KVH = 4
ATT_GROUP = 4
ATT_KVW = ATT_KVH * ATT_DH
WINDOW = 128
ROPE_BASE = 10000.0
ALPHA = (2.0 * DEPTH) ** 0.25
LN_EPS = 1e-5
RMS_EPS = 1e-6

TM = 512
LAT_ROWS = B * T
ROWS = LAT_ROWS + B * L
N_LAT_TILES = LAT_ROWS // TM
N_TILES = ROWS // TM
TILES_PER_BATCH = T // TM

CH = 64
SP = 2 * CH
TB = 2048
NP = TB // SP
NP_CTX = L // SP
NBLK = T // TB
NP_ALL = NP_CTX + T // SP
N_SLABS = ROWS // SP
KV_UNROLL = 16
OUT_UNROLL = 16

TQ = 1024
QB = 128
KWIN = 3 * QB
LOG2E = 1.4426950408889634

VMEM_LIMIT = 56 * 1024 * 1024


def _dot(a, b):
    return jnp.dot(a, b, preferred_element_type=F32)


def _dot_nt(a, b):
    return lax.dot_general(a, b, (((1,), (1,)), ((), ())), preferred_element_type=F32)


def _full_spec(shape):
    nd = len(shape)
    return pl.BlockSpec(shape, lambda *_: (0,) * nd, pipeline_mode=pl.Buffered(1))


def _mod_row(t):
    return jnp.minimum(t // TILES_PER_BATCH, B)


def _mod_spec(layer):
    return pl.BlockSpec((None, 1, 6 * D), lambda t: (layer * 8 + _mod_row(t), 0, 0))


def _layer_norm(x, g, b):
    mu = jnp.mean(x, axis=-1, keepdims=True)
    xc = x - mu
    var = jnp.mean(xc * xc, axis=-1, keepdims=True)
    return xc * lax.rsqrt(var + LN_EPS) * g + b


def _log_sigmoid(z):
    return jnp.minimum(z, 0.0) - jnp.log1p(jnp.exp(-jnp.abs(z)))


MOD_TN = 1536


def _mod_kernel(c_ref, w_ref, b_ref, o_ref):
    s = jax.nn.silu(c_ref[...])
    s_hi = s.astype(BF16)
    s_lo = (s - s_hi.astype(F32)).astype(BF16)
    w = w_ref[...].astype(BF16)
    o_ref[...] = _dot(s_hi, w) + _dot(s_lo, w) + b_ref[...]


def _modulation(cs, w_mod, b_mod):
    return pl.pallas_call(
        _mod_kernel,
        grid=(DEPTH, 6 * D // MOD_TN),
        in_specs=[
            pl.BlockSpec((8, D), lambda i, n: (0, 0)),
            pl.BlockSpec((None, D, MOD_TN), lambda i, n: (i, 0, n)),
            pl.BlockSpec((None, 1, MOD_TN), lambda i, n: (i, 0, n)),
        ],
        out_specs=pl.BlockSpec((None, 8, MOD_TN), lambda i, n: (i, 0, n)),
        out_shape=jax.ShapeDtypeStruct((DEPTH, 8, 6 * D), F32),
        compiler_params=pltpu.CompilerParams(
            dimension_semantics=("arbitrary", "arbitrary"), vmem_limit_bytes=VMEM_LIMIT),
        name="modulation",
    )(cs, w_mod, b_mod.reshape(DEPTH, 1, 6 * D))


IN0_OFF = (0, 256, 512, 1024, 1536, 1792, 2048, 2560, 3072, 3104)


def _transpose_bf16(w):
    return w.astype(F32).T.astype(BF16)


def _in0_kernel(x_ref, ctx_ref, mod_ref, wT_ref, g2T, gbc,
                qa_o, kaT_o, va_o, ga_o, qb_o, kbT_o, vb_o, gb_o, lfT_o, lbT_o, wbT_s):
    t = pl.program_id(0)
    rows = lambda i: slice(IN0_OFF[i], IN0_OFF[i + 1])
    piece = lambda i: wbT_s[rows(i), :]

    @pl.when(t == 0)
    def _():
        for i in range(len(IN0_OFF) - 1):
            wbT_s[rows(i), :] = wT_ref[rows(i), :].astype(BF16)

    xt = jnp.where(t == N_LAT_TILES, ctx_ref[...], x_ref[...])
    sh1 = mod_ref[:, 0:D]
    sc1 = mod_ref[:, D:2 * D]
    h = (xt * (1.0 + sc1) + sh1).astype(BF16)
    qk_scale = HEAD_DK ** -0.5

    def put_slabs(o_ref, val):
        for i in range(TM // SP):
            o_ref[i] = val[:, i * SP:(i + 1) * SP].astype(o_ref.dtype)

    lr = _dot_nt(h, piece(8)).astype(BF16)
    lsT = _log_sigmoid(_dot_nt(g2T[...], lr) + gbc[...]) * (1.0 / GATE_TAU)
    put_slabs(lfT_o, lsT[0:QK_W, :])
    put_slabs(lbT_o, lsT[QK_W:2 * QK_W, :])
    ga_o[...] = jax.nn.silu(_dot_nt(h, piece(3))).astype(BF16)
    gb_o[...] = jax.nn.silu(_dot_nt(h, piece(7))).astype(BF16)

    put_slabs(kaT_o, _dot_nt(piece(1), h) * qk_scale)
    put_slabs(kbT_o, _dot_nt(piece(5), h))
    qa_o[...] = _dot_nt(h, piece(0)).astype(BF16)
    qb_o[...] = (_dot_nt(h, piece(4)) * qk_scale).astype(BF16)
    va_o[...] = _dot_nt(h, piece(2)).astype(BF16)
    vb_o[...] = _dot_nt(h, piece(6)).astype(BF16)


def _in_proj0(x2, ctx2, mods, w_inT, g2T, gbc):
    row = lambda t: (t, 0)
    slab = lambda t: (t, 0, 0)
    nsl = TM // SP
    row_out = lambda width: (jax.ShapeDtypeStruct((ROWS, width), BF16), pl.BlockSpec((TM, width), row))
    slab_out = lambda dt: (jax.ShapeDtypeStruct((N_SLABS, QK_W, SP), dt), pl.BlockSpec((nsl, QK_W, SP), slab))
    outs = [row_out(QK_W), slab_out(BF16), row_out(V_W), row_out(V_W),
            row_out(QK_W), slab_out(BF16), row_out(V_W), row_out(V_W),
            slab_out(BF16), slab_out(BF16)]
    in_specs = [
        pl.BlockSpec((TM, D), lambda t: (jnp.minimum(t, N_LAT_TILES - 1), 0)),
        _full_spec((B * L, D)),
        _mod_spec(0),
        _full_spec(w_inT.shape), _full_spec(g2T.shape), _full_spec(gbc.shape),
    ]
    return pl.pallas_call(
        _in0_kernel,
        grid=(N_TILES,),
        in_specs=in_specs,
        out_specs=[o[1] for o in outs],
        out_shape=[o[0] for o in outs],
        scratch_shapes=[pltpu.VMEM(w_inT.shape, BF16)],
        compiler_params=pltpu.CompilerParams(
            dimension_semantics=("arbitrary",), vmem_limit_bytes=VMEM_LIMIT),
        name="in_proj0",
    )(x2, ctx2, mods, w_inT, g2T, gbc)


def _chunk_diag(kv, c):
    r0 = c * 2 * HEAD_DK
    return jnp.concatenate([kv[r0:r0 + HEAD_DK, 0:HEAD_DV],
                            kv[r0 + HEAD_DK:r0 + 2 * HEAD_DK, HEAD_DV:2 * HEAD_DV]], axis=0)


def _scan_kernel(*refs, gla):
    if gla:
        (q_ref, kT_ref, v_ref, g_ref, lfT_ref, lbT_ref,
         qc_ref, kTc_ref, vc_ref, gc_ref, lfTc_ref, lbTc_ref, ng_ref, wsrc_ref,
         y_ref, yc_ref, wdst_ref, s_ref, r_ref, rst_ref, sst_ref, kv_ref, dec_ref,
         kd_ref, lhs_ref, pm_ref) = refs
        lat = (q_ref, kT_ref, v_ref, g_ref, lfT_ref, lbT_ref)
        cxt = (qc_ref, kTc_ref, vc_ref, gc_ref, lfTc_ref, lbTc_ref)
    else:
        (q_ref, kT_ref, v_ref, g_ref, qc_ref, kTc_ref, vc_ref, gc_ref, th_ref, wsrc_ref,
         y_ref, yc_ref, wdst_ref, s_ref, r_ref, rst_ref, sst_ref, kv_ref, dec_ref,
         kd_ref, lhs_ref, pm_ref) = refs
        lat = (q_ref, kT_ref, v_ref, g_ref, None, None)
        cxt = (qc_ref, kTc_ref, vc_ref, gc_ref, None, None)

    phase = pl.program_id(2)
    j = pl.program_id(3)

    wdst_ref[...] = wsrc_ref[...].astype(BF16)

    ri = lax.broadcasted_iota(jnp.int32, (SP, SP), 0)
    ci = lax.broadcasted_iota(jnp.int32, (SP, SP), 1)
    same = (ri // CH) == (ci // CH)
    first_lane = ci < CH
    head_a = ci < HEAD_DK

    if gla:
        as_w = lambda m: m.astype(BF16)
        tot = jnp.concatenate([jnp.broadcast_to(ri < CH, (SP, SP)), jnp.broadcast_to(ri >= CH, (SP, SP))], axis=1)
        w_end_f = jnp.concatenate([as_w(same & (ri > ci)), as_w(tot)], axis=1)
        w_end_b = jnp.concatenate([as_w(same & (ri < ci)), as_w(tot)], axis=1)
        w_cum_f = as_w(same & (ri <= ci))
        w_cum_b = as_w(same & (ri >= ci))
    else:
        hp = pl.program_id(1)
        th = [[th_ref[0, dr, 2 * hp + hd] for hd in range(2)] for dr in range(2)]
        lane1 = lax.broadcasted_iota(jnp.int32, (1, SP), 1) < HEAD_DK
        row1 = lax.broadcasted_iota(jnp.int32, (SP, 1), 0) < HEAD_DK
        lg_row = [jnp.log1p(-jnp.exp(jnp.where(lane1, th[dr][0], th[dr][1]))) for dr in range(2)]
        lg_col = [jnp.log1p(-jnp.exp(jnp.where(row1, th[dr][0], th[dr][1]))) for dr in range(2)]
        it = (ci % CH).astype(F32)
        ir = (ri % CH).astype(F32)
        ret_end_f = jnp.exp((CH - 1.0 - it) * lg_col[0])
        ret_end_b = jnp.exp(it * lg_col[1])
        ret_dec_f = jnp.exp(jnp.broadcast_to(CH * lg_col[0], (SP, SP)))
        ret_dec_b = jnp.exp(jnp.broadcast_to(CH * lg_col[1], (SP, SP)))
        ret_ebTi = jnp.exp(-(it + 1.0) * lg_col[0])
        ret_erTi = jnp.exp(-(CH - it) * lg_col[1])
        ret_eb = jnp.exp((ir + 1.0) * lg_row[0])
        ret_er = jnp.exp((CH - ir) * lg_row[1])

    def kv_stage(blk, n, fwd):
        _, kT_r, v_r, _, lfT_r, lbT_r = blk

        def body(p, carry):
            kT = kT_r[p].astype(F32)
            v = v_r[pl.ds(pl.multiple_of(p * SP, SP), SP), :]
            if gla:
                res = _dot((lfT_r if fwd else lbT_r)[p], w_end_f if fwd else w_end_b)
                e_end = jnp.exp(res[:, 0:SP])
                dec0 = jnp.exp(res[:, SP:2 * SP])
                dec1 = jnp.exp(res[:, 2 * SP:3 * SP])
            else:
                e_end = ret_end_f if fwd else ret_end_b
                dec0 = dec1 = ret_dec_f if fwd else ret_dec_b
            ke = kT * e_end
            lhs = jnp.concatenate([jnp.where(first_lane, ke, 0.0), jnp.where(first_lane, 0.0, ke)],
                                  axis=0).astype(BF16)
            kv = _dot(lhs, v)
            kv_ref[2 * p] = _chunk_diag(kv, 0)
            kv_ref[2 * p + 1] = _chunk_diag(kv, 1)
            dec_ref[2 * p] = dec0
            dec_ref[2 * p + 1] = dec1
            return carry

        lax.fori_loop(0, n, body, 0, unroll=min(n, KV_UNROLL))

    def phase0_block(blk, n, slot0):
        kv_stage(blk, n, fwd=False)

        def body(i, r_state):
            p = n - 1 - i
            rst_ref[slot0 + p, :, HEAD_DV:2 * HEAD_DV] = r_state.astype(BF16)
            r_state = dec_ref[2 * p + 1] * r_state + kv_ref[2 * p + 1]
            rst_ref[slot0 + p, :, 0:HEAD_DV] = r_state.astype(BF16)
            return dec_ref[2 * p] * r_state + kv_ref[2 * p]

        r_ref[...] = lax.fori_loop(0, n, body, r_ref[...])

    def phase1_block(blk, n, slot0, out_ref):
        q_r, kT_r, v_r, g_r, lfT_r, lbT_r = blk
        kv_stage(blk, n, fwd=True)

        def rec(p, s_state):
            sst_ref[p, :, 0:HEAD_DV] = s_state.astype(BF16)
            s_state = dec_ref[2 * p] * s_state + kv_ref[2 * p]
            sst_ref[p, :, HEAD_DV:2 * HEAD_DV] = s_state.astype(BF16)
            return dec_ref[2 * p + 1] * s_state + kv_ref[2 * p + 1]

        s_ref[...] = lax.fori_loop(0, n, rec, s_ref[...])

        r2 = lax.broadcasted_iota(jnp.int32, (2 * SP, SP), 0) % SP
        c2 = lax.broadcasted_iota(jnp.int32, (2 * SP, SP), 1)
        same2 = (r2 // CH) == (c2 // CH)
        mask_f = same2 & (r2 >= c2)
        mask_b = same2 & (r2 < c2)

        def prep(p, carry):
            rows = pl.ds(pl.multiple_of(p * SP, SP), SP)
            q = q_r[rows, :].astype(F32)
            kT = kT_r[p].astype(F32)
            if gla:
                lfT = lfT_r[p]
                lbT = lbT_r[p]
                ebTi = jnp.exp(-_dot(lfT, w_cum_f))
                erTi = jnp.exp(-_dot(lbT, w_cum_b))
                e_b = jnp.exp(_dot_nt(w_cum_b, lfT))
                e_r = jnp.exp(_dot_nt(w_cum_f, lbT))
            else:
                ebTi, erTi, e_b, e_r = ret_ebTi, ret_erTi, ret_eb, ret_er
            kd_ref[p, 0:SP, 0:SP] = (kT * ebTi).astype(BF16)
            kd_ref[p, SP:2 * SP, SP:2 * SP] = (kT * erTi).astype(BF16)
            qf = q * e_b
            qb = q * e_r
            lhs_ref[p, :, 0:SP] = jnp.concatenate(
                [jnp.where(head_a, qf, 0.0), jnp.where(head_a, 0.0, qf)], axis=0).astype(BF16)
            lhs_ref[p, :, SP:2 * SP] = jnp.concatenate(
                [jnp.where(head_a, qb, 0.0), jnp.where(head_a, 0.0, qb)], axis=0).astype(BF16)
            return carry

        def score(p, carry):
            sc = _dot(lhs_ref[p], kd_ref[p])
            pm_ref[p] = jnp.where(mask_f, sc[:, 0:SP], jnp.where(mask_b, sc[:, SP:2 * SP], 0.0)).astype(BF16)
            return carry

        def emit(p, carry):
            rows = pl.ds(pl.multiple_of(p * SP, SP), SP)
            v = v_r[rows, :]
            states = jnp.concatenate([sst_ref[p], rst_ref[slot0 + p]], axis=0)
            o_int = _dot(lhs_ref[p], states)
            o_a = _dot(pm_ref[p, 0:SP, :], v[:, 0:HEAD_DV]) + jnp.concatenate(
                [o_int[0:CH, 0:HEAD_DV], o_int[CH:SP, HEAD_DV:2 * HEAD_DV]], axis=0)
            o_b = _dot(pm_ref[p, SP:2 * SP, :], v[:, HEAD_DV:2 * HEAD_DV]) + jnp.concatenate(
                [o_int[SP:SP + CH, 0:HEAD_DV], o_int[SP + CH:2 * SP, HEAD_DV:2 * HEAD_DV]], axis=0)

            def nrm(o):
                y = o * lax.rsqrt(jnp.mean(o * o, axis=-1, keepdims=True) + RMS_EPS)
                return y * ng_ref[...] if gla else y

            y = jnp.concatenate([nrm(o_a), nrm(o_b)], axis=1) * g_r[rows, :].astype(F32)
            out_ref[rows, :] = y.astype(BF16)
            return carry

        lax.fori_loop(0, n, prep, 0, unroll=min(n, OUT_UNROLL))
        lax.fori_loop(0, n, score, 0, unroll=min(n, OUT_UNROLL))
        lax.fori_loop(0, n, emit, 0, unroll=min(n, OUT_UNROLL))

    @pl.when((pl.program_id(0) == 0) & (pl.program_id(1) == 0) & (phase == 0) & (j == 0))
    def _():
        kd_ref[...] = jnp.zeros_like(kd_ref)

    @pl.when(phase == 0)
    def _():
        @pl.when(j == 0)
        def _():
            r_ref[...] = jnp.zeros_like(r_ref)
            phase0_block(cxt, NP_CTX, 0)

        phase0_block(lat, NP, NP_CTX + (NBLK - 1 - j) * NP)

    @pl.when(phase == 1)
    def _():
        @pl.when(j == 0)
        def _():
            s_ref[...] = jnp.zeros_like(s_ref)
            phase1_block(cxt, NP_CTX, 0, yc_ref)

        phase1_block(lat, NP, NP_CTX + j * NP, y_ref)


SCAN_STEPS = B * 2 * 2 * NBLK


def _scan_group(q, kT, v, g, extra, wsrc, *, gla):
    w_spec = pl.BlockSpec((None,) + wsrc.shape[1:],
                          lambda b, p, ph, j: (((b * 2 + p) * 2 + ph) * NBLK + j, 0, 0))

    def blk(b, ph, j, used_in_phase0):
        jj = jnp.where(ph == 0, NBLK - 1 - j, j)
        if not used_in_phase0:
            jj = jnp.where(ph == 0, 0, jj)
        return b * NBLK + jj

    def lat_specs(used0):
        return dict(
            row=lambda w: pl.BlockSpec((TB, w), lambda b, p, ph, j: (blk(b, ph, j, used0), p)),
            slab=pl.BlockSpec((NP, SP, SP), lambda b, p, ph, j: (blk(b, ph, j, used0), p, 0)))

    ctx_row = lambda w: pl.BlockSpec((L, w), lambda b, p, ph, j: (LAT_ROWS // L + b, p))
    ctx_slab = pl.BlockSpec((NP_CTX, SP, SP), lambda b, p, ph, j: (LAT_ROWS // L + b, p, 0))
    used, unused = lat_specs(True), lat_specs(False)

    in_specs = [unused["row"](2 * HEAD_DK), used["slab"], used["row"](2 * HEAD_DV), unused["row"](2 * HEAD_DV)]
    ctx_specs = [ctx_row(2 * HEAD_DK), ctx_slab, ctx_row(2 * HEAD_DV), ctx_row(2 * HEAD_DV)]
    if gla:
        lfT, lbT, ng = extra
        in_specs += [unused["slab"], used["slab"]]
        ctx_specs += [ctx_slab, ctx_slab]
        args = (q, kT, v, g, lfT, lbT, q, kT, v, g, lfT, lbT, ng, wsrc)
        in_specs = in_specs + ctx_specs + [pl.BlockSpec((1, HEAD_DV), lambda b, p, ph, j: (0, 0)), w_spec]
    else:
        (theta,) = extra
        args = (q, kT, v, g, q, kT, v, g, theta, wsrc)
        in_specs = in_specs + ctx_specs + [pl.BlockSpec(memory_space=pltpu.SMEM), w_spec]
    return pl.pallas_call(
        functools.partial(_scan_kernel, gla=gla),
        grid=(B, 2, 2, NBLK),
        in_specs=in_specs,
        out_specs=[
            pl.BlockSpec((TB, 2 * HEAD_DV), lambda b, p, ph, j: (b * NBLK + jnp.where(ph == 0, 0, j), p)),
            pl.BlockSpec((L, 2 * HEAD_DV), lambda b, p, ph, j: (b, p)),
            w_spec,
        ],
        out_shape=[jax.ShapeDtypeStruct((LAT_ROWS, V_W), BF16), jax.ShapeDtypeStruct((B * L, V_W), BF16),
                   jax.ShapeDtypeStruct(wsrc.shape, BF16)],
        scratch_shapes=[
            pltpu.VMEM((SP, HEAD_DV), F32),
            pltpu.VMEM((SP, HEAD_DV), F32),
            pltpu.VMEM((NP_ALL, SP, 2 * HEAD_DV), BF16),
            pltpu.VMEM((NP, SP, 2 * HEAD_DV), BF16),
            pltpu.VMEM((2 * NP, SP, HEAD_DV), F32),
            pltpu.VMEM((2 * NP, SP, HEAD_DV), F32),
            pltpu.VMEM((NP, 2 * SP, 2 * SP), BF16),
            pltpu.VMEM((NP, 2 * SP, 2 * SP), BF16),
            pltpu.VMEM((NP, 2 * SP, SP), BF16),
        ],
        compiler_params=pltpu.CompilerParams(
            dimension_semantics=("arbitrary",) * 4, vmem_limit_bytes=VMEM_LIMIT),
        name="scan_gla" if gla else "scan_ret",
    )(*args)


FF_CH = 512


def _out_kernel(*refs, layer, split_ctx):
    if split_ctx:
        (ya_ref, yb_ref, yac_ref, ybc_ref, x_ref, ctx_ref, mod_ref, lng_ref, lnb_ref,
         wo_ref, w1_ref, w2_ref, o_ref, wo_s) = refs
        is_ctx = pl.program_id(0) == N_LAT_TILES
        x = jnp.where(is_ctx, ctx_ref[...], x_ref[...])
        ya = jnp.where(is_ctx, yac_ref[...], ya_ref[...])
        yb = jnp.where(is_ctx, ybc_ref[...], yb_ref[...])
    else:
        ya_ref, yb_ref, x_ref, mod_ref, lng_ref, lnb_ref, wo_ref, w1_ref, w2_ref, o_ref, wo_s = refs
        x = x_ref[...]
        ya = ya_ref[...]
        yb = yb_ref[...]
    g1 = mod_ref[:, 2 * D:3 * D]
    sh2 = mod_ref[:, 3 * D:4 * D]
    sc2 = mod_ref[:, 4 * D:5 * D]
    g2 = mod_ref[:, 5 * D:6 * D]
    ln_g0 = lng_ref[2 * layer:2 * layer + 1, :]
    ln_g1 = lng_ref[2 * layer + 1:2 * layer + 2, :]
    ln_b0 = lnb_ref[2 * layer:2 * layer + 1, :]
    ln_b1 = lnb_ref[2 * layer + 1:2 * layer + 2, :]

    @pl.when(pl.program_id(0) == 0)
    def _():
        wo_s[...] = wo_ref[...].astype(BF16)

    half = D // 2
    y = _dot(ya, wo_s[0:half, :]) + _dot(yb, wo_s[half:D, :])
    x1 = _layer_norm(ALPHA * x + g1 * y, ln_g0, ln_b0)
    h2 = (x1 * (1.0 + sc2) + sh2).astype(BF16)
    acc = jnp.zeros((TM, D), F32)
    for c in range(D_FF // FF_CH):
        cols = slice(c * FF_CH, (c + 1) * FF_CH)
        hc = jnp.maximum(_dot(h2, w1_ref[:, cols]), 0.0)
        acc = acc + _dot((hc * hc).astype(BF16), w2_ref[cols, :])
    o_ref[...] = _layer_norm(ALPHA * x1 + g2 * acc, ln_g1, ln_b1)


def _out_mlp(ys, xs, ctx2, mods, ln, wo, w1, w2, *, layer, n_tiles, split_ctx):
    half = D // 2
    lat_row = lambda t: (jnp.minimum(t, N_LAT_TILES - 1), 0)
    if split_ctx:
        ya, yb, yac, ybc = ys
        in_specs = [pl.BlockSpec((TM, half), lat_row), pl.BlockSpec((TM, half), lat_row),
                    _full_spec((B * L, half)), _full_spec((B * L, half)),
                    pl.BlockSpec((TM, D), lat_row), _full_spec((B * L, D))]
        args = [ya, yb, yac, ybc, xs, ctx2]
    else:
        (att,) = ys
        in_specs = [pl.BlockSpec((TM, half), lambda t: (t, 0)), pl.BlockSpec((TM, half), lambda t: (t, 1)),
                    pl.BlockSpec((TM, D), lambda t: (t, 0))]
        args = [att, att, xs]
    stacked = lambda w, i: pl.BlockSpec((None,) + w.shape[1:], lambda t: (i, 0, 0), pipeline_mode=pl.Buffered(1))
    in_specs += [_mod_spec(layer), _full_spec(ln[0].shape), _full_spec(ln[1].shape),
                 stacked(wo, 0), stacked(w1, layer), stacked(w2, layer)]
    args += [mods, ln[0], ln[1], wo, w1, w2]
    return pl.pallas_call(
        functools.partial(_out_kernel, layer=layer, split_ctx=split_ctx),
        grid=(n_tiles,),
        in_specs=in_specs,
        out_specs=pl.BlockSpec((TM, D), lambda t: (t, 0)),
        out_shape=jax.ShapeDtypeStruct((n_tiles * TM, D), F32),
        compiler_params=pltpu.CompilerParams(
            dimension_semantics=("arbitrary",), vmem_limit_bytes=VMEM_LIMIT),
        scratch_shapes=[pltpu.VMEM((D, D), BF16)],
        name="out_mlp%d" % layer,
    )(*args)


def _in1_kernel(x_ref, mod_ref, cos_ref, sin_ref, w_ref, qT_o, k_o, vT_o, wqT_s, wk_s, wvT_s):
    @pl.when(pl.program_id(0) == 0)
    def _():
        wqT_s[...] = _transpose_bf16(w_ref[:, 0:D])
        wk_s[...] = w_ref[:, D:D + ATT_KVW].astype(BF16)
        wvT_s[...] = _transpose_bf16(w_ref[:, D + ATT_KVW:D + 2 * ATT_KVW])

    sh1 = mod_ref[:, 0:D]
    sc1 = mod_ref[:, D:2 * D]
    h = (x_ref[...] * (1.0 + sc1) + sh1).astype(BF16)
    nsl = TM // SP

    cos = cos_ref[...]
    sin = sin_ref[...]
    first = (lax.broadcasted_iota(jnp.int32, (TM, 128), 1) % 32) < 16
    sa = jnp.where(first, -sin, 0.0)
    sb = jnp.where(first, 0.0, sin)

    cosT, saT, sbT = cos.T, sa.T, sb.T
    qT = _dot_nt(wqT_s[...], h) * (LOG2E * ATT_DH ** -0.5)
    for i in range(D // 128):
        u = qT[i * 128:(i + 1) * 128, :]
        r = (u * cosT + pltpu.roll(u, 128 - 16, 0) * saT + pltpu.roll(u, 16, 0) * sbT).astype(BF16)
        for s in range(nsl):
            qT_o[s, i * 128:(i + 1) * 128, :] = r[:, s * SP:(s + 1) * SP]

    k = _dot(h, wk_s[...])
    for i in range(ATT_KVW // 128):
        u = k[:, i * 128:(i + 1) * 128]
        r = u * cos + pltpu.roll(u, 128 - 16, 1) * sa + pltpu.roll(u, 16, 1) * sb
        k_o[:, i * 128:(i + 1) * 128] = r.astype(BF16)

    vT = _dot_nt(wvT_s[...], h).astype(BF16)
    for s in range(nsl):
        vT_o[s] = vT[:, s * SP:(s + 1) * SP]


def _in_proj1(xs, mods, cos_tab, sin_tab, wqkv):
    tile = lambda t: jnp.where(t == N_LAT_TILES, TILES_PER_BATCH, t % TILES_PER_BATCH)
    row = lambda t: (t, 0)
    slab = lambda t: (t, 0, 0)
    nsl = TM // SP
    tab_spec = pl.BlockSpec((TM, 128), lambda t: (tile(t), 0))
    return pl.pallas_call(
        _in1_kernel,
        grid=(N_TILES,),
        in_specs=[pl.BlockSpec((TM, D), row), _mod_spec(1), tab_spec, tab_spec,
                  pl.BlockSpec((None,) + wqkv.shape[1:], lambda t: (0, 0, 0), pipeline_mode=pl.Buffered(1))],
        out_specs=[pl.BlockSpec((nsl, D, SP), slab), pl.BlockSpec((TM, ATT_KVW), row),
                   pl.BlockSpec((nsl, ATT_KVW, SP), slab)],
        out_shape=[jax.ShapeDtypeStruct((N_SLABS, D, SP), BF16), jax.ShapeDtypeStruct((ROWS, ATT_KVW), BF16),
                   jax.ShapeDtypeStruct((N_SLABS, ATT_KVW, SP), BF16)],
        scratch_shapes=[pltpu.VMEM((D, D), BF16), pltpu.VMEM((D, ATT_KVW), BF16), pltpu.VMEM((ATT_KVW, D), BF16)],
        compiler_params=pltpu.CompilerParams(
            dimension_semantics=("arbitrary",), vmem_limit_bytes=VMEM_LIMIT),
        name="in_proj1",
    )(xs, mods, cos_tab, sin_tab, wqkv)


def _attn_kernel(sink_ref, qT_ref, k_ref, vT_ref, kc_ref, vcT_ref, o_ref, bias_ref):
    n = pl.program_id(1)
    nql = ATT_GROUP * QB
    lane_g = lax.broadcasted_iota(jnp.int32, (1, nql), 1) // QB
    zero_half = jnp.zeros((ATT_DH, nql), BF16)
    ones_rows = jnp.ones((16, KWIN + L), BF16)

    @pl.when((pl.program_id(0) == 0) & (n == 0))
    def _():
        kj = lax.broadcasted_iota(jnp.int32, (KWIN, QB), 0)
        qi = lax.broadcasted_iota(jnp.int32, (KWIN, QB), 1)
        for case, delta in enumerate((-QB, 0, -2 * QB)):
            d = kj - qi + delta
            bias_ref[case] = jnp.where((d >= -WINDOW) & (d <= WINDOW), 0.0, -jnp.inf)

    def scores(i, kh):
        n0 = (n * (TQ // QB) + i) * QB
        start = pl.multiple_of(jnp.clip(n0 - QB, 0, T - KWIN), QB)
        bias1 = bias_ref[jnp.where(n0 == 0, 1, jnp.where(n0 == T - QB, 2, 0))]
        bias = jnp.concatenate([bias1] * ATT_GROUP, axis=1)
        pair = slice((kh // 2) * 128, (kh // 2 + 1) * 128)
        qT = jnp.concatenate(
            [qT_ref[i, (kh * ATT_GROUP + g) * ATT_DH:(kh * ATT_GROUP + g + 1) * ATT_DH, :]
             for g in range(ATT_GROUP)], axis=1)
        q_pad = jnp.concatenate([zero_half, qT] if kh % 2 else [qT, zero_half], axis=0)
        s_loc = _dot(k_ref[pl.ds(start, KWIN), pair], q_pad) + bias
        s_ctx = _dot(kc_ref[:, pair], q_pad)
        return s_loc, s_ctx, start // SP

    def finish(i, kh, s_loc, s_ctx, slab0):
        drows = slice(kh * ATT_DH, (kh + 1) * ATT_DH)
        sink = jnp.zeros((1, nql), F32)
        for g in range(ATT_GROUP):
            sink = jnp.where(lane_g == g, sink_ref[0, kh * ATT_GROUP + g] * LOG2E, sink)
        m = jnp.maximum(jnp.maximum(jnp.max(s_loc, axis=0, keepdims=True),
                                    jnp.max(s_ctx, axis=0, keepdims=True)), sink)
        pT = jnp.concatenate([jnp.exp2(s_loc - m).astype(BF16),
                              jnp.exp2(s_ctx - m).astype(BF16)], axis=0)
        vT = jnp.concatenate([vT_ref[slab0 + t, drows, :] for t in range(KWIN // SP)]
                             + [vcT_ref[t, drows, :] for t in range(L // SP)], axis=1)
        o_ext = _dot(jnp.concatenate([vT, ones_rows], axis=0), pT)
        denom = o_ext[ATT_DH:ATT_DH + 1, :] + jnp.exp2(sink - m)
        oT = o_ext[0:ATT_DH, :] / denom
        for g in range(0, ATT_GROUP, 2):
            two = jnp.concatenate([oT[:, g * QB:(g + 1) * QB], oT[:, (g + 1) * QB:(g + 2) * QB]], axis=0)
            c0 = (kh * ATT_GROUP + g) * ATT_DH
            o_ref[i * QB:(i + 1) * QB, c0:c0 + 2 * ATT_DH] = two.T.astype(BF16)

    items = [(i, kh) for i in range(TQ // QB) for kh in range(ATT_KVH)]
    pending = scores(*items[0])
    for t, item in enumerate(items):
        nxt = scores(*items[t + 1]) if t + 1 < len(items) else None
        finish(*item, *pending)
        pending = nxt


def _attention(sink, qT, k, vT):
    nq = T // TQ
    return pl.pallas_call(
        _attn_kernel,
        grid=(B, nq),
        in_specs=[
            pl.BlockSpec(memory_space=pltpu.SMEM),
            pl.BlockSpec((TQ // SP, D, SP), lambda b, n: (b * nq + n, 0, 0)),
            pl.BlockSpec((T, ATT_KVW), lambda b, n: (b, 0)),
            pl.BlockSpec((T // SP, ATT_KVW, SP), lambda b, n: (b, 0, 0)),
            pl.BlockSpec((L, ATT_KVW), lambda b, n: (LAT_ROWS // L + b, 0)),
            pl.BlockSpec((L // SP, ATT_KVW, SP), lambda b, n: (LAT_ROWS // L + b, 0, 0)),
        ],
        out_specs=pl.BlockSpec((TQ, D), lambda b, n: (b * nq + n, 0)),
        out_shape=jax.ShapeDtypeStruct((LAT_ROWS, D), BF16),
        compiler_params=pltpu.CompilerParams(
            dimension_semantics=("arbitrary", "arbitrary"), vmem_limit_bytes=VMEM_LIMIT),
        scratch_shapes=[pltpu.VMEM((3, KWIN, QB), F32)],
        name="window_attn",
    )(sink, qT, k, vT, k, vT)


def _rope_tables():
    half = ATT_DH // 2
    inv_freq = np.power(np.float32(ROPE_BASE), -np.arange(0, half, 2, dtype=np.float32) / np.float32(half))
    inv_freq = inv_freq.astype(np.float32)
    pos = np.arange(T)
    ang_r = (pos // GRID_W).astype(np.float32)[:, None] * inv_freq[None, :]
    ang_c = (pos % GRID_W).astype(np.float32)[:, None] * inv_freq[None, :]
    ang = np.concatenate([ang_r, ang_r, ang_c, ang_c], axis=1)
    ang = np.concatenate([ang, np.zeros((TM, ATT_DH), np.float32)], axis=0)
    ang = np.concatenate([ang, ang], axis=1)
    return np.cos(ang).astype(np.float32), np.sin(ang).astype(np.float32)


def kernel(x, c, ctx, c_ctx, w_mod, b_mod, ln_g, ln_b, mlp_w1, mlp_w2, ev_w_in, ev_ret_theta, ev_gla_gk_w,
           ev_gla_gk_b, ev_gla_norm_g, ev_w_out, od_w_qkv, od_sink, od_w_out):
    x2 = x.reshape(LAT_ROWS, D)
    ctx2 = ctx.reshape(B * L, D)

    cs = jnp.concatenate([c, c_ctx[None, :], jnp.zeros((8 - B - 1, D), F32)], axis=0)
    mods = _modulation(cs, w_mod, b_mod).reshape(DEPTH * 8, 1, 6 * D)
    ln = (ln_g.reshape(2 * DEPTH, D), ln_b.reshape(2 * DEPTH, D))

    gk_w = ev_gla_gk_w[0]
    zeros = jnp.zeros((GATE_RANK, QK_W), F32)
    g2 = jnp.concatenate([jnp.concatenate([gk_w[0], zeros], axis=1),
                          jnp.concatenate([zeros, gk_w[1]], axis=1)], axis=0)
    qa, kaT, va, ga, qb, kbT, vb, gbv, lfT, lbT = _in_proj0(
        x2, ctx2, mods, ev_w_in[0].T, g2.T.astype(BF16), ev_gla_gk_b[0].reshape(2 * QK_W, 1))

    w1_rows = DEPTH * D // SCAN_STEPS
    w2_rows = DEPTH * D_FF // SCAN_STEPS
    y_ret, yc_ret, w1 = _scan_group(qa, kaT, va, ga, (ev_ret_theta,),
                                    mlp_w1.reshape(SCAN_STEPS, w1_rows, D_FF), gla=False)
    y_gla, yc_gla, w2 = _scan_group(qb, kbT, vb, gbv, (lfT, lbT, ev_gla_norm_g),
                                    mlp_w2.reshape(SCAN_STEPS, w2_rows, D), gla=True)
    w1 = w1.reshape(DEPTH, D, D_FF)
    w2 = w2.reshape(DEPTH, D_FF, D)

    xs = _out_mlp((y_ret, y_gla, yc_ret, yc_gla), x2, ctx2, mods, ln, ev_w_out, w1, w2,
                  layer=0, n_tiles=N_TILES, split_ctx=True)

    cos_tab, sin_tab = _rope_tables()
    q1T, k1, v1T = _in_proj1(xs, mods, cos_tab, sin_tab, od_w_qkv)
    att = _attention(od_sink, q1T, k1, v1T)
    out = _out_mlp((att,), xs, None, mods, ln, od_w_out, w1, w2,
                   layer=1, n_tiles=N_LAT_TILES, split_ctx=False)
    return out.reshape(B, T, D)
```

```python
import functools

import jax
import jax.numpy as jnp
import numpy as np
from jax import lax
from jax.experimental import pallas as pl
from jax.experimental.pallas import tpu as pltpu

F32 = jnp.float32
BF16 = jnp.bfloat16

D = 1024
B = 2
T = 8192
L = 256
DEPTH = 2
GRID_W = 64
D_FF = 4 * D
HEAD_DV = 128
HEAD_DK = 64
GATE_RANK = 16
GATE_TAU = 16.0
QK_W = 256
V_W = 512
ATT_DH = 64
ATT_QH = 16
ATT_KVH = 4
ATT_GROUP = 4
ATT_KVW = ATT_KVH * ATT_DH
WINDOW = 128
ROPE_BASE = 10000.0
ALPHA = (2.0 * DEPTH) ** 0.25
LN_EPS = 1e-5
RMS_EPS = 1e-6

TM = 512
LAT_ROWS = B * T
ROWS = LAT_ROWS + B * L
N_LAT_TILES = LAT_ROWS // TM
N_TILES = ROWS // TM
TILES_PER_BATCH = T // TM

CH = 64
SP = 2 * CH
TB = 4096
NP = TB // SP
NP_CTX = L // SP
NBLK = T // TB
NP_ALL = NP_CTX + T // SP
N_SLABS = ROWS // SP
KV_UNROLL = 32
OUT_UNROLL = 32

TQ = 1024
QB = 128
KWIN = 3 * QB
LOG2E = 1.4426950408889634

VMEM_LIMIT = 56 * 1024 * 1024


def _dot(a, b):
    return jnp.dot(a, b, preferred_element_type=F32)


def _dot_nt(a, b):
    return lax.dot_general(a, b, (((1,), (1,)), ((), ())), preferred_element_type=F32)


def _full_spec(shape):
    nd = len(shape)
    return pl.BlockSpec(shape, lambda *_: (0,) * nd, pipeline_mode=pl.Buffered(1))


def _mod_row(t):
    return jnp.minimum(t // TILES_PER_BATCH, B)


def _mod_spec(layer):
    return pl.BlockSpec((None, 1, 6 * D), lambda t: (layer * 8 + _mod_row(t), 0, 0))


def _layer_norm(x, g, b):
    mu = jnp.mean(x, axis=-1, keepdims=True)
    xc = x - mu
    var = jnp.mean(xc * xc, axis=-1, keepdims=True)
    return xc * lax.rsqrt(var + LN_EPS) * g + b


def _log_sigmoid(z):
    return jnp.minimum(z, 0.0) - jnp.log1p(jnp.exp(-jnp.abs(z)))


MOD_TN = 1536


def _mod_kernel(c_ref, w_ref, b_ref, o_ref):
    s = jax.nn.silu(c_ref[...])
    s_hi = s.astype(BF16)
    s_lo = (s - s_hi.astype(F32)).astype(BF16)
    w = w_ref[...].astype(BF16)
    o_ref[...] = _dot(s_hi, w) + _dot(s_lo, w) + b_ref[...]


def _modulation(cs, w_mod, b_mod):
    return pl.pallas_call(
        _mod_kernel,
        grid=(DEPTH, 6 * D // MOD_TN),
        in_specs=[
            pl.BlockSpec((8, D), lambda i, n: (0, 0)),
            pl.BlockSpec((None, D, MOD_TN), lambda i, n: (i, 0, n)),
            pl.BlockSpec((None, 1, MOD_TN), lambda i, n: (i, 0, n)),
        ],
        out_specs=pl.BlockSpec((None, 8, MOD_TN), lambda i, n: (i, 0, n)),
        out_shape=jax.ShapeDtypeStruct((DEPTH, 8, 6 * D), F32),
        compiler_params=pltpu.CompilerParams(
            dimension_semantics=("arbitrary", "arbitrary"), vmem_limit_bytes=VMEM_LIMIT),
        name="modulation",
    )(cs, w_mod, b_mod.reshape(DEPTH, 1, 6 * D))


IN0_OFF = (0, 256, 512, 1024, 1536, 1792, 2048, 2560, 3072, 3104)


def _transpose_bf16(w):
    return w.astype(F32).T.astype(BF16)


def _in0_kernel(x_ref, ctx_ref, mod_ref, wT_ref, g2T, gbc,
                qa_o, kaT_o, va_o, ga_o, qb_o, kbT_o, vb_o, gb_o, lfT_o, lbT_o, wbT_s):
    t = pl.program_id(0)
    rows = lambda i: slice(IN0_OFF[i], IN0_OFF[i + 1])
    piece = lambda i: wbT_s[rows(i), :]

    @pl.when(t == 0)
    def _():
        for i in range(len(IN0_OFF) - 1):
            wbT_s[rows(i), :] = wT_ref[rows(i), :].astype(BF16)

    xt = jnp.where(t == N_LAT_TILES, ctx_ref[...], x_ref[...])
    sh1 = mod_ref[:, 0:D]
    sc1 = mod_ref[:, D:2 * D]
    h = (xt * (1.0 + sc1) + sh1).astype(BF16)
    qk_scale = HEAD_DK ** -0.5

    def put_slabs(o_ref, val):
        for i in range(TM // SP):
            o_ref[i] = val[:, i * SP:(i + 1) * SP].astype(o_ref.dtype)

    lr = _dot_nt(h, piece(8)).astype(BF16)
    lsT = _log_sigmoid(_dot_nt(g2T[...], lr) + gbc[...]) * (1.0 / GATE_TAU)
    put_slabs(lfT_o, lsT[0:QK_W, :])
    put_slabs(lbT_o, lsT[QK_W:2 * QK_W, :])
    ga_o[...] = jax.nn.silu(_dot_nt(h, piece(3))).astype(BF16)
    gb_o[...] = jax.nn.silu(_dot_nt(h, piece(7))).astype(BF16)

    put_slabs(kaT_o, _dot_nt(piece(1), h) * qk_scale)
    put_slabs(kbT_o, _dot_nt(piece(5), h))
    qa_o[...] = _dot_nt(h, piece(0)).astype(BF16)
    qb_o[...] = (_dot_nt(h, piece(4)) * qk_scale).astype(BF16)
    va_o[...] = _dot_nt(h, piece(2)).astype(BF16)
    vb_o[...] = _dot_nt(h, piece(6)).astype(BF16)


def _in_proj0(x2, ctx2, mods, w_inT, g2T, gbc):
    row = lambda t: (t, 0)
    slab = lambda t: (t, 0, 0)
    nsl = TM // SP
    row_out = lambda width: (jax.ShapeDtypeStruct((ROWS, width), BF16), pl.BlockSpec((TM, width), row))
    slab_out = lambda dt: (jax.ShapeDtypeStruct((N_SLABS, QK_W, SP), dt), pl.BlockSpec((nsl, QK_W, SP), slab))
    outs = [row_out(QK_W), slab_out(BF16), row_out(V_W), row_out(V_W),
            row_out(QK_W), slab_out(BF16), row_out(V_W), row_out(V_W),
            slab_out(BF16), slab_out(BF16)]
    in_specs = [
        pl.BlockSpec((TM, D), lambda t: (jnp.minimum(t, N_LAT_TILES - 1), 0)),
        _full_spec((B * L, D)),
        _mod_spec(0),
        _full_spec(w_inT.shape), _full_spec(g2T.shape), _full_spec(gbc.shape),
    ]
    return pl.pallas_call(
        _in0_kernel,
        grid=(N_TILES,),
        in_specs=in_specs,
        out_specs=[o[1] for o in outs],
        out_shape=[o[0] for o in outs],
        scratch_shapes=[pltpu.VMEM(w_inT.shape, BF16)],
        compiler_params=pltpu.CompilerParams(
            dimension_semantics=("arbitrary",), vmem_limit_bytes=VMEM_LIMIT),
        name="in_proj0",
    )(x2, ctx2, mods, w_inT, g2T, gbc)


def _chunk_diag(kv, c):
    r0 = c * 2 * HEAD_DK
    return jnp.concatenate([kv[r0:r0 + HEAD_DK, 0:HEAD_DV],
                            kv[r0 + HEAD_DK:r0 + 2 * HEAD_DK, HEAD_DV:2 * HEAD_DV]], axis=0)


def _scan_kernel(*refs, gla):
    if gla:
        (q_ref, kT_ref, v_ref, g_ref, lfT_ref, lbT_ref,
         qc_ref, kTc_ref, vc_ref, gc_ref, lfTc_ref, lbTc_ref, ng_ref, wsrc_ref,
         y_ref, yc_ref, wdst_ref, s_ref, r_ref, rst_ref, sst_ref, kv_ref, dec_ref,
         kd_ref, lhs_ref, pm_ref) = refs
        lat = (q_ref, kT_ref, v_ref, g_ref, lfT_ref, lbT_ref)
        cxt = (qc_ref, kTc_ref, vc_ref, gc_ref, lfTc_ref, lbTc_ref)
    else:
        (q_ref, kT_ref, v_ref, g_ref, qc_ref, kTc_ref, vc_ref, gc_ref, th_ref, wsrc_ref,
         y_ref, yc_ref, wdst_ref, s_ref, r_ref, rst_ref, sst_ref, kv_ref, dec_ref,
         kd_ref, lhs_ref, pm_ref) = refs
        lat = (q_ref, kT_ref, v_ref, g_ref, None, None)
        cxt = (qc_ref, kTc_ref, vc_ref, gc_ref, None, None)

    phase = pl.program_id(2)
    j = pl.program_id(3)

    wdst_ref[...] = wsrc_ref[...].astype(BF16)

    ri = lax.broadcasted_iota(jnp.int32, (SP, SP), 0)
    ci = lax.broadcasted_iota(jnp.int32, (SP, SP), 1)
    same = (ri // CH) == (ci // CH)
    first_lane = ci < CH
    head_a = ci < HEAD_DK

    if gla:
        as_w = lambda m: m.astype(BF16)
        tot = jnp.concatenate([jnp.broadcast_to(ri < CH, (SP, SP)), jnp.broadcast_to(ri >= CH, (SP, SP))], axis=1)
        w_end_f = jnp.concatenate([as_w(same & (ri > ci)), as_w(tot)], axis=1)
        w_end_b = jnp.concatenate([as_w(same & (ri < ci)), as_w(tot)], axis=1)
        w_cum_f = as_w(same & (ri <= ci))
        w_cum_b = as_w(same & (ri >= ci))
    else:
        hp = pl.program_id(1)
        th = [[th_ref[0, dr, 2 * hp + hd] for hd in range(2)] for dr in range(2)]
        lane1 = lax.broadcasted_iota(jnp.int32, (1, SP), 1) < HEAD_DK
        row1 = lax.broadcasted_iota(jnp.int32, (SP, 1), 0) < HEAD_DK
        lg_row = [jnp.log1p(-jnp.exp(jnp.where(lane1, th[dr][0], th[dr][1]))) for dr in range(2)]
        lg_col = [jnp.log1p(-jnp.exp(jnp.where(row1, th[dr][0], th[dr][1]))) for dr in range(2)]
        it = (ci % CH).astype(F32)
        ir = (ri % CH).astype(F32)
        ret_end_f = jnp.exp((CH - 1.0 - it) * lg_col[0])
        ret_end_b = jnp.exp(it * lg_col[1])
        ret_dec_f = jnp.exp(jnp.broadcast_to(CH * lg_col[0], (SP, SP)))
        ret_dec_b = jnp.exp(jnp.broadcast_to(CH * lg_col[1], (SP, SP)))
        ret_ebTi = jnp.exp(-(it + 1.0) * lg_col[0])
        ret_erTi = jnp.exp(-(CH - it) * lg_col[1])
        ret_eb = jnp.exp((ir + 1.0) * lg_row[0])
        ret_er = jnp.exp((CH - ir) * lg_row[1])

    def kv_stage(blk, n, fwd):
        _, kT_r, v_r, _, lfT_r, lbT_r = blk

        def body(p, carry):
            kT = kT_r[p].astype(F32)
            v = v_r[pl.ds(pl.multiple_of(p * SP, SP), SP), :]
            if gla:
                res = _dot((lfT_r if fwd else lbT_r)[p], w_end_f if fwd else w_end_b)
                e_end = jnp.exp(res[:, 0:SP])
                dec0 = jnp.exp(res[:, SP:2 * SP])
                dec1 = jnp.exp(res[:, 2 * SP:3 * SP])
            else:
                e_end = ret_end_f if fwd else ret_end_b
                dec0 = dec1 = ret_dec_f if fwd else ret_dec_b
            ke = kT * e_end
            lhs = jnp.concatenate([jnp.where(first_lane, ke, 0.0), jnp.where(first_lane, 0.0, ke)],
                                  axis=0).astype(BF16)
            kv = _dot(lhs, v)
            kv_ref[2 * p] = _chunk_diag(kv, 0)
            kv_ref[2 * p + 1] = _chunk_diag(kv, 1)
            dec_ref[2 * p] = dec0
            dec_ref[2 * p + 1] = dec1
            return carry

        lax.fori_loop(0, n, body, 0, unroll=min(n, KV_UNROLL))

    def phase0_block(blk, n, slot0):
        kv_stage(blk, n, fwd=False)

        def body(i, r_state):
            p = n - 1 - i
            rst_ref[slot0 + p, :, HEAD_DV:2 * HEAD_DV] = r_state.astype(BF16)
            r_state = dec_ref[2 * p + 1] * r_state + kv_ref[2 * p + 1]
            rst_ref[slot0 + p, :, 0:HEAD_DV] = r_state.astype(BF16)
            return dec_ref[2 * p] * r_state + kv_ref[2 * p]

        r_ref[...] = lax.fori_loop(0, n, body, r_ref[...])

    def phase1_block(blk, n, slot0, out_ref):
        q_r, kT_r, v_r, g_r, lfT_r, lbT_r = blk
        kv_stage(blk, n, fwd=True)

        def rec(p, s_state):
            sst_ref[p, :, 0:HEAD_DV] = s_state.astype(BF16)
            s_state = dec_ref[2 * p] * s_state + kv_ref[2 * p]
            sst_ref[p, :, HEAD_DV:2 * HEAD_DV] = s_state.astype(BF16)
            return dec_ref[2 * p + 1] * s_state + kv_ref[2 * p + 1]

        s_ref[...] = lax.fori_loop(0, n, rec, s_ref[...])

        r2 = lax.broadcasted_iota(jnp.int32, (2 * SP, SP), 0) % SP
        c2 = lax.broadcasted_iota(jnp.int32, (2 * SP, SP), 1)
        same2 = (r2 // CH) == (c2 // CH)
        mask_f = same2 & (r2 >= c2)
        mask_b = same2 & (r2 < c2)

        def prep(p, carry):
            rows = pl.ds(pl.multiple_of(p * SP, SP), SP)
            q = q_r[rows, :].astype(F32)
            kT = kT_r[p].astype(F32)
            if gla:
                lfT = lfT_r[p]
                lbT = lbT_r[p]
                ebTi = jnp.exp(-_dot(lfT, w_cum_f))
                erTi = jnp.exp(-_dot(lbT, w_cum_b))
                e_b = jnp.exp(_dot_nt(w_cum_b, lfT))
                e_r = jnp.exp(_dot_nt(w_cum_f, lbT))
            else:
                ebTi, erTi, e_b, e_r = ret_ebTi, ret_erTi, ret_eb, ret_er
            kd_ref[p, 0:SP, 0:SP] = (kT * ebTi).astype(BF16)
            kd_ref[p, SP:2 * SP, SP:2 * SP] = (kT * erTi).astype(BF16)
            qf = q * e_b
            qb = q * e_r
            lhs_ref[p, :, 0:SP] = jnp.concatenate(
                [jnp.where(head_a, qf, 0.0), jnp.where(head_a, 0.0, qf)], axis=0).astype(BF16)
            lhs_ref[p, :, SP:2 * SP] = jnp.concatenate(
                [jnp.where(head_a, qb, 0.0), jnp.where(head_a, 0.0, qb)], axis=0).astype(BF16)
            return carry

        def score(p, carry):
            sc = _dot(lhs_ref[p], kd_ref[p])
            pm_ref[p] = jnp.where(mask_f, sc[:, 0:SP], jnp.where(mask_b, sc[:, SP:2 * SP], 0.0)).astype(BF16)
            return carry

        def emit(p, carry):
            rows = pl.ds(pl.multiple_of(p * SP, SP), SP)
            v = v_r[rows, :]
            states = jnp.concatenate([sst_ref[p], rst_ref[slot0 + p]], axis=0)
            o_int = _dot(lhs_ref[p], states)
            o_a = _dot(pm_ref[p, 0:SP, :], v[:, 0:HEAD_DV]) + jnp.concatenate(
                [o_int[0:CH, 0:HEAD_DV], o_int[CH:SP, HEAD_DV:2 * HEAD_DV]], axis=0)
            o_b = _dot(pm_ref[p, SP:2 * SP, :], v[:, HEAD_DV:2 * HEAD_DV]) + jnp.concatenate(
                [o_int[SP:SP + CH, 0:HEAD_DV], o_int[SP + CH:2 * SP, HEAD_DV:2 * HEAD_DV]], axis=0)

            def nrm(o):
                y = o * lax.rsqrt(jnp.mean(o * o, axis=-1, keepdims=True) + RMS_EPS)
                return y * ng_ref[...] if gla else y

            y = jnp.concatenate([nrm(o_a), nrm(o_b)], axis=1) * g_r[rows, :].astype(F32)
            out_ref[rows, :] = y.astype(BF16)
            return carry

        lax.fori_loop(0, n, prep, 0, unroll=min(n, OUT_UNROLL))
        lax.fori_loop(0, n, score, 0, unroll=min(n, OUT_UNROLL))
        lax.fori_loop(0, n, emit, 0, unroll=min(n, OUT_UNROLL))

    @pl.when((pl.program_id(0) == 0) & (pl.program_id(1) == 0) & (phase == 0) & (j == 0))
    def _():
        kd_ref[...] = jnp.zeros_like(kd_ref)

    @pl.when(phase == 0)
    def _():
        @pl.when(j == 0)
        def _():
            r_ref[...] = jnp.zeros_like(r_ref)
            phase0_block(cxt, NP_CTX, 0)

        phase0_block(lat, NP, NP_CTX + (NBLK - 1 - j) * NP)

    @pl.when(phase == 1)
    def _():
        @pl.when(j == 0)
        def _():
            s_ref[...] = jnp.zeros_like(s_ref)
            phase1_block(cxt, NP_CTX, 0, yc_ref)

        phase1_block(lat, NP, NP_CTX + j * NP, y_ref)


SCAN_STEPS = B * 2 * 2 * NBLK


def _scan_group(q, kT, v, g, extra, wsrc, *, gla):
    w_spec = pl.BlockSpec((None,) + wsrc.shape[1:],
                          lambda b, p, ph, j: (((b * 2 + p) * 2 + ph) * NBLK + j, 0, 0))

    def blk(b, ph, j, used_in_phase0):
        jj = jnp.where(ph == 0, NBLK - 1 - j, j)
        if not used_in_phase0:
            jj = jnp.where(ph == 0, 0, jj)
        return b * NBLK + jj

    def lat_specs(used0):
        return dict(
            row=lambda w: pl.BlockSpec((TB, w), lambda b, p, ph, j: (blk(b, ph, j, used0), p)),
            slab=pl.BlockSpec((NP, SP, SP), lambda b, p, ph, j: (blk(b, ph, j, used0), p, 0)))

    ctx_row = lambda w: pl.BlockSpec((L, w), lambda b, p, ph, j: (LAT_ROWS // L + b, p))
    ctx_slab = pl.BlockSpec((NP_CTX, SP, SP), lambda b, p, ph, j: (LAT_ROWS // L + b, p, 0))
    used, unused = lat_specs(True), lat_specs(False)

    in_specs = [unused["row"](2 * HEAD_DK), used["slab"], used["row"](2 * HEAD_DV), unused["row"](2 * HEAD_DV)]
    ctx_specs = [ctx_row(2 * HEAD_DK), ctx_slab, ctx_row(2 * HEAD_DV), ctx_row(2 * HEAD_DV)]
    if gla:
        lfT, lbT, ng = extra
        in_specs += [unused["slab"], used["slab"]]
        ctx_specs += [ctx_slab, ctx_slab]
        args = (q, kT, v, g, lfT, lbT, q, kT, v, g, lfT, lbT, ng, wsrc)
        in_specs = in_specs + ctx_specs + [pl.BlockSpec((1, HEAD_DV), lambda b, p, ph, j: (0, 0)), w_spec]
    else:
        (theta,) = extra
        args = (q, kT, v, g, q, kT, v, g, theta, wsrc)
        in_specs = in_specs + ctx_specs + [pl.BlockSpec(memory_space=pltpu.SMEM), w_spec]
    return pl.pallas_call(
        functools.partial(_scan_kernel, gla=gla),
        grid=(B, 2, 2, NBLK),
        in_specs=in_specs,
        out_specs=[
            pl.BlockSpec((TB, 2 * HEAD_DV), lambda b, p, ph, j: (b * NBLK + jnp.where(ph == 0, 0, j), p)),
            pl.BlockSpec((L, 2 * HEAD_DV), lambda b, p, ph, j: (b, p)),
            w_spec,
        ],
        out_shape=[jax.ShapeDtypeStruct((LAT_ROWS, V_W), BF16), jax.ShapeDtypeStruct((B * L, V_W), BF16),
                   jax.ShapeDtypeStruct(wsrc.shape, BF16)],
        scratch_shapes=[
            pltpu.VMEM((SP, HEAD_DV), F32),
            pltpu.VMEM((SP, HEAD_DV), F32),
            pltpu.VMEM((NP_ALL, SP, 2 * HEAD_DV), BF16),
            pltpu.VMEM((NP, SP, 2 * HEAD_DV), BF16),
            pltpu.VMEM((2 * NP, SP, HEAD_DV), F32),
            pltpu.VMEM((2 * NP, SP, HEAD_DV), F32),
            pltpu.VMEM((NP, 2 * SP, 2 * SP), BF16),
            pltpu.VMEM((NP, 2 * SP, 2 * SP), BF16),
            pltpu.VMEM((NP, 2 * SP, SP), BF16),
        ],
        compiler_params=pltpu.CompilerParams(
            dimension_semantics=("arbitrary",) * 4, vmem_limit_bytes=VMEM_LIMIT),
        name="scan_gla" if gla else "scan_ret",
    )(*args)


FF_CH = 512


def _out_kernel(*refs, layer, split_ctx):
    if split_ctx:
        (ya_ref, yb_ref, yac_ref, ybc_ref, x_ref, ctx_ref, mod_ref, lng_ref, lnb_ref,
         wo_ref, w1_ref, w2_ref, o_ref, wo_s) = refs
        is_ctx = pl.program_id(0) == N_LAT_TILES
        x = jnp.where(is_ctx, ctx_ref[...], x_ref[...])
        ya = jnp.where(is_ctx, yac_ref[...], ya_ref[...])
        yb = jnp.where(is_ctx, ybc_ref[...], yb_ref[...])
    else:
        ya_ref, yb_ref, x_ref, mod_ref, lng_ref, lnb_ref, wo_ref, w1_ref, w2_ref, o_ref, wo_s = refs
        x = x_ref[...]
        ya = ya_ref[...]
        yb = yb_ref[...]
    g1 = mod_ref[:, 2 * D:3 * D]
    sh2 = mod_ref[:, 3 * D:4 * D]
    sc2 = mod_ref[:, 4 * D:5 * D]
    g2 = mod_ref[:, 5 * D:6 * D]
    ln_g0 = lng_ref[2 * layer:2 * layer + 1, :]
    ln_g1 = lng_ref[2 * layer + 1:2 * layer + 2, :]
    ln_b0 = lnb_ref[2 * layer:2 * layer + 1, :]
    ln_b1 = lnb_ref[2 * layer + 1:2 * layer + 2, :]

    @pl.when(pl.program_id(0) == 0)
    def _():
        wo_s[...] = wo_ref[...].astype(BF16)

    half = D // 2
    y = _dot(ya, wo_s[0:half, :]) + _dot(yb, wo_s[half:D, :])
    x1 = _layer_norm(ALPHA * x + g1 * y, ln_g0, ln_b0)
    h2 = (x1 * (1.0 + sc2) + sh2).astype(BF16)
    acc = jnp.zeros((TM, D), F32)
    for c in range(D_FF // FF_CH):
        cols = slice(c * FF_CH, (c + 1) * FF_CH)
        hc = jnp.maximum(_dot(h2, w1_ref[:, cols]), 0.0)
        acc = acc + _dot((hc * hc).astype(BF16), w2_ref[cols, :])
    o_ref[...] = _layer_norm(ALPHA * x1 + g2 * acc, ln_g1, ln_b1)


def _out_mlp(ys, xs, ctx2, mods, ln, wo, w1, w2, *, layer, n_tiles, split_ctx):
    half = D // 2
    lat_row = lambda t: (jnp.minimum(t, N_LAT_TILES - 1), 0)
    if split_ctx:
        ya, yb, yac, ybc = ys
        in_specs = [pl.BlockSpec((TM, half), lat_row), pl.BlockSpec((TM, half), lat_row),
                    _full_spec((B * L, half)), _full_spec((B * L, half)),
                    pl.BlockSpec((TM, D), lat_row), _full_spec((B * L, D))]
        args = [ya, yb, yac, ybc, xs, ctx2]
    else:
        (att,) = ys
        in_specs = [pl.BlockSpec((TM, half), lambda t: (t, 0)), pl.BlockSpec((TM, half), lambda t: (t, 1)),
                    pl.BlockSpec((TM, D), lambda t: (t, 0))]
        args = [att, att, xs]
    stacked = lambda w, i: pl.BlockSpec((None,) + w.shape[1:], lambda t: (i, 0, 0), pipeline_mode=pl.Buffered(1))
    in_specs += [_mod_spec(layer), _full_spec(ln[0].shape), _full_spec(ln[1].shape),
                 stacked(wo, 0), stacked(w1, layer), stacked(w2, layer)]
    args += [mods, ln[0], ln[1], wo, w1, w2]
    return pl.pallas_call(
        functools.partial(_out_kernel, layer=layer, split_ctx=split_ctx),
        grid=(n_tiles,),
        in_specs=in_specs,
        out_specs=pl.BlockSpec((TM, D), lambda t: (t, 0)),
        out_shape=jax.ShapeDtypeStruct((n_tiles * TM, D), F32),
        compiler_params=pltpu.CompilerParams(
            dimension_semantics=("arbitrary",), vmem_limit_bytes=VMEM_LIMIT),
        scratch_shapes=[pltpu.VMEM((D, D), BF16)],
        name="out_mlp%d" % layer,
    )(*args)


def _in1_kernel(x_ref, mod_ref, cos_ref, sin_ref, w_ref, qT_o, k_o, vT_o, wqT_s, wk_s, wvT_s):
    @pl.when(pl.program_id(0) == 0)
    def _():
        wqT_s[...] = _transpose_bf16(w_ref[:, 0:D])
        wk_s[...] = w_ref[:, D:D + ATT_KVW].astype(BF16)
        wvT_s[...] = _transpose_bf16(w_ref[:, D + ATT_KVW:D + 2 * ATT_KVW])

    sh1 = mod_ref[:, 0:D]
    sc1 = mod_ref[:, D:2 * D]
    h = (x_ref[...] * (1.0 + sc1) + sh1).astype(BF16)
    nsl = TM // SP

    cos = cos_ref[...]
    sin = sin_ref[...]
    first = (lax.broadcasted_iota(jnp.int32, (TM, 128), 1) % 32) < 16
    sa = jnp.where(first, -sin, 0.0)
    sb = jnp.where(first, 0.0, sin)

    k = _dot(h, wk_s[...])
    for i in range(ATT_KVW // 128):
        u = k[:, i * 128:(i + 1) * 128]
        r = u * cos + pltpu.roll(u, 128 - 16, 1) * sa + pltpu.roll(u, 16, 1) * sb
        k_o[:, i * 128:(i + 1) * 128] = r.astype(BF16)

    cosT, saT, sbT = cos.T, sa.T, sb.T
    qT = _dot_nt(wqT_s[...], h) * (LOG2E * ATT_DH ** -0.5)
    for i in range(D // 128):
        u = qT[i * 128:(i + 1) * 128, :]
        r = (u * cosT + pltpu.roll(u, 128 - 16, 0) * saT + pltpu.roll(u, 16, 0) * sbT).astype(BF16)
        for s in range(nsl):
            qT_o[s, i * 128:(i + 1) * 128, :] = r[:, s * SP:(s + 1) * SP]

    vT = _dot_nt(wvT_s[...], h).astype(BF16)
    for s in range(nsl):
        vT_o[s] = vT[:, s * SP:(s + 1) * SP]


def _in_proj1(xs, mods, cos_tab, sin_tab, wqkv):
    tile = lambda t: jnp.where(t == N_LAT_TILES, TILES_PER_BATCH, t % TILES_PER_BATCH)
    row = lambda t: (t, 0)
    slab = lambda t: (t, 0, 0)
    nsl = TM // SP
    tab_spec = pl.BlockSpec((TM, 128), lambda t: (tile(t), 0))
    return pl.pallas_call(
        _in1_kernel,
        grid=(N_TILES,),
        in_specs=[pl.BlockSpec((TM, D), row), _mod_spec(1), tab_spec, tab_spec,
                  pl.BlockSpec((None,) + wqkv.shape[1:], lambda t: (0, 0, 0), pipeline_mode=pl.Buffered(1))],
        out_specs=[pl.BlockSpec((nsl, D, SP), slab), pl.BlockSpec((TM, ATT_KVW), row),
                   pl.BlockSpec((nsl, ATT_KVW, SP), slab)],
        out_shape=[jax.ShapeDtypeStruct((N_SLABS, D, SP), BF16), jax.ShapeDtypeStruct((ROWS, ATT_KVW), BF16),
                   jax.ShapeDtypeStruct((N_SLABS, ATT_KVW, SP), BF16)],
        scratch_shapes=[pltpu.VMEM((D, D), BF16), pltpu.VMEM((D, ATT_KVW), BF16), pltpu.VMEM((ATT_KVW, D), BF16)],
        compiler_params=pltpu.CompilerParams(
            dimension_semantics=("arbitrary",), vmem_limit_bytes=VMEM_LIMIT),
        name="in_proj1",
    )(xs, mods, cos_tab, sin_tab, wqkv)


def _attn_kernel(sink_ref, qT_ref, k_ref, vT_ref, kc_ref, vcT_ref, o_ref, bias_ref):
    n = pl.program_id(1)
    nql = ATT_GROUP * QB
    lane_g = lax.broadcasted_iota(jnp.int32, (1, nql), 1) // QB
    zero_half = jnp.zeros((ATT_DH, nql), BF16)
    ones_rows = jnp.ones((16, KWIN + L), BF16)

    @pl.when((pl.program_id(0) == 0) & (n == 0))
    def _():
        kj = lax.broadcasted_iota(jnp.int32, (KWIN, QB), 0)
        qi = lax.broadcasted_iota(jnp.int32, (KWIN, QB), 1)
        for case, delta in enumerate((-QB, 0, -2 * QB)):
            d = kj - qi + delta
            bias_ref[case] = jnp.where((d >= -WINDOW) & (d <= WINDOW), 0.0, -jnp.inf)

    def scores(i, kh):
        n0 = (n * (TQ // QB) + i) * QB
        start = pl.multiple_of(jnp.clip(n0 - QB, 0, T - KWIN), QB)
        bias1 = bias_ref[jnp.where(n0 == 0, 1, jnp.where(n0 == T - QB, 2, 0))]
        bias = jnp.concatenate([bias1] * ATT_GROUP, axis=1)
        pair = slice((kh // 2) * 128, (kh // 2 + 1) * 128)
        qT = jnp.concatenate(
            [qT_ref[i, (kh * ATT_GROUP + g) * ATT_DH:(kh * ATT_GROUP + g + 1) * ATT_DH, :]
             for g in range(ATT_GROUP)], axis=1)
        q_pad = jnp.concatenate([zero_half, qT] if kh % 2 else [qT, zero_half], axis=0)
        s_loc = _dot(k_ref[pl.ds(start, KWIN), pair], q_pad) + bias
        s_ctx = _dot(kc_ref[:, pair], q_pad)
        return s_loc, s_ctx, start // SP

    def finish(i, kh, s_loc, s_ctx, slab0):
        drows = slice(kh * ATT_DH, (kh + 1) * ATT_DH)
        sink = jnp.zeros((1, nql), F32)
        for g in range(ATT_GROUP):
            sink = jnp.where(lane_g == g, sink_ref[0, kh * ATT_GROUP + g] * LOG2E, sink)
        m = jnp.maximum(jnp.maximum(jnp.max(s_loc, axis=0, keepdims=True),
                                    jnp.max(s_ctx, axis=0, keepdims=True)), sink)
        pT = jnp.concatenate([jnp.exp2(s_loc - m).astype(BF16),
                              jnp.exp2(s_ctx - m).astype(BF16)], axis=0)
        vT = jnp.concatenate([vT_ref[slab0 + t, drows, :] for t in range(KWIN // SP)]
                             + [vcT_ref[t, drows, :] for t in range(L // SP)], axis=1)
        o_ext = _dot(jnp.concatenate([vT, ones_rows], axis=0), pT)
        denom = o_ext[ATT_DH:ATT_DH + 1, :] + jnp.exp2(sink - m)
        oT = o_ext[0:ATT_DH, :] / denom
        for g in range(0, ATT_GROUP, 2):
            two = jnp.concatenate([oT[:, g * QB:(g + 1) * QB], oT[:, (g + 1) * QB:(g + 2) * QB]], axis=0)
            c0 = (kh * ATT_GROUP + g) * ATT_DH
            o_ref[i * QB:(i + 1) * QB, c0:c0 + 2 * ATT_DH] = two.T.astype(BF16)

    items = [(i, kh) for i in range(TQ // QB) for kh in range(ATT_KVH)]
    pending = scores(*items[0])
    for t, item in enumerate(items):
        nxt = scores(*items[t + 1]) if t + 1 < len(items) else None
        finish(*item, *pending)
        pending = nxt


def _attention(sink, qT, k, vT):
    nq = T // TQ
    return pl.pallas_call(
        _attn_kernel,
        grid=(B, nq),
        in_specs=[
            pl.BlockSpec(memory_space=pltpu.SMEM),
            pl.BlockSpec((TQ // SP, D, SP), lambda b, n: (b * nq + n, 0, 0)),
            pl.BlockSpec((T, ATT_KVW), lambda b, n: (b, 0)),
            pl.BlockSpec((T // SP, ATT_KVW, SP), lambda b, n: (b, 0, 0)),
            pl.BlockSpec((L, ATT_KVW), lambda b, n: (LAT_ROWS // L + b, 0)),
            pl.BlockSpec((L // SP, ATT_KVW, SP), lambda b, n: (LAT_ROWS // L + b, 0, 0)),
        ],
        out_specs=pl.BlockSpec((TQ, D), lambda b, n: (b * nq + n, 0)),
        out_shape=jax.ShapeDtypeStruct((LAT_ROWS, D), BF16),
        compiler_params=pltpu.CompilerParams(
            dimension_semantics=("arbitrary", "arbitrary"), vmem_limit_bytes=VMEM_LIMIT),
        scratch_shapes=[pltpu.VMEM((3, KWIN, QB), F32)],
        name="window_attn",
    )(sink, qT, k, vT, k, vT)


def _rope_tables():
    half = ATT_DH // 2
    inv_freq = np.power(np.float32(ROPE_BASE), -np.arange(0, half, 2, dtype=np.float32) / np.float32(half))
    inv_freq = inv_freq.astype(np.float32)
    pos = np.arange(T)
    ang_r = (pos // GRID_W).astype(np.float32)[:, None] * inv_freq[None, :]
    ang_c = (pos % GRID_W).astype(np.float32)[:, None] * inv_freq[None, :]
    ang = np.concatenate([ang_r, ang_r, ang_c, ang_c], axis=1)
    ang = np.concatenate([ang, np.zeros((TM, ATT_DH), np.float32)], axis=0)
    ang = np.concatenate([ang, ang], axis=1)
    return np.cos(ang).astype(np.float32), np.sin(ang).astype(np.float32)


def kernel(x, c, ctx, c_ctx, w_mod, b_mod, ln_g, ln_b, mlp_w1, mlp_w2, ev_w_in, ev_ret_theta, ev_gla_gk_w,
           ev_gla_gk_b, ev_gla_norm_g, ev_w_out, od_w_qkv, od_sink, od_w_out):
    x2 = x.reshape(LAT_ROWS, D)
    ctx2 = ctx.reshape(B * L, D)

    cs = jnp.concatenate([c, c_ctx[None, :], jnp.zeros((8 - B - 1, D), F32)], axis=0)
    mods = _modulation(cs, w_mod, b_mod).reshape(DEPTH * 8, 1, 6 * D)
    ln = (ln_g.reshape(2 * DEPTH, D), ln_b.reshape(2 * DEPTH, D))

    gk_w = ev_gla_gk_w[0]
    zeros = jnp.zeros((GATE_RANK, QK_W), F32)
    g2 = jnp.concatenate([jnp.concatenate([gk_w[0], zeros], axis=1),
                          jnp.concatenate([zeros, gk_w[1]], axis=1)], axis=0)
    qa, kaT, va, ga, qb, kbT, vb, gbv, lfT, lbT = _in_proj0(
        x2, ctx2, mods, ev_w_in[0].T, g2.T.astype(BF16), ev_gla_gk_b[0].reshape(2 * QK_W, 1))

    w1_rows = DEPTH * D // SCAN_STEPS
    w2_rows = DEPTH * D_FF // SCAN_STEPS
    y_ret, yc_ret, w1 = _scan_group(qa, kaT, va, ga, (ev_ret_theta,),
                                    mlp_w1.reshape(SCAN_STEPS, w1_rows, D_FF), gla=False)
    y_gla, yc_gla, w2 = _scan_group(qb, kbT, vb, gbv, (lfT, lbT, ev_gla_norm_g),
                                    mlp_w2.reshape(SCAN_STEPS, w2_rows, D), gla=True)
    w1 = w1.reshape(DEPTH, D, D_FF)
    w2 = w2.reshape(DEPTH, D_FF, D)

    xs = _out_mlp((y_ret, y_gla, yc_ret, yc_gla), x2, ctx2, mods, ln, ev_w_out, w1, w2,
                  layer=0, n_tiles=N_TILES, split_ctx=True)

    cos_tab, sin_tab = _rope_tables()
    q1T, k1, v1T = _in_proj1(xs, mods, cos_tab, sin_tab, od_w_qkv)
    att = _attention(od_sink, q1T, k1, v1T)
    out = _out_mlp((att,), xs, None, mods, ln, od_w_out, w1, w2,
                   layer=1, n_tiles=N_LAT_TILES, split_ctx=False)
    return out.reshape(B, T, D)
```

```python
import functools

import jax
import jax.numpy as jnp
import numpy as np
from jax import lax
from jax.experimental import pallas as pl
from jax.experimental.pallas import tpu as pltpu

F32 = jnp.float32
BF16 = jnp.bfloat16

D = 1024
B = 2
T = 8192
L = 256
DEPTH = 2
GRID_W = 64
D_FF = 4 * D
HEAD_DV = 128
HEAD_DK = 64
GATE_RANK = 16
GATE_TAU = 16.0
QK_W = 256
V_W = 512
ATT_DH = 64
ATT_QH = 16
ATT_KVH = 4
ATT_GROUP = 4
ATT_KVW = ATT_KVH * ATT_DH
WINDOW = 128
ROPE_BASE = 10000.0
ALPHA = (2.0 * DEPTH) ** 0.25
LN_EPS = 1e-5
RMS_EPS = 1e-6

LANES = 128
SUBLANES = 8
BF16_ROWS = 2 * SUBLANES
MOD_ROWS = SUBLANES
ROT = ATT_DH // 4

TM = 512
LAT_ROWS = B * T
ROWS = LAT_ROWS + B * L
N_LAT_TILES = LAT_ROWS // TM
N_TILES = ROWS // TM
TILES_PER_BATCH = T // TM

CH = 64
SP = 2 * CH
TB = 4096
NP = TB // SP
NP_CTX = L // SP
NBLK = T // TB
NP_ALL = NP_CTX + T // SP
N_SLABS = ROWS // SP
assert SP == LANES and 2 * HEAD_DK == LANES and HEAD_DV == LANES
KV_UNROLL = 32
OUT_UNROLL = 32

TQ = 1024
QB = 128
KWIN = 3 * QB
LOG2E = 1.4426950408889634

VMEM_LIMIT = 56 * 1024 * 1024


def _dot(a, b):
    return jnp.dot(a, b, preferred_element_type=F32)


def _dot_nt(a, b):
    return lax.dot_general(a, b, (((1,), (1,)), ((), ())), preferred_element_type=F32)


def _full_spec(shape):
    nd = len(shape)
    return pl.BlockSpec(shape, lambda *_: (0,) * nd, pipeline_mode=pl.Buffered(1))


def _mod_row(t):
    return jnp.minimum(t // TILES_PER_BATCH, B)


def _mod_spec(layer):
    return pl.BlockSpec((None, 1, 6 * D), lambda t: (layer * MOD_ROWS + _mod_row(t), 0, 0))


def _layer_norm(x, g, b):
    mu = jnp.mean(x, axis=-1, keepdims=True)
    xc = x - mu
    var = jnp.mean(xc * xc, axis=-1, keepdims=True)
    return xc * lax.rsqrt(var + LN_EPS) * g + b


def _log_sigmoid(z):
    return jnp.minimum(z, 0.0) - jnp.log(1.0 + jnp.exp(-jnp.abs(z)))


def _silu(x):
    half = 0.5 * x
    return half + half * jnp.tanh(half)


MOD_TN = 1536


def _mod_kernel(c_ref, w_ref, b_ref, o_ref):
    s = jax.nn.silu(c_ref[...])
    s_hi = s.astype(BF16)
    s_lo = (s - s_hi.astype(F32)).astype(BF16)
    w = w_ref[...].astype(BF16)
    o_ref[...] = _dot(s_hi, w) + _dot(s_lo, w) + b_ref[...]


def _modulation(cs, w_mod, b_mod):
    return pl.pallas_call(
        _mod_kernel,
        grid=(DEPTH, 6 * D // MOD_TN),
        in_specs=[
            pl.BlockSpec((MOD_ROWS, D), lambda i, n: (0, 0)),
            pl.BlockSpec((None, D, MOD_TN), lambda i, n: (i, 0, n)),
            pl.BlockSpec((None, 1, MOD_TN), lambda i, n: (i, 0, n)),
        ],
        out_specs=pl.BlockSpec((None, MOD_ROWS, MOD_TN), lambda i, n: (i, 0, n)),
        out_shape=jax.ShapeDtypeStruct((DEPTH, MOD_ROWS, 6 * D), F32),
        compiler_params=pltpu.CompilerParams(
            dimension_semantics=("arbitrary", "arbitrary"), vmem_limit_bytes=VMEM_LIMIT),
        name="modulation",
    )(cs, w_mod, b_mod.reshape(DEPTH, 1, 6 * D))


IN0_OFF = (0, 256, 512, 1024, 1536, 1792, 2048, 2560, 3072, 3104)


def _transpose_bf16(w):
    return w.astype(F32).T.astype(BF16)


def _in0_kernel(x_ref, ctx_ref, mod_ref, wT_ref, g2T, gbc,
                qa_o, kaT_o, va_o, ga_o, qb_o, kbT_o, vb_o, gb_o, lfT_o, lbT_o, wbT_s):
    t = pl.program_id(0)
    rows = lambda i: slice(IN0_OFF[i], IN0_OFF[i + 1])
    piece = lambda i: wbT_s[rows(i), :]

    @pl.when(t == 0)
    def _():
        for i in range(len(IN0_OFF) - 1):
            wbT_s[rows(i), :] = wT_ref[rows(i), :].astype(BF16)

    xt = jnp.where(t == N_LAT_TILES, ctx_ref[...], x_ref[...])
    sh1 = mod_ref[:, 0:D]
    sc1 = mod_ref[:, D:2 * D]
    h = (xt * (1.0 + sc1) + sh1).astype(BF16)
    qk_scale = HEAD_DK ** -0.5

    def put_slabs(o_ref, val):
        for i in range(TM // SP):
            o_ref[i] = val[:, i * SP:(i + 1) * SP].astype(o_ref.dtype)

    lrT = _dot_nt(piece(8), h).astype(BF16)
    ga_o[...] = _silu(_dot_nt(h, piece(3))).astype(BF16)
    lsT = _log_sigmoid(_dot(g2T[...], lrT) + gbc[...]) * (1.0 / GATE_TAU)
    put_slabs(lfT_o, lsT[0:QK_W, :])
    put_slabs(lbT_o, lsT[QK_W:2 * QK_W, :])
    gb_o[...] = _silu(_dot_nt(h, piece(7))).astype(BF16)

    put_slabs(kaT_o, _dot_nt(piece(1), h) * qk_scale)
    put_slabs(kbT_o, _dot_nt(piece(5), h))
    qa_o[...] = _dot_nt(h, piece(0)).astype(BF16)
    qb_o[...] = (_dot_nt(h, piece(4)) * qk_scale).astype(BF16)
    va_o[...] = _dot_nt(h, piece(2)).astype(BF16)
    vb_o[...] = _dot_nt(h, piece(6)).astype(BF16)


def _in_proj0(x2, ctx2, mods, w_inT, g2T, gbc):
    row = lambda t: (t, 0)
    slab = lambda t: (t, 0, 0)
    nsl = TM // SP
    row_out = lambda width: (jax.ShapeDtypeStruct((ROWS, width), BF16), pl.BlockSpec((TM, width), row))
    slab_out = lambda dt: (jax.ShapeDtypeStruct((N_SLABS, QK_W, SP), dt), pl.BlockSpec((nsl, QK_W, SP), slab))
    outs = [row_out(QK_W), slab_out(BF16), row_out(V_W), row_out(V_W),
            row_out(QK_W), slab_out(BF16), row_out(V_W), row_out(V_W),
            slab_out(BF16), slab_out(BF16)]
    in_specs = [
        pl.BlockSpec((TM, D), lambda t: (jnp.minimum(t, N_LAT_TILES - 1), 0)),
        _full_spec((B * L, D)),
        _mod_spec(0),
        _full_spec(w_inT.shape), _full_spec(g2T.shape), _full_spec(gbc.shape),
    ]
    return pl.pallas_call(
        _in0_kernel,
        grid=(N_TILES,),
        in_specs=in_specs,
        out_specs=[o[1] for o in outs],
        out_shape=[o[0] for o in outs],
        scratch_shapes=[pltpu.VMEM(w_inT.shape, BF16)],
        compiler_params=pltpu.CompilerParams(
            dimension_semantics=("arbitrary",), vmem_limit_bytes=VMEM_LIMIT),
        name="in_proj0",
    )(x2, ctx2, mods, w_inT, g2T, gbc)


def _chunk_diag(kv, c):
    r0 = c * 2 * HEAD_DK
    return jnp.concatenate([kv[r0:r0 + HEAD_DK, 0:HEAD_DV],
                            kv[r0 + HEAD_DK:r0 + 2 * HEAD_DK, HEAD_DV:2 * HEAD_DV]], axis=0)


def _scan_kernel(*refs, gla):
    if gla:
        (q_ref, kT_ref, v_ref, g_ref, lfT_ref, lbT_ref,
         qc_ref, kTc_ref, vc_ref, gc_ref, lfTc_ref, lbTc_ref, ng_ref, wsrc_ref,
         y_ref, yc_ref, wdst_ref, s_ref, r_ref, rst_ref, sst_ref, kv_ref, dec_ref,
         kd_ref, lhs_ref, pm_ref) = refs
        lat = (q_ref, kT_ref, v_ref, g_ref, lfT_ref, lbT_ref)
        cxt = (qc_ref, kTc_ref, vc_ref, gc_ref, lfTc_ref, lbTc_ref)
    else:
        (q_ref, kT_ref, v_ref, g_ref, qc_ref, kTc_ref, vc_ref, gc_ref, th_ref, wsrc_ref,
         y_ref, yc_ref, wdst_ref, s_ref, r_ref, rst_ref, sst_ref, kv_ref, dec_ref,
         kd_ref, lhs_ref, pm_ref) = refs
        lat = (q_ref, kT_ref, v_ref, g_ref, None, None)
        cxt = (qc_ref, kTc_ref, vc_ref, gc_ref, None, None)

    phase = pl.program_id(2)
    j = pl.program_id(3)

    wdst_ref[...] = wsrc_ref[...].astype(BF16)

    ri = lax.broadcasted_iota(jnp.int32, (SP, SP), 0)
    ci = lax.broadcasted_iota(jnp.int32, (SP, SP), 1)
    same = (ri // CH) == (ci // CH)
    first_lane = ci < CH
    head_a = ci < HEAD_DK

    if gla:
        as_w = lambda m: m.astype(BF16)
        tot = jnp.concatenate([jnp.broadcast_to(ri < CH, (SP, SP)), jnp.broadcast_to(ri >= CH, (SP, SP))], axis=1)
        w_end_f = jnp.concatenate([as_w(same & (ri > ci)), as_w(tot)], axis=1)
        w_end_b = jnp.concatenate([as_w(same & (ri < ci)), as_w(tot)], axis=1)
        w_cum_f = as_w(same & (ri <= ci))
        w_cum_b = as_w(same & (ri >= ci))
    else:
        hp = pl.program_id(1)
        th = [[th_ref[0, dr, 2 * hp + hd] for hd in range(2)] for dr in range(2)]
        lane1 = lax.broadcasted_iota(jnp.int32, (1, SP), 1) < HEAD_DK
        row1 = lax.broadcasted_iota(jnp.int32, (SP, 1), 0) < HEAD_DK
        lg_row = [jnp.log1p(-jnp.exp(jnp.where(lane1, th[dr][0], th[dr][1]))) for dr in range(2)]
        lg_col = [jnp.log1p(-jnp.exp(jnp.where(row1, th[dr][0], th[dr][1]))) for dr in range(2)]
        it = (ci % CH).astype(F32)
        ir = (ri % CH).astype(F32)
        ret_end_f = jnp.exp((CH - 1.0 - it) * lg_col[0])
        ret_end_b = jnp.exp(it * lg_col[1])
        ret_dec_f = jnp.exp(jnp.broadcast_to(CH * lg_col[0], (SP, SP)))
        ret_dec_b = jnp.exp(jnp.broadcast_to(CH * lg_col[1], (SP, SP)))
        ret_ebTi = jnp.exp(-(it + 1.0) * lg_col[0])
        ret_erTi = jnp.exp(-(CH - it) * lg_col[1])
        ret_eb = jnp.exp((ir + 1.0) * lg_row[0])
        ret_er = jnp.exp((CH - ir) * lg_row[1])

    def kv_stage(blk, n, fwd):
        _, kT_r, v_r, _, lfT_r, lbT_r = blk

        def body(p, carry):
            kT = kT_r[p].astype(F32)
            v = v_r[pl.ds(pl.multiple_of(p * SP, SP), SP), :]
            if gla:
                res = _dot((lfT_r if fwd else lbT_r)[p], w_end_f if fwd else w_end_b)
                e_end = jnp.exp(res[:, 0:SP])
                dec0 = jnp.exp(res[:, SP:2 * SP])
                dec1 = jnp.exp(res[:, 2 * SP:3 * SP])
            else:
                e_end = ret_end_f if fwd else ret_end_b
                dec0 = dec1 = ret_dec_f if fwd else ret_dec_b
            ke = kT * e_end
            lhs = jnp.concatenate([jnp.where(first_lane, ke, 0.0), jnp.where(first_lane, 0.0, ke)],
                                  axis=0).astype(BF16)
            kv = _dot(lhs, v)
            kv_ref[2 * p] = _chunk_diag(kv, 0)
            kv_ref[2 * p + 1] = _chunk_diag(kv, 1)
            dec_ref[2 * p] = dec0
            dec_ref[2 * p + 1] = dec1
            return carry

        lax.fori_loop(0, n, body, 0, unroll=min(n, KV_UNROLL))

    def phase0_block(blk, n, slot0):
        kv_stage(blk, n, fwd=False)

        def body(i, r_state):
            p = n - 1 - i
            rst_ref[slot0 + p, :, HEAD_DV:2 * HEAD_DV] = r_state.astype(BF16)
            r_state = dec_ref[2 * p + 1] * r_state + kv_ref[2 * p + 1]
            rst_ref[slot0 + p, :, 0:HEAD_DV] = r_state.astype(BF16)
            return dec_ref[2 * p] * r_state + kv_ref[2 * p]

        r_ref[...] = lax.fori_loop(0, n, body, r_ref[...])

    def phase1_block(blk, n, slot0, out_ref):
        q_r, kT_r, v_r, g_r, lfT_r, lbT_r = blk
        kv_stage(blk, n, fwd=True)

        def rec(p, s_state):
            sst_ref[p, :, 0:HEAD_DV] = s_state.astype(BF16)
            s_state = dec_ref[2 * p] * s_state + kv_ref[2 * p]
            sst_ref[p, :, HEAD_DV:2 * HEAD_DV] = s_state.astype(BF16)
            return dec_ref[2 * p + 1] * s_state + kv_ref[2 * p + 1]

        s_ref[...] = lax.fori_loop(0, n, rec, s_ref[...])

        r2 = lax.broadcasted_iota(jnp.int32, (2 * SP, SP), 0) % SP
        c2 = lax.broadcasted_iota(jnp.int32, (2 * SP, SP), 1)
        same2 = (r2 // CH) == (c2 // CH)
        mask_f = same2 & (r2 >= c2)
        mask_b = same2 & (r2 < c2)

        def prep(p, carry):
            rows = pl.ds(pl.multiple_of(p * SP, SP), SP)
            q = q_r[rows, :].astype(F32)
            kT = kT_r[p].astype(F32)
            if gla:
                lfT = lfT_r[p]
                lbT = lbT_r[p]
                ebTi = jnp.exp(-_dot(lfT, w_cum_f))
                erTi = jnp.exp(-_dot(lbT, w_cum_b))
                e_b = jnp.exp(_dot_nt(w_cum_b, lfT))
                e_r = jnp.exp(_dot_nt(w_cum_f, lbT))
            else:
                ebTi, erTi, e_b, e_r = ret_ebTi, ret_erTi, ret_eb, ret_er
            kd_ref[p, 0:SP, 0:SP] = (kT * ebTi).astype(BF16)
            kd_ref[p, SP:2 * SP, SP:2 * SP] = (kT * erTi).astype(BF16)
            qf = q * e_b
            qb = q * e_r
            lhs_ref[p, :, 0:SP] = jnp.concatenate(
                [jnp.where(head_a, qf, 0.0), jnp.where(head_a, 0.0, qf)], axis=0).astype(BF16)
            lhs_ref[p, :, SP:2 * SP] = jnp.concatenate(
                [jnp.where(head_a, qb, 0.0), jnp.where(head_a, 0.0, qb)], axis=0).astype(BF16)
            return carry

        def score(p, carry):
            sc = _dot(lhs_ref[p], kd_ref[p])
            pm_ref[p] = jnp.where(mask_f, sc[:, 0:SP], jnp.where(mask_b, sc[:, SP:2 * SP], 0.0)).astype(BF16)
            return carry

        def emit(p, carry):
            rows = pl.ds(pl.multiple_of(p * SP, SP), SP)
            v = v_r[rows, :]
            states = jnp.concatenate([sst_ref[p], rst_ref[slot0 + p]], axis=0)
            o_int = _dot(lhs_ref[p], states)
            o_a = _dot(pm_ref[p, 0:SP, :], v[:, 0:HEAD_DV]) + jnp.concatenate(
                [o_int[0:CH, 0:HEAD_DV], o_int[CH:SP, HEAD_DV:2 * HEAD_DV]], axis=0)
            o_b = _dot(pm_ref[p, SP:2 * SP, :], v[:, HEAD_DV:2 * HEAD_DV]) + jnp.concatenate(
                [o_int[SP:SP + CH, 0:HEAD_DV], o_int[SP + CH:2 * SP, HEAD_DV:2 * HEAD_DV]], axis=0)

            def nrm(o):
                y = o * lax.rsqrt(jnp.mean(o * o, axis=-1, keepdims=True) + RMS_EPS)
                return y * ng_ref[...] if gla else y

            y = jnp.concatenate([nrm(o_a), nrm(o_b)], axis=1) * g_r[rows, :].astype(F32)
            out_ref[rows, :] = y.astype(BF16)
            return carry

        lax.fori_loop(0, n, prep, 0, unroll=min(n, OUT_UNROLL))
        lax.fori_loop(0, n, score, 0, unroll=min(n, OUT_UNROLL))
        lax.fori_loop(0, n, emit, 0, unroll=min(n, OUT_UNROLL))

    @pl.when((pl.program_id(0) == 0) & (pl.program_id(1) == 0) & (phase == 0) & (j == 0))
    def _():
        kd_ref[...] = jnp.zeros_like(kd_ref)

    @pl.when(phase == 0)
    def _():
        @pl.when(j == 0)
        def _():
            r_ref[...] = jnp.zeros_like(r_ref)
            phase0_block(cxt, NP_CTX, 0)

        phase0_block(lat, NP, NP_CTX + (NBLK - 1 - j) * NP)

    @pl.when(phase == 1)
    def _():
        @pl.when(j == 0)
        def _():
            s_ref[...] = jnp.zeros_like(s_ref)
            phase1_block(cxt, NP_CTX, 0, yc_ref)

        phase1_block(lat, NP, NP_CTX + j * NP, y_ref)


SCAN_STEPS = B * 2 * 2 * NBLK


def _scan_group(q, kT, v, g, extra, wsrc, *, gla):
    w_spec = pl.BlockSpec((None,) + wsrc.shape[1:],
                          lambda b, p, ph, j: (((b * 2 + p) * 2 + ph) * NBLK + j, 0, 0))

    def blk(b, ph, j, used_in_phase0):
        jj = jnp.where(ph == 0, NBLK - 1 - j, j)
        if not used_in_phase0:
            jj = jnp.where(ph == 0, 0, jj)
        return b * NBLK + jj

    def lat_specs(used0):
        return dict(
            row=lambda w: pl.BlockSpec((TB, w), lambda b, p, ph, j: (blk(b, ph, j, used0), p)),
            slab=pl.BlockSpec((NP, SP, SP), lambda b, p, ph, j: (blk(b, ph, j, used0), p, 0)))

    ctx_row = lambda w: pl.BlockSpec((L, w), lambda b, p, ph, j: (LAT_ROWS // L + b, p))
    ctx_slab = pl.BlockSpec((NP_CTX, SP, SP), lambda b, p, ph, j: (LAT_ROWS // L + b, p, 0))
    used, unused = lat_specs(True), lat_specs(False)

    in_specs = [unused["row"](2 * HEAD_DK), used["slab"], used["row"](2 * HEAD_DV), unused["row"](2 * HEAD_DV)]
    ctx_specs = [ctx_row(2 * HEAD_DK), ctx_slab, ctx_row(2 * HEAD_DV), ctx_row(2 * HEAD_DV)]
    if gla:
        lfT, lbT, ng = extra
        in_specs += [unused["slab"], used["slab"]]
        ctx_specs += [ctx_slab, ctx_slab]
        args = (q, kT, v, g, lfT, lbT, q, kT, v, g, lfT, lbT, ng, wsrc)
        in_specs = in_specs + ctx_specs + [pl.BlockSpec((1, HEAD_DV), lambda b, p, ph, j: (0, 0)), w_spec]
    else:
        (theta,) = extra
        args = (q, kT, v, g, q, kT, v, g, theta, wsrc)
        in_specs = in_specs + ctx_specs + [pl.BlockSpec(memory_space=pltpu.SMEM), w_spec]
    return pl.pallas_call(
        functools.partial(_scan_kernel, gla=gla),
        grid=(B, 2, 2, NBLK),
        in_specs=in_specs,
        out_specs=[
            pl.BlockSpec((TB, 2 * HEAD_DV), lambda b, p, ph, j: (b * NBLK + jnp.where(ph == 0, 0, j), p)),
            pl.BlockSpec((L, 2 * HEAD_DV), lambda b, p, ph, j: (b, p)),
            w_spec,
        ],
        out_shape=[jax.ShapeDtypeStruct((LAT_ROWS, V_W), BF16), jax.ShapeDtypeStruct((B * L, V_W), BF16),
                   jax.ShapeDtypeStruct(wsrc.shape, BF16)],
        scratch_shapes=[
            pltpu.VMEM((SP, HEAD_DV), F32),
            pltpu.VMEM((SP, HEAD_DV), F32),
            pltpu.VMEM((NP_ALL, SP, 2 * HEAD_DV), BF16),
            pltpu.VMEM((NP, SP, 2 * HEAD_DV), BF16),
            pltpu.VMEM((2 * NP, SP, HEAD_DV), F32),
            pltpu.VMEM((2 * NP, SP, HEAD_DV), F32),
            pltpu.VMEM((NP, 2 * SP, 2 * SP), BF16),
            pltpu.VMEM((NP, 2 * SP, 2 * SP), BF16),
            pltpu.VMEM((NP, 2 * SP, SP), BF16),
        ],
        compiler_params=pltpu.CompilerParams(
            dimension_semantics=("arbitrary",) * 4, vmem_limit_bytes=VMEM_LIMIT),
        name="scan_gla" if gla else "scan_ret",
    )(*args)


FF_CH = 512


def _out_kernel(*refs, layer, split_ctx):
    if split_ctx:
        (ya_ref, yb_ref, yac_ref, ybc_ref, x_ref, ctx_ref, mod_ref, lng_ref, lnb_ref,
         wo_ref, w1_ref, w2_ref, o_ref, wo_s) = refs
        is_ctx = pl.program_id(0) == N_LAT_TILES
        x = jnp.where(is_ctx, ctx_ref[...], x_ref[...])
        ya = jnp.where(is_ctx, yac_ref[...], ya_ref[...])
        yb = jnp.where(is_ctx, ybc_ref[...], yb_ref[...])
    else:
        ya_ref, yb_ref, x_ref, mod_ref, lng_ref, lnb_ref, wo_ref, w1_ref, w2_ref, o_ref, wo_s = refs
        x = x_ref[...]
        ya = ya_ref[...]
        yb = yb_ref[...]
    g1 = mod_ref[:, 2 * D:3 * D]
    sh2 = mod_ref[:, 3 * D:4 * D]
    sc2 = mod_ref[:, 4 * D:5 * D]
    g2 = mod_ref[:, 5 * D:6 * D]
    ln_g0 = lng_ref[2 * layer:2 * layer + 1, :]
    ln_g1 = lng_ref[2 * layer + 1:2 * layer + 2, :]
    ln_b0 = lnb_ref[2 * layer:2 * layer + 1, :]
    ln_b1 = lnb_ref[2 * layer + 1:2 * layer + 2, :]

    @pl.when(pl.program_id(0) == 0)
    def _():
        wo_s[...] = wo_ref[...].astype(BF16)

    half = D // 2
    y = _dot(ya, wo_s[0:half, :]) + _dot(yb, wo_s[half:D, :])
    x1 = _layer_norm(ALPHA * x + g1 * y, ln_g0, ln_b0)
    h2 = (x1 * (1.0 + sc2) + sh2).astype(BF16)
    acc = jnp.zeros((TM, D), F32)
    for c in range(D_FF // FF_CH):
        cols = slice(c * FF_CH, (c + 1) * FF_CH)
        hc = jnp.maximum(_dot(h2, w1_ref[:, cols]), 0.0)
        acc = acc + _dot((hc * hc).astype(BF16), w2_ref[cols, :])
    o_ref[...] = _layer_norm(ALPHA * x1 + g2 * acc, ln_g1, ln_b1)


def _out_mlp(ys, xs, ctx2, mods, ln, wo, w1, w2, *, layer, n_tiles, split_ctx):
    half = D // 2
    lat_row = lambda t: (jnp.minimum(t, N_LAT_TILES - 1), 0)
    if split_ctx:
        ya, yb, yac, ybc = ys
        in_specs = [pl.BlockSpec((TM, half), lat_row), pl.BlockSpec((TM, half), lat_row),
                    _full_spec((B * L, half)), _full_spec((B * L, half)),
                    pl.BlockSpec((TM, D), lat_row), _full_spec((B * L, D))]
        args = [ya, yb, yac, ybc, xs, ctx2]
    else:
        (att,) = ys
        in_specs = [pl.BlockSpec((TM, half), lambda t: (t, 0)), pl.BlockSpec((TM, half), lambda t: (t, 1)),
                    pl.BlockSpec((TM, D), lambda t: (t, 0))]
        args = [att, att, xs]
    stacked = lambda w, i: pl.BlockSpec((None,) + w.shape[1:], lambda t: (i, 0, 0), pipeline_mode=pl.Buffered(1))
    in_specs += [_mod_spec(layer), _full_spec(ln[0].shape), _full_spec(ln[1].shape),
                 stacked(wo, 0), stacked(w1, layer), stacked(w2, layer)]
    args += [mods, ln[0], ln[1], wo, w1, w2]
    return pl.pallas_call(
        functools.partial(_out_kernel, layer=layer, split_ctx=split_ctx),
        grid=(n_tiles,),
        in_specs=in_specs,
        out_specs=pl.BlockSpec((TM, D), lambda t: (t, 0)),
        out_shape=jax.ShapeDtypeStruct((n_tiles * TM, D), F32),
        compiler_params=pltpu.CompilerParams(
            dimension_semantics=("arbitrary",), vmem_limit_bytes=VMEM_LIMIT),
        scratch_shapes=[pltpu.VMEM((D, D), BF16)],
        name="out_mlp%d" % layer,
    )(*args)


def _in1_kernel(x_ref, mod_ref, cos_ref, sin_ref, w_ref, qT_o, k_o, vT_o, wqT_s, wk_s, wvT_s):
    @pl.when(pl.program_id(0) == 0)
    def _():
        wqT_s[...] = _transpose_bf16(w_ref[:, 0:D])
        wk_s[...] = w_ref[:, D:D + ATT_KVW].astype(BF16)
        wvT_s[...] = _transpose_bf16(w_ref[:, D + ATT_KVW:D + 2 * ATT_KVW])

    sh1 = mod_ref[:, 0:D]
    sc1 = mod_ref[:, D:2 * D]
    h = (x_ref[...] * (1.0 + sc1) + sh1).astype(BF16)
    nsl = TM // SP

    cos = cos_ref[...]
    sin = sin_ref[...]
    first = (lax.broadcasted_iota(jnp.int32, (TM, LANES), 1) % (2 * ROT)) < ROT
    sa = jnp.where(first, -sin, 0.0)
    sb = jnp.where(first, 0.0, sin)

    k = _dot(h, wk_s[...])
    for i in range(ATT_KVW // LANES):
        u = k[:, i * LANES:(i + 1) * LANES]
        r = u * cos + pltpu.roll(u, LANES - ROT, 1) * sa + pltpu.roll(u, ROT, 1) * sb
        k_o[:, i * LANES:(i + 1) * LANES] = r.astype(BF16)

    cosT, saT, sbT = cos.T, sa.T, sb.T
    qT = _dot_nt(wqT_s[...], h) * (LOG2E * ATT_DH ** -0.5)
    for i in range(D // LANES):
        u = qT[i * LANES:(i + 1) * LANES, :]
        r = (u * cosT + pltpu.roll(u, LANES - ROT, 0) * saT + pltpu.roll(u, ROT, 0) * sbT).astype(BF16)
        for s in range(nsl):
            qT_o[s, i * LANES:(i + 1) * LANES, :] = r[:, s * SP:(s + 1) * SP]

    vT = _dot_nt(wvT_s[...], h).astype(BF16)
    for s in range(nsl):
        vT_o[s] = vT[:, s * SP:(s + 1) * SP]


def _in_proj1(xs, mods, cos_tab, sin_tab, wqkv):
    tile = lambda t: jnp.where(t == N_LAT_TILES, TILES_PER_BATCH, t % TILES_PER_BATCH)
    row = lambda t: (t, 0)
    slab = lambda t: (t, 0, 0)
    nsl = TM // SP
    tab_spec = pl.BlockSpec((TM, LANES), lambda t: (tile(t), 0))
    return pl.pallas_call(
        _in1_kernel,
        grid=(N_TILES,),
        in_specs=[pl.BlockSpec((TM, D), row), _mod_spec(1), tab_spec, tab_spec,
                  pl.BlockSpec((None,) + wqkv.shape[1:], lambda t: (0, 0, 0), pipeline_mode=pl.Buffered(1))],
        out_specs=[pl.BlockSpec((nsl, D, SP), slab), pl.BlockSpec((TM, ATT_KVW), row),
                   pl.BlockSpec((nsl, ATT_KVW, SP), slab)],
        out_shape=[jax.ShapeDtypeStruct((N_SLABS, D, SP), BF16), jax.ShapeDtypeStruct((ROWS, ATT_KVW), BF16),
                   jax.ShapeDtypeStruct((N_SLABS, ATT_KVW, SP), BF16)],
        scratch_shapes=[pltpu.VMEM((D, D), BF16), pltpu.VMEM((D, ATT_KVW), BF16), pltpu.VMEM((ATT_KVW, D), BF16)],
        compiler_params=pltpu.CompilerParams(
            dimension_semantics=("arbitrary",), vmem_limit_bytes=VMEM_LIMIT),
        name="in_proj1",
    )(xs, mods, cos_tab, sin_tab, wqkv)


def _attn_kernel(sink_ref, qT_ref, k_ref, vT_ref, kc_ref, vcT_ref, o_ref, bias_ref):
    n = pl.program_id(1)
    nql = ATT_GROUP * QB
    lane_g = lax.broadcasted_iota(jnp.int32, (1, nql), 1) // QB
    zero_half = jnp.zeros((ATT_DH, nql), BF16)
    ones_rows = jnp.ones((BF16_ROWS, KWIN + L), BF16)

    @pl.when((pl.program_id(0) == 0) & (n == 0))
    def _():
        kj = lax.broadcasted_iota(jnp.int32, (KWIN, QB), 0)
        qi = lax.broadcasted_iota(jnp.int32, (KWIN, QB), 1)
        for case, delta in enumerate((-QB, 0, -2 * QB)):
            d = kj - qi + delta
            bias_ref[case] = jnp.where((d >= -WINDOW) & (d <= WINDOW), 0.0, -jnp.inf)

    def scores(i, kh):
        n0 = (n * (TQ // QB) + i) * QB
        start = pl.multiple_of(jnp.clip(n0 - QB, 0, T - KWIN), QB)
        bias1 = bias_ref[jnp.where(n0 == 0, 1, jnp.where(n0 == T - QB, 2, 0))]
        bias = jnp.concatenate([bias1] * ATT_GROUP, axis=1)
        pair = slice((kh // 2) * LANES, (kh // 2 + 1) * LANES)
        qT = jnp.concatenate(
            [qT_ref[i, (kh * ATT_GROUP + g) * ATT_DH:(kh * ATT_GROUP + g + 1) * ATT_DH, :]
             for g in range(ATT_GROUP)], axis=1)
        q_pad = jnp.concatenate([zero_half, qT] if kh % 2 else [qT, zero_half], axis=0)
        s_loc = _dot(k_ref[pl.ds(start, KWIN), pair], q_pad) + bias
        s_ctx = _dot(kc_ref[:, pair], q_pad)
        return s_loc, s_ctx, start // SP

    def finish(i, kh, s_loc, s_ctx, slab0):
        drows = slice(kh * ATT_DH, (kh + 1) * ATT_DH)
        sink = jnp.zeros((1, nql), F32)
        for g in range(ATT_GROUP):
            sink = jnp.where(lane_g == g, sink_ref[0, kh * ATT_GROUP + g] * LOG2E, sink)
        m = jnp.maximum(jnp.maximum(jnp.max(s_loc, axis=0, keepdims=True),
                                    jnp.max(s_ctx, axis=0, keepdims=True)), sink)
        pT = jnp.concatenate([jnp.exp2(s_loc - m).astype(BF16),
                              jnp.exp2(s_ctx - m).astype(BF16)], axis=0)
        vT = jnp.concatenate([vT_ref[slab0 + t, drows, :] for t in range(KWIN // SP)]
                             + [vcT_ref[t, drows, :] for t in range(L // SP)], axis=1)
        o_ext = _dot(jnp.concatenate([vT, ones_rows], axis=0), pT)
        inv = 1.0 / (o_ext[ATT_DH:ATT_DH + 1, :] + jnp.exp2(sink - m))
        oT = o_ext[0:ATT_DH, :] * inv
        for g in range(0, ATT_GROUP, 2):
            two = jnp.concatenate([oT[:, g * QB:(g + 1) * QB], oT[:, (g + 1) * QB:(g + 2) * QB]], axis=0)
            c0 = (kh * ATT_GROUP + g) * ATT_DH
            o_ref[i * QB:(i + 1) * QB, c0:c0 + 2 * ATT_DH] = two.T.astype(BF16)

    items = [(i, kh) for i in range(TQ // QB) for kh in range(ATT_KVH)]
    pending = scores(*items[0])
    for t, item in enumerate(items):
        nxt = scores(*items[t + 1]) if t + 1 < len(items) else None
        finish(*item, *pending)
        pending = nxt


def _attention(sink, qT, k, vT):
    nq = T // TQ
    return pl.pallas_call(
        _attn_kernel,
        grid=(B, nq),
        in_specs=[
            pl.BlockSpec(memory_space=pltpu.SMEM),
            pl.BlockSpec((TQ // SP, D, SP), lambda b, n: (b * nq + n, 0, 0)),
            pl.BlockSpec((T, ATT_KVW), lambda b, n: (b, 0)),
            pl.BlockSpec((T // SP, ATT_KVW, SP), lambda b, n: (b, 0, 0)),
            pl.BlockSpec((L, ATT_KVW), lambda b, n: (LAT_ROWS // L + b, 0)),
            pl.BlockSpec((L // SP, ATT_KVW, SP), lambda b, n: (LAT_ROWS // L + b, 0, 0)),
        ],
        out_specs=pl.BlockSpec((TQ, D), lambda b, n: (b * nq + n, 0)),
        out_shape=jax.ShapeDtypeStruct((LAT_ROWS, D), BF16),
        compiler_params=pltpu.CompilerParams(
            dimension_semantics=("arbitrary", "arbitrary"), vmem_limit_bytes=VMEM_LIMIT),
        scratch_shapes=[pltpu.VMEM((3, KWIN, QB), F32)],
        name="window_attn",
    )(sink, qT, k, vT, k, vT)


def _rope_tables():
    half = ATT_DH // 2
    inv_freq = np.power(np.float32(ROPE_BASE), -np.arange(0, half, 2, dtype=np.float32) / np.float32(half))
    inv_freq = inv_freq.astype(np.float32)
    pos = np.arange(T)
    ang_r = (pos // GRID_W).astype(np.float32)[:, None] * inv_freq[None, :]
    ang_c = (pos % GRID_W).astype(np.float32)[:, None] * inv_freq[None, :]
    ang = np.concatenate([ang_r, ang_r, ang_c, ang_c], axis=1)
    ang = np.concatenate([ang, np.zeros((TM, ATT_DH), np.float32)], axis=0)
    ang = np.concatenate([ang, ang], axis=1)
    return np.cos(ang).astype(np.float32), np.sin(ang).astype(np.float32)


def kernel(x, c, ctx, c_ctx, w_mod, b_mod, ln_g, ln_b, mlp_w1, mlp_w2, ev_w_in, ev_ret_theta, ev_gla_gk_w,
           ev_gla_gk_b, ev_gla_norm_g, ev_w_out, od_w_qkv, od_sink, od_w_out):
    x2 = x.reshape(LAT_ROWS, D)
    ctx2 = ctx.reshape(B * L, D)

    cs = jnp.concatenate([c, c_ctx[None, :], jnp.zeros((MOD_ROWS - B - 1, D), F32)], axis=0)
    mods = _modulation(cs, w_mod, b_mod).reshape(DEPTH * MOD_ROWS, 1, 6 * D)
    ln = (ln_g.reshape(2 * DEPTH, D), ln_b.reshape(2 * DEPTH, D))

    gk_w = ev_gla_gk_w[0]
    zeros = jnp.zeros((GATE_RANK, QK_W), F32)
    g2 = jnp.concatenate([jnp.concatenate([gk_w[0], zeros], axis=1),
                          jnp.concatenate([zeros, gk_w[1]], axis=1)], axis=0)
    qa, kaT, va, ga, qb, kbT, vb, gbv, lfT, lbT = _in_proj0(
        x2, ctx2, mods, ev_w_in[0].T, g2.T.astype(BF16), ev_gla_gk_b[0].reshape(2 * QK_W, 1))

    w1_rows = DEPTH * D // SCAN_STEPS
    w2_rows = DEPTH * D_FF // SCAN_STEPS
    y_ret, yc_ret, w1 = _scan_group(qa, kaT, va, ga, (ev_ret_theta,),
                                    mlp_w1.reshape(SCAN_STEPS, w1_rows, D_FF), gla=False)
    y_gla, yc_gla, w2 = _scan_group(qb, kbT, vb, gbv, (lfT, lbT, ev_gla_norm_g),
                                    mlp_w2.reshape(SCAN_STEPS, w2_rows, D), gla=True)
    w1 = w1.reshape(DEPTH, D, D_FF)
    w2 = w2.reshape(DEPTH, D_FF, D)

    xs = _out_mlp((y_ret, y_gla, yc_ret, yc_gla), x2, ctx2, mods, ln, ev_w_out, w1, w2,
                  layer=0, n_tiles=N_TILES, split_ctx=True)

    cos_tab, sin_tab = _rope_tables()
    q1T, k1, v1T = _in_proj1(xs, mods, cos_tab, sin_tab, od_w_qkv)
    att = _attention(od_sink, q1T, k1, v1T)
    out = _out_mlp((att,), xs, None, mods, ln, od_w_out, w1, w2,
                   layer=1, n_tiles=N_LAT_TILES, split_ctx=False)
    return out.reshape(B, T, D)
```

```python
import functools

import jax
import jax.numpy as jnp
import numpy as np
from jax import lax
from jax.experimental import pallas as pl
from jax.experimental.pallas import tpu as pltpu

F32 = jnp.float32
BF16 = jnp.bfloat16

D = 1024
B = 2
T = 8192
L = 256
DEPTH = 2
GRID_W = 64
D_FF = 4 * D
HEAD_DV = 128
HEAD_DK = 64
GATE_RANK = 16
GATE_TAU = 16.0
QK_W = 256
V_W = 512
ATT_DH = 64
ATT_QH = 16
ATT_KVH = 4
ATT_GROUP = 4
ATT_KVW = ATT_KVH * ATT_DH
WINDOW = 128
ROPE_BASE = 10000.0
ALPHA = (2.0 * DEPTH) ** 0.25
LN_EPS = 1e-5
RMS_EPS = 1e-6

LANES = 128
SUBLANES = 8
BF16_ROWS = 2 * SUBLANES
MOD_ROWS = SUBLANES
ROT = ATT_DH // 4

TM = 512
LAT_ROWS = B * T
ROWS = LAT_ROWS + B * L
N_LAT_TILES = LAT_ROWS // TM
N_TILES = ROWS // TM
TILES_PER_BATCH = T // TM

CH = 64
SP = 2 * CH
TB = 4096
NP = TB // SP
NP_CTX = L // SP
NBLK = T // TB
NP_ALL = NP_CTX + T // SP
N_SLABS = ROWS // SP
assert SP == LANES and 2 * HEAD_DK == LANES and HEAD_DV == LANES
KV_UNROLL = 32
OUT_UNROLL = 32

TQ = 1024
QB = 128
KWIN = 3 * QB
SCORE_AHEAD = 2
LOG2E = 1.4426950408889634

VMEM_LIMIT = 56 * 1024 * 1024


def _dot(a, b):
    return jnp.dot(a, b, preferred_element_type=F32)


def _dot_nt(a, b):
    return lax.dot_general(a, b, (((1,), (1,)), ((), ())), preferred_element_type=F32)


def _full_spec(shape):
    nd = len(shape)
    return pl.BlockSpec(shape, lambda *_: (0,) * nd, pipeline_mode=pl.Buffered(1))


def _mod_row(t):
    return jnp.minimum(t // TILES_PER_BATCH, B)


def _mod_spec(layer):
    return pl.BlockSpec((None, 1, 6 * D), lambda t: (layer * MOD_ROWS + _mod_row(t), 0, 0))


def _layer_norm(x, g, b):
    mu = jnp.mean(x, axis=-1, keepdims=True)
    xc = x - mu
    var = jnp.mean(xc * xc, axis=-1, keepdims=True)
    return xc * lax.rsqrt(var + LN_EPS) * g + b


def _log_sigmoid(z):
    return jnp.minimum(z, 0.0) - jnp.log(1.0 + jnp.exp(-jnp.abs(z)))


def _silu(x):
    half = 0.5 * x
    return half + half * jnp.tanh(half)


MOD_TN = 1536


def _mod_kernel(c_ref, w_ref, b_ref, o_ref):
    s = jax.nn.silu(c_ref[...])
    s_hi = s.astype(BF16)
    s_lo = (s - s_hi.astype(F32)).astype(BF16)
    w = w_ref[...].astype(BF16)
    o_ref[...] = _dot(s_hi, w) + _dot(s_lo, w) + b_ref[...]


def _modulation(cs, w_mod, b_mod):
    return pl.pallas_call(
        _mod_kernel,
        grid=(DEPTH, 6 * D // MOD_TN),
        in_specs=[
            pl.BlockSpec((MOD_ROWS, D), lambda i, n: (0, 0)),
            pl.BlockSpec((None, D, MOD_TN), lambda i, n: (i, 0, n)),
            pl.BlockSpec((None, 1, MOD_TN), lambda i, n: (i, 0, n)),
        ],
        out_specs=pl.BlockSpec((None, MOD_ROWS, MOD_TN), lambda i, n: (i, 0, n)),
        out_shape=jax.ShapeDtypeStruct((DEPTH, MOD_ROWS, 6 * D), F32),
        compiler_params=pltpu.CompilerParams(
            dimension_semantics=("arbitrary", "arbitrary"), vmem_limit_bytes=VMEM_LIMIT),
        name="modulation",
    )(cs, w_mod, b_mod.reshape(DEPTH, 1, 6 * D))


IN0_OFF = (0, 256, 512, 1024, 1536, 1792, 2048, 2560, 3072, 3104)


def _transpose_bf16(w):
    return w.astype(F32).T.astype(BF16)


def _in0_kernel(x_ref, ctx_ref, mod_ref, wT_ref, g2T, gbc,
                qa_o, kaT_o, va_o, ga_o, qb_o, kbT_o, vb_o, gb_o, lfT_o, lbT_o, wbT_s):
    t = pl.program_id(0)
    rows = lambda i: slice(IN0_OFF[i], IN0_OFF[i + 1])
    piece = lambda i: wbT_s[rows(i), :]

    @pl.when(t == 0)
    def _():
        for i in range(len(IN0_OFF) - 1):
            wbT_s[rows(i), :] = wT_ref[rows(i), :].astype(BF16)

    xt = jnp.where(t == N_LAT_TILES, ctx_ref[...], x_ref[...])
    sh1 = mod_ref[:, 0:D]
    sc1 = mod_ref[:, D:2 * D]
    h = (xt * (1.0 + sc1) + sh1).astype(BF16)
    qk_scale = HEAD_DK ** -0.5

    def put_slabs(o_ref, val):
        for i in range(TM // SP):
            o_ref[i] = val[:, i * SP:(i + 1) * SP].astype(o_ref.dtype)

    lrT = _dot_nt(piece(8), h).astype(BF16)
    ga_o[...] = _silu(_dot_nt(h, piece(3))).astype(BF16)
    lsT = _log_sigmoid(_dot(g2T[...], lrT) + gbc[...]) * (1.0 / GATE_TAU)
    put_slabs(lfT_o, lsT[0:QK_W, :])
    put_slabs(lbT_o, lsT[QK_W:2 * QK_W, :])
    gb_o[...] = _silu(_dot_nt(h, piece(7))).astype(BF16)

    put_slabs(kaT_o, _dot_nt(piece(1), h) * qk_scale)
    put_slabs(kbT_o, _dot_nt(piece(5), h))
    qa_o[...] = _dot_nt(h, piece(0)).astype(BF16)
    qb_o[...] = (_dot_nt(h, piece(4)) * qk_scale).astype(BF16)
    va_o[...] = _dot_nt(h, piece(2)).astype(BF16)
    vb_o[...] = _dot_nt(h, piece(6)).astype(BF16)


def _in_proj0(x2, ctx2, mods, w_inT, g2T, gbc):
    row = lambda t: (t, 0)
    slab = lambda t: (t, 0, 0)
    nsl = TM // SP
    row_out = lambda width: (jax.ShapeDtypeStruct((ROWS, width), BF16), pl.BlockSpec((TM, width), row))
    slab_out = lambda dt: (jax.ShapeDtypeStruct((N_SLABS, QK_W, SP), dt), pl.BlockSpec((nsl, QK_W, SP), slab))
    outs = [row_out(QK_W), slab_out(BF16), row_out(V_W), row_out(V_W),
            row_out(QK_W), slab_out(BF16), row_out(V_W), row_out(V_W),
            slab_out(BF16), slab_out(BF16)]
    in_specs = [
        pl.BlockSpec((TM, D), lambda t: (jnp.minimum(t, N_LAT_TILES - 1), 0)),
        _full_spec((B * L, D)),
        _mod_spec(0),
        _full_spec(w_inT.shape), _full_spec(g2T.shape), _full_spec(gbc.shape),
    ]
    return pl.pallas_call(
        _in0_kernel,
        grid=(N_TILES,),
        in_specs=in_specs,
        out_specs=[o[1] for o in outs],
        out_shape=[o[0] for o in outs],
        scratch_shapes=[pltpu.VMEM(w_inT.shape, BF16)],
        compiler_params=pltpu.CompilerParams(
            dimension_semantics=("arbitrary",), vmem_limit_bytes=VMEM_LIMIT),
        name="in_proj0",
    )(x2, ctx2, mods, w_inT, g2T, gbc)


def _chunk_diag(kv, c):
    r0 = c * 2 * HEAD_DK
    return jnp.concatenate([kv[r0:r0 + HEAD_DK, 0:HEAD_DV],
                            kv[r0 + HEAD_DK:r0 + 2 * HEAD_DK, HEAD_DV:2 * HEAD_DV]], axis=0)


def _scan_kernel(*refs, gla):
    if gla:
        (q_ref, kT_ref, v_ref, g_ref, lfT_ref, lbT_ref,
         qc_ref, kTc_ref, vc_ref, gc_ref, lfTc_ref, lbTc_ref, ng_ref, wsrc_ref,
         y_ref, yc_ref, wdst_ref, s_ref, r_ref, rst_ref, sst_ref, kv_ref, dec_ref,
         kd_ref, lhs_ref, pm_ref) = refs
        lat = (q_ref, kT_ref, v_ref, g_ref, lfT_ref, lbT_ref)
        cxt = (qc_ref, kTc_ref, vc_ref, gc_ref, lfTc_ref, lbTc_ref)
    else:
        (q_ref, kT_ref, v_ref, g_ref, qc_ref, kTc_ref, vc_ref, gc_ref, th_ref, wsrc_ref,
         y_ref, yc_ref, wdst_ref, s_ref, r_ref, rst_ref, sst_ref, kv_ref, dec_ref,
         kd_ref, lhs_ref, pm_ref) = refs
        lat = (q_ref, kT_ref, v_ref, g_ref, None, None)
        cxt = (qc_ref, kTc_ref, vc_ref, gc_ref, None, None)

    phase = pl.program_id(2)
    j = pl.program_id(3)

    wdst_ref[...] = wsrc_ref[...].astype(BF16)

    ri = lax.broadcasted_iota(jnp.int32, (SP, SP), 0)
    ci = lax.broadcasted_iota(jnp.int32, (SP, SP), 1)
    same = (ri // CH) == (ci // CH)
    first_lane = ci < CH
    head_a = ci < HEAD_DK

    if gla:
        as_w = lambda m: m.astype(BF16)
        tot = jnp.concatenate([jnp.broadcast_to(ri < CH, (SP, SP)), jnp.broadcast_to(ri >= CH, (SP, SP))], axis=1)
        w_end_f = jnp.concatenate([as_w(same & (ri > ci)), as_w(tot)], axis=1)
        w_end_b = jnp.concatenate([as_w(same & (ri < ci)), as_w(tot)], axis=1)
        w_cum_f = as_w(same & (ri <= ci))
        w_cum_b = as_w(same & (ri >= ci))
    else:
        hp = pl.program_id(1)
        th = [[th_ref[0, dr, 2 * hp + hd] for hd in range(2)] for dr in range(2)]
        lane1 = lax.broadcasted_iota(jnp.int32, (1, SP), 1) < HEAD_DK
        row1 = lax.broadcasted_iota(jnp.int32, (SP, 1), 0) < HEAD_DK
        lg_row = [jnp.log1p(-jnp.exp(jnp.where(lane1, th[dr][0], th[dr][1]))) for dr in range(2)]
        lg_col = [jnp.log1p(-jnp.exp(jnp.where(row1, th[dr][0], th[dr][1]))) for dr in range(2)]
        it = (ci % CH).astype(F32)
        ir = (ri % CH).astype(F32)
        ret_end_f = jnp.exp((CH - 1.0 - it) * lg_col[0])
        ret_end_b = jnp.exp(it * lg_col[1])
        ret_dec_f = jnp.exp(jnp.broadcast_to(CH * lg_col[0], (SP, SP)))
        ret_dec_b = jnp.exp(jnp.broadcast_to(CH * lg_col[1], (SP, SP)))
        ret_ebTi = jnp.exp(-(it + 1.0) * lg_col[0])
        ret_erTi = jnp.exp(-(CH - it) * lg_col[1])
        ret_eb = jnp.exp((ir + 1.0) * lg_row[0])
        ret_er = jnp.exp((CH - ir) * lg_row[1])

    def kv_stage(blk, n, fwd):
        _, kT_r, v_r, _, lfT_r, lbT_r = blk

        def body(p, carry):
            kT = kT_r[p].astype(F32)
            v = v_r[pl.ds(pl.multiple_of(p * SP, SP), SP), :]
            if gla:
                res = _dot((lfT_r if fwd else lbT_r)[p], w_end_f if fwd else w_end_b)
                e_end = jnp.exp(res[:, 0:SP])
                dec0 = jnp.exp(res[:, SP:2 * SP])
                dec1 = jnp.exp(res[:, 2 * SP:3 * SP])
            else:
                e_end = ret_end_f if fwd else ret_end_b
                dec0 = dec1 = ret_dec_f if fwd else ret_dec_b
            ke = kT * e_end
            lhs = jnp.concatenate([jnp.where(first_lane, ke, 0.0), jnp.where(first_lane, 0.0, ke)],
                                  axis=0).astype(BF16)
            kv = _dot(lhs, v)
            kv_ref[2 * p] = _chunk_diag(kv, 0)
            kv_ref[2 * p + 1] = _chunk_diag(kv, 1)
            dec_ref[2 * p] = dec0
            dec_ref[2 * p + 1] = dec1
            return carry

        lax.fori_loop(0, n, body, 0, unroll=min(n, KV_UNROLL))

    def phase0_block(blk, n, slot0):
        kv_stage(blk, n, fwd=False)

        def body(i, r_state):
            p = n - 1 - i
            rst_ref[slot0 + p, :, HEAD_DV:2 * HEAD_DV] = r_state.astype(BF16)
            r_state = dec_ref[2 * p + 1] * r_state + kv_ref[2 * p + 1]
            rst_ref[slot0 + p, :, 0:HEAD_DV] = r_state.astype(BF16)
            return dec_ref[2 * p] * r_state + kv_ref[2 * p]

        r_ref[...] = lax.fori_loop(0, n, body, r_ref[...])

    def phase1_block(blk, n, slot0, out_ref):
        q_r, kT_r, v_r, g_r, lfT_r, lbT_r = blk
        kv_stage(blk, n, fwd=True)

        def rec(p, s_state):
            sst_ref[p, :, 0:HEAD_DV] = s_state.astype(BF16)
            s_state = dec_ref[2 * p] * s_state + kv_ref[2 * p]
            sst_ref[p, :, HEAD_DV:2 * HEAD_DV] = s_state.astype(BF16)
            return dec_ref[2 * p + 1] * s_state + kv_ref[2 * p + 1]

        s_ref[...] = lax.fori_loop(0, n, rec, s_ref[...])

        r2 = lax.broadcasted_iota(jnp.int32, (2 * SP, SP), 0) % SP
        c2 = lax.broadcasted_iota(jnp.int32, (2 * SP, SP), 1)
        same2 = (r2 // CH) == (c2 // CH)
        mask_f = same2 & (r2 >= c2)
        mask_b = same2 & (r2 < c2)

        def prep(p, carry):
            rows = pl.ds(pl.multiple_of(p * SP, SP), SP)
            q = q_r[rows, :].astype(F32)
            kT = kT_r[p].astype(F32)
            if gla:
                lfT = lfT_r[p]
                lbT = lbT_r[p]
                ebTi = jnp.exp(-_dot(lfT, w_cum_f))
                erTi = jnp.exp(-_dot(lbT, w_cum_b))
                e_b = jnp.exp(_dot_nt(w_cum_b, lfT))
                e_r = jnp.exp(_dot_nt(w_cum_f, lbT))
            else:
                ebTi, erTi, e_b, e_r = ret_ebTi, ret_erTi, ret_eb, ret_er
            kd_ref[p, 0:SP, 0:SP] = (kT * ebTi).astype(BF16)
            kd_ref[p, SP:2 * SP, SP:2 * SP] = (kT * erTi).astype(BF16)
            qf = q * e_b
            qb = q * e_r
            lhs_ref[p, :, 0:SP] = jnp.concatenate(
                [jnp.where(head_a, qf, 0.0), jnp.where(head_a, 0.0, qf)], axis=0).astype(BF16)
            lhs_ref[p, :, SP:2 * SP] = jnp.concatenate(
                [jnp.where(head_a, qb, 0.0), jnp.where(head_a, 0.0, qb)], axis=0).astype(BF16)
            return carry

        def score(p, carry):
            sc = _dot(lhs_ref[p], kd_ref[p])
            pm_ref[p] = jnp.where(mask_f, sc[:, 0:SP], jnp.where(mask_b, sc[:, SP:2 * SP], 0.0)).astype(BF16)
            return carry

        def emit(p, carry):
            rows = pl.ds(pl.multiple_of(p * SP, SP), SP)
            v = v_r[rows, :]
            states = jnp.concatenate([sst_ref[p], rst_ref[slot0 + p]], axis=0)
            o_int = _dot(lhs_ref[p], states)
            o_a = _dot(pm_ref[p, 0:SP, :], v[:, 0:HEAD_DV]) + jnp.concatenate(
                [o_int[0:CH, 0:HEAD_DV], o_int[CH:SP, HEAD_DV:2 * HEAD_DV]], axis=0)
            o_b = _dot(pm_ref[p, SP:2 * SP, :], v[:, HEAD_DV:2 * HEAD_DV]) + jnp.concatenate(
                [o_int[SP:SP + CH, 0:HEAD_DV], o_int[SP + CH:2 * SP, HEAD_DV:2 * HEAD_DV]], axis=0)

            def nrm(o):
                y = o * lax.rsqrt(jnp.mean(o * o, axis=-1, keepdims=True) + RMS_EPS)
                return y * ng_ref[...] if gla else y

            y = jnp.concatenate([nrm(o_a), nrm(o_b)], axis=1) * g_r[rows, :].astype(F32)
            out_ref[rows, :] = y.astype(BF16)
            return carry

        lax.fori_loop(0, n, prep, 0, unroll=min(n, OUT_UNROLL))
        lax.fori_loop(0, n, score, 0, unroll=min(n, OUT_UNROLL))
        lax.fori_loop(0, n, emit, 0, unroll=min(n, OUT_UNROLL))

    @pl.when((pl.program_id(0) == 0) & (pl.program_id(1) == 0) & (phase == 0) & (j == 0))
    def _():
        kd_ref[...] = jnp.zeros_like(kd_ref)

    @pl.when(phase == 0)
    def _():
        @pl.when(j == 0)
        def _():
            r_ref[...] = jnp.zeros_like(r_ref)
            phase0_block(cxt, NP_CTX, 0)

        phase0_block(lat, NP, NP_CTX + (NBLK - 1 - j) * NP)

    @pl.when(phase == 1)
    def _():
        @pl.when(j == 0)
        def _():
            s_ref[...] = jnp.zeros_like(s_ref)
            phase1_block(cxt, NP_CTX, 0, yc_ref)

        phase1_block(lat, NP, NP_CTX + j * NP, y_ref)


SCAN_STEPS = B * 2 * 2 * NBLK


def _scan_group(q, kT, v, g, extra, wsrc, *, gla):
    w_spec = pl.BlockSpec((None,) + wsrc.shape[1:],
                          lambda b, p, ph, j: (((b * 2 + p) * 2 + ph) * NBLK + j, 0, 0))

    def blk(b, ph, j, used_in_phase0):
        jj = jnp.where(ph == 0, NBLK - 1 - j, j)
        if not used_in_phase0:
            jj = jnp.where(ph == 0, 0, jj)
        return b * NBLK + jj

    def lat_specs(used0):
        return dict(
            row=lambda w: pl.BlockSpec((TB, w), lambda b, p, ph, j: (blk(b, ph, j, used0), p)),
            slab=pl.BlockSpec((NP, SP, SP), lambda b, p, ph, j: (blk(b, ph, j, used0), p, 0)))

    ctx_row = lambda w: pl.BlockSpec((L, w), lambda b, p, ph, j: (LAT_ROWS // L + b, p))
    ctx_slab = pl.BlockSpec((NP_CTX, SP, SP), lambda b, p, ph, j: (LAT_ROWS // L + b, p, 0))
    used, unused = lat_specs(True), lat_specs(False)

    in_specs = [unused["row"](2 * HEAD_DK), used["slab"], used["row"](2 * HEAD_DV), unused["row"](2 * HEAD_DV)]
    ctx_specs = [ctx_row(2 * HEAD_DK), ctx_slab, ctx_row(2 * HEAD_DV), ctx_row(2 * HEAD_DV)]
    if gla:
        lfT, lbT, ng = extra
        in_specs += [unused["slab"], used["slab"]]
        ctx_specs += [ctx_slab, ctx_slab]
        args = (q, kT, v, g, lfT, lbT, q, kT, v, g, lfT, lbT, ng, wsrc)
        in_specs = in_specs + ctx_specs + [pl.BlockSpec((1, HEAD_DV), lambda b, p, ph, j: (0, 0)), w_spec]
    else:
        (theta,) = extra
        args = (q, kT, v, g, q, kT, v, g, theta, wsrc)
        in_specs = in_specs + ctx_specs + [pl.BlockSpec(memory_space=pltpu.SMEM), w_spec]
    return pl.pallas_call(
        functools.partial(_scan_kernel, gla=gla),
        grid=(B, 2, 2, NBLK),
        in_specs=in_specs,
        out_specs=[
            pl.BlockSpec((TB, 2 * HEAD_DV), lambda b, p, ph, j: (b * NBLK + jnp.where(ph == 0, 0, j), p)),
            pl.BlockSpec((L, 2 * HEAD_DV), lambda b, p, ph, j: (b, p)),
            w_spec,
        ],
        out_shape=[jax.ShapeDtypeStruct((LAT_ROWS, V_W), BF16), jax.ShapeDtypeStruct((B * L, V_W), BF16),
                   jax.ShapeDtypeStruct(wsrc.shape, BF16)],
        scratch_shapes=[
            pltpu.VMEM((SP, HEAD_DV), F32),
            pltpu.VMEM((SP, HEAD_DV), F32),
            pltpu.VMEM((NP_ALL, SP, 2 * HEAD_DV), BF16),
            pltpu.VMEM((NP, SP, 2 * HEAD_DV), BF16),
            pltpu.VMEM((2 * NP, SP, HEAD_DV), F32),
            pltpu.VMEM((2 * NP, SP, HEAD_DV), F32),
            pltpu.VMEM((NP, 2 * SP, 2 * SP), BF16),
            pltpu.VMEM((NP, 2 * SP, 2 * SP), BF16),
            pltpu.VMEM((NP, 2 * SP, SP), BF16),
        ],
        compiler_params=pltpu.CompilerParams(
            dimension_semantics=("arbitrary",) * 4, vmem_limit_bytes=VMEM_LIMIT),
        name="scan_gla" if gla else "scan_ret",
    )(*args)


FF_CH = 512


def _out_kernel(*refs, layer, split_ctx):
    if split_ctx:
        (ya_ref, yb_ref, yac_ref, ybc_ref, x_ref, ctx_ref, mod_ref, lng_ref, lnb_ref,
         wo_ref, w1_ref, w2_ref, o_ref, wo_s) = refs
        is_ctx = pl.program_id(0) == N_LAT_TILES
        x = jnp.where(is_ctx, ctx_ref[...], x_ref[...])
        ya = jnp.where(is_ctx, yac_ref[...], ya_ref[...])
        yb = jnp.where(is_ctx, ybc_ref[...], yb_ref[...])
    else:
        ya_ref, yb_ref, x_ref, mod_ref, lng_ref, lnb_ref, wo_ref, w1_ref, w2_ref, o_ref, wo_s = refs
        x = x_ref[...]
        ya = ya_ref[...]
        yb = yb_ref[...]
    g1 = mod_ref[:, 2 * D:3 * D]
    sh2 = mod_ref[:, 3 * D:4 * D]
    sc2 = mod_ref[:, 4 * D:5 * D]
    g2 = mod_ref[:, 5 * D:6 * D]
    ln_g0 = lng_ref[2 * layer:2 * layer + 1, :]
    ln_g1 = lng_ref[2 * layer + 1:2 * layer + 2, :]
    ln_b0 = lnb_ref[2 * layer:2 * layer + 1, :]
    ln_b1 = lnb_ref[2 * layer + 1:2 * layer + 2, :]

    @pl.when(pl.program_id(0) == 0)
    def _():
        wo_s[...] = wo_ref[...].astype(BF16)

    half = D // 2
    y = _dot(ya, wo_s[0:half, :]) + _dot(yb, wo_s[half:D, :])
    x1 = _layer_norm(ALPHA * x + g1 * y, ln_g0, ln_b0)
    h2 = (x1 * (1.0 + sc2) + sh2).astype(BF16)
    acc = jnp.zeros((TM, D), F32)
    for c in range(D_FF // FF_CH):
        cols = slice(c * FF_CH, (c + 1) * FF_CH)
        hc = jnp.maximum(_dot(h2, w1_ref[:, cols]), 0.0)
        acc = acc + _dot((hc * hc).astype(BF16), w2_ref[cols, :])
    o_ref[...] = _layer_norm(ALPHA * x1 + g2 * acc, ln_g1, ln_b1)


def _out_mlp(ys, xs, ctx2, mods, ln, wo, w1, w2, *, layer, n_tiles, split_ctx):
    half = D // 2
    lat_row = lambda t: (jnp.minimum(t, N_LAT_TILES - 1), 0)
    if split_ctx:
        ya, yb, yac, ybc = ys
        in_specs = [pl.BlockSpec((TM, half), lat_row), pl.BlockSpec((TM, half), lat_row),
                    _full_spec((B * L, half)), _full_spec((B * L, half)),
                    pl.BlockSpec((TM, D), lat_row), _full_spec((B * L, D))]
        args = [ya, yb, yac, ybc, xs, ctx2]
    else:
        (att,) = ys
        in_specs = [pl.BlockSpec((TM, half), lambda t: (t, 0)), pl.BlockSpec((TM, half), lambda t: (t, 1)),
                    pl.BlockSpec((TM, D), lambda t: (t, 0))]
        args = [att, att, xs]
    stacked = lambda w, i: pl.BlockSpec((None,) + w.shape[1:], lambda t: (i, 0, 0), pipeline_mode=pl.Buffered(1))
    in_specs += [_mod_spec(layer), _full_spec(ln[0].shape), _full_spec(ln[1].shape),
                 stacked(wo, 0), stacked(w1, layer), stacked(w2, layer)]
    args += [mods, ln[0], ln[1], wo, w1, w2]
    return pl.pallas_call(
        functools.partial(_out_kernel, layer=layer, split_ctx=split_ctx),
        grid=(n_tiles,),
        in_specs=in_specs,
        out_specs=pl.BlockSpec((TM, D), lambda t: (t, 0)),
        out_shape=jax.ShapeDtypeStruct((n_tiles * TM, D), F32),
        compiler_params=pltpu.CompilerParams(
            dimension_semantics=("arbitrary",), vmem_limit_bytes=VMEM_LIMIT),
        scratch_shapes=[pltpu.VMEM((D, D), BF16)],
        name="out_mlp%d" % layer,
    )(*args)


def _in1_kernel(x_ref, mod_ref, cos_ref, sin_ref, w_ref, qT_o, k_o, vT_o, wqT_s, wk_s, wvT_s):
    @pl.when(pl.program_id(0) == 0)
    def _():
        wqT_s[...] = _transpose_bf16(w_ref[:, 0:D])
        wk_s[...] = w_ref[:, D:D + ATT_KVW].astype(BF16)
        wvT_s[...] = _transpose_bf16(w_ref[:, D + ATT_KVW:D + 2 * ATT_KVW])

    sh1 = mod_ref[:, 0:D]
    sc1 = mod_ref[:, D:2 * D]
    h = (x_ref[...] * (1.0 + sc1) + sh1).astype(BF16)
    nsl = TM // SP

    cos = cos_ref[...]
    sin = sin_ref[...]
    first = (lax.broadcasted_iota(jnp.int32, (TM, LANES), 1) % (2 * ROT)) < ROT
    sa = jnp.where(first, -sin, 0.0)
    sb = jnp.where(first, 0.0, sin)

    k = _dot(h, wk_s[...])
    for i in range(ATT_KVW // LANES):
        u = k[:, i * LANES:(i + 1) * LANES]
        r = u * cos + pltpu.roll(u, LANES - ROT, 1) * sa + pltpu.roll(u, ROT, 1) * sb
        k_o[:, i * LANES:(i + 1) * LANES] = r.astype(BF16)

    cosT, saT, sbT = cos.T, sa.T, sb.T
    qT = _dot_nt(wqT_s[...], h) * (LOG2E * ATT_DH ** -0.5)
    for i in range(D // LANES):
        u = qT[i * LANES:(i + 1) * LANES, :]
        r = (u * cosT + pltpu.roll(u, LANES - ROT, 0) * saT + pltpu.roll(u, ROT, 0) * sbT).astype(BF16)
        for s in range(nsl):
            qT_o[s, i * LANES:(i + 1) * LANES, :] = r[:, s * SP:(s + 1) * SP]

    vT = _dot_nt(wvT_s[...], h).astype(BF16)
    for s in range(nsl):
        vT_o[s] = vT[:, s * SP:(s + 1) * SP]


def _in_proj1(xs, mods, cos_tab, sin_tab, wqkv):
    tile = lambda t: jnp.where(t == N_LAT_TILES, TILES_PER_BATCH, t % TILES_PER_BATCH)
    row = lambda t: (t, 0)
    slab = lambda t: (t, 0, 0)
    nsl = TM // SP
    tab_spec = pl.BlockSpec((TM, LANES), lambda t: (tile(t), 0))
    return pl.pallas_call(
        _in1_kernel,
        grid=(N_TILES,),
        in_specs=[pl.BlockSpec((TM, D), row), _mod_spec(1), tab_spec, tab_spec,
                  pl.BlockSpec((None,) + wqkv.shape[1:], lambda t: (0, 0, 0), pipeline_mode=pl.Buffered(1))],
        out_specs=[pl.BlockSpec((nsl, D, SP), slab), pl.BlockSpec((TM, ATT_KVW), row),
                   pl.BlockSpec((nsl, ATT_KVW, SP), slab)],
        out_shape=[jax.ShapeDtypeStruct((N_SLABS, D, SP), BF16), jax.ShapeDtypeStruct((ROWS, ATT_KVW), BF16),
                   jax.ShapeDtypeStruct((N_SLABS, ATT_KVW, SP), BF16)],
        scratch_shapes=[pltpu.VMEM((D, D), BF16), pltpu.VMEM((D, ATT_KVW), BF16), pltpu.VMEM((ATT_KVW, D), BF16)],
        compiler_params=pltpu.CompilerParams(
            dimension_semantics=("arbitrary",), vmem_limit_bytes=VMEM_LIMIT),
        name="in_proj1",
    )(xs, mods, cos_tab, sin_tab, wqkv)


def _attn_kernel(sink_ref, qT_ref, k_ref, vT_ref, kc_ref, vcT_ref, o_ref, bias_ref):
    n = pl.program_id(1)
    nql = ATT_GROUP * QB
    lane_g = lax.broadcasted_iota(jnp.int32, (1, nql), 1) // QB
    zero_half = jnp.zeros((ATT_DH, nql), BF16)
    ones_rows = jnp.ones((BF16_ROWS, KWIN + L), BF16)

    @pl.when((pl.program_id(0) == 0) & (n == 0))
    def _():
        kj = lax.broadcasted_iota(jnp.int32, (KWIN, QB), 0)
        qi = lax.broadcasted_iota(jnp.int32, (KWIN, QB), 1)
        for case, delta in enumerate((-QB, 0, -2 * QB)):
            d = kj - qi + delta
            bias_ref[case] = jnp.where((d >= -WINDOW) & (d <= WINDOW), 0.0, -jnp.inf)

    def scores(i, kh):
        n0 = (n * (TQ // QB) + i) * QB
        start = pl.multiple_of(jnp.clip(n0 - QB, 0, T - KWIN), QB)
        bias1 = bias_ref[jnp.where(n0 == 0, 1, jnp.where(n0 == T - QB, 2, 0))]
        bias = jnp.concatenate([bias1] * ATT_GROUP, axis=1)
        pair = slice((kh // 2) * LANES, (kh // 2 + 1) * LANES)
        qT = jnp.concatenate(
            [qT_ref[i, (kh * ATT_GROUP + g) * ATT_DH:(kh * ATT_GROUP + g + 1) * ATT_DH, :]
             for g in range(ATT_GROUP)], axis=1)
        q_pad = jnp.concatenate([zero_half, qT] if kh % 2 else [qT, zero_half], axis=0)
        s_loc = _dot(k_ref[pl.ds(start, KWIN), pair], q_pad) + bias
        s_ctx = _dot(kc_ref[:, pair], q_pad)
        return s_loc, s_ctx, start // SP

    def finish(i, kh, s_loc, s_ctx, slab0):
        drows = slice(kh * ATT_DH, (kh + 1) * ATT_DH)
        sink = jnp.zeros((1, nql), F32)
        for g in range(ATT_GROUP):
            sink = jnp.where(lane_g == g, sink_ref[0, kh * ATT_GROUP + g] * LOG2E, sink)
        m = jnp.maximum(jnp.maximum(jnp.max(s_loc, axis=0, keepdims=True),
                                    jnp.max(s_ctx, axis=0, keepdims=True)), sink)
        pT = jnp.concatenate([jnp.exp2(s_loc - m).astype(BF16),
                              jnp.exp2(s_ctx - m).astype(BF16)], axis=0)
        vT = jnp.concatenate([vT_ref[slab0 + t, drows, :] for t in range(KWIN // SP)]
                             + [vcT_ref[t, drows, :] for t in range(L // SP)], axis=1)
        o_ext = _dot(jnp.concatenate([vT, ones_rows], axis=0), pT)
        denom = o_ext[ATT_DH:ATT_DH + 1, :] + jnp.exp2(sink - m)
        oT = o_ext[0:ATT_DH, :] / denom
        for g in range(0, ATT_GROUP, 2):
            two = jnp.concatenate([oT[:, g * QB:(g + 1) * QB], oT[:, (g + 1) * QB:(g + 2) * QB]], axis=0)
            c0 = (kh * ATT_GROUP + g) * ATT_DH
            o_ref[i * QB:(i + 1) * QB, c0:c0 + 2 * ATT_DH] = two.T.astype(BF16)

    items = [(i, kh) for i in range(TQ // QB) for kh in range(ATT_KVH)]
    pending = [scores(*it) for it in items[:SCORE_AHEAD]]
    for t, item in enumerate(items):
        if t + SCORE_AHEAD < len(items):
            pending.append(scores(*items[t + SCORE_AHEAD]))
        finish(*item, *pending.pop(0))


def _attention(sink, qT, k, vT):
    nq = T // TQ
    return pl.pallas_call(
        _attn_kernel,
        grid=(B, nq),
        in_specs=[
            pl.BlockSpec(memory_space=pltpu.SMEM),
            pl.BlockSpec((TQ // SP, D, SP), lambda b, n: (b * nq + n, 0, 0)),
            pl.BlockSpec((T, ATT_KVW), lambda b, n: (b, 0)),
            pl.BlockSpec((T // SP, ATT_KVW, SP), lambda b, n: (b, 0, 0)),
            pl.BlockSpec((L, ATT_KVW), lambda b, n: (LAT_ROWS // L + b, 0)),
            pl.BlockSpec((L // SP, ATT_KVW, SP), lambda b, n: (LAT_ROWS // L + b, 0, 0)),
        ],
        out_specs=pl.BlockSpec((TQ, D), lambda b, n: (b * nq + n, 0)),
        out_shape=jax.ShapeDtypeStruct((LAT_ROWS, D), BF16),
        compiler_params=pltpu.CompilerParams(
            dimension_semantics=("arbitrary", "arbitrary"), vmem_limit_bytes=VMEM_LIMIT),
        scratch_shapes=[pltpu.VMEM((3, KWIN, QB), F32)],
        name="window_attn",
    )(sink, qT, k, vT, k, vT)


def _rope_tables():
    half = ATT_DH // 2
    inv_freq = np.power(np.float32(ROPE_BASE), -np.arange(0, half, 2, dtype=np.float32) / np.float32(half))
    inv_freq = inv_freq.astype(np.float32)
    pos = np.arange(T)
    ang_r = (pos // GRID_W).astype(np.float32)[:, None] * inv_freq[None, :]
    ang_c = (pos % GRID_W).astype(np.float32)[:, None] * inv_freq[None, :]
    ang = np.concatenate([ang_r, ang_r, ang_c, ang_c], axis=1)
    ang = np.concatenate([ang, np.zeros((TM, ATT_DH), np.float32)], axis=0)
    ang = np.concatenate([ang, ang], axis=1)
    return np.cos(ang).astype(np.float32), np.sin(ang).astype(np.float32)


def kernel(x, c, ctx, c_ctx, w_mod, b_mod, ln_g, ln_b, mlp_w1, mlp_w2, ev_w_in, ev_ret_theta, ev_gla_gk_w,
           ev_gla_gk_b, ev_gla_norm_g, ev_w_out, od_w_qkv, od_sink, od_w_out):
    x2 = x.reshape(LAT_ROWS, D)
    ctx2 = ctx.reshape(B * L, D)

    cs = jnp.concatenate([c, c_ctx[None, :], jnp.zeros((MOD_ROWS - B - 1, D), F32)], axis=0)
    mods = _modulation(cs, w_mod, b_mod).reshape(DEPTH * MOD_ROWS, 1, 6 * D)
    ln = (ln_g.reshape(2 * DEPTH, D), ln_b.reshape(2 * DEPTH, D))

    gk_w = ev_gla_gk_w[0]
    zeros = jnp.zeros((GATE_RANK, QK_W), F32)
    g2 = jnp.concatenate([jnp.concatenate([gk_w[0], zeros], axis=1),
                          jnp.concatenate([zeros, gk_w[1]], axis=1)], axis=0)
    qa, kaT, va, ga, qb, kbT, vb, gbv, lfT, lbT = _in_proj0(
        x2, ctx2, mods, ev_w_in[0].T, g2.T.astype(BF16), ev_gla_gk_b[0].reshape(2 * QK_W, 1))

    w1_rows = DEPTH * D // SCAN_STEPS
    w2_rows = DEPTH * D_FF // SCAN_STEPS
    y_ret, yc_ret, w1 = _scan_group(qa, kaT, va, ga, (ev_ret_theta,),
                                    mlp_w1.reshape(SCAN_STEPS, w1_rows, D_FF), gla=False)
    y_gla, yc_gla, w2 = _scan_group(qb, kbT, vb, gbv, (lfT, lbT, ev_gla_norm_g),
                                    mlp_w2.reshape(SCAN_STEPS, w2_rows, D), gla=True)
    w1 = w1.reshape(DEPTH, D, D_FF)
    w2 = w2.reshape(DEPTH, D_FF, D)

    xs = _out_mlp((y_ret, y_gla, yc_ret, yc_gla), x2, ctx2, mods, ln, ev_w_out, w1, w2,
                  layer=0, n_tiles=N_TILES, split_ctx=True)

    cos_tab, sin_tab = _rope_tables()
    q1T, k1, v1T = _in_proj1(xs, mods, cos_tab, sin_tab, od_w_qkv)
    att = _attention(od_sink, q1T, k1, v1T)
    out = _out_mlp((att,), xs, None, mods, ln, od_w_out, w1, w2,
                   layer=1, n_tiles=N_LAT_TILES, split_ctx=False)
    return out.reshape(B, T, D)
```

```python
import functools

import jax
import jax.numpy as jnp
import numpy as np
from jax import lax
from jax.experimental import pallas as pl
from jax.experimental.pallas import tpu as pltpu

F32 = jnp.float32
BF16 = jnp.bfloat16

D = 1024
B = 2
T = 8192
L = 256
DEPTH = 2
GRID_W = 64
D_FF = 4 * D
HEAD_DV = 128
HEAD_DK = 64
GATE_RANK = 16
GATE_TAU = 16.0
QK_W = 256
V_W = 512
ATT_DH = 64
ATT_QH = 16
ATT_KVH = 4
ATT_GROUP = 4
ATT_KVW = ATT_KVH * ATT_DH
WINDOW = 128
ROPE_BASE = 10000.0
ALPHA = (2.0 * DEPTH) ** 0.25
LN_EPS = 1e-5
RMS_EPS = 1e-6

LANES = 128
SUBLANES = 8
BF16_ROWS = 2 * SUBLANES
MOD_ROWS = SUBLANES
ROT = ATT_DH // 4

TM = 512
LAT_ROWS = B * T
ROWS = LAT_ROWS + B * L
N_LAT_TILES = LAT_ROWS // TM
N_TILES = ROWS // TM
TILES_PER_BATCH = T // TM

CH = 64
SP = 2 * CH
TB = 4096
NP = TB // SP
NP_CTX = L // SP
NBLK = T // TB
NP_ALL = NP_CTX + T // SP
N_SLABS = ROWS // SP
assert SP == LANES and 2 * HEAD_DK == LANES and HEAD_DV == LANES
KV_UNROLL = 32
OUT_UNROLL = 32

TQ = 1024
QB = 128
KWIN = 3 * QB
SCORE_AHEAD = 3
LOG2E = 1.4426950408889634

VMEM_LIMIT = 56 * 1024 * 1024


def _dot(a, b):
    return jnp.dot(a, b, preferred_element_type=F32)


def _dot_nt(a, b):
    return lax.dot_general(a, b, (((1,), (1,)), ((), ())), preferred_element_type=F32)


def _full_spec(shape):
    nd = len(shape)
    return pl.BlockSpec(shape, lambda *_: (0,) * nd, pipeline_mode=pl.Buffered(1))


def _mod_row(t):
    return jnp.minimum(t // TILES_PER_BATCH, B)


def _mod_spec(layer):
    return pl.BlockSpec((None, 1, 6 * D), lambda t: (layer * MOD_ROWS + _mod_row(t), 0, 0))


def _layer_norm(x, g, b):
    mu = jnp.mean(x, axis=-1, keepdims=True)
    xc = x - mu
    var = jnp.mean(xc * xc, axis=-1, keepdims=True)
    return xc * lax.rsqrt(var + LN_EPS) * g + b


def _log_sigmoid(z):
    return jnp.minimum(z, 0.0) - jnp.log(1.0 + jnp.exp(-jnp.abs(z)))


def _silu(x):
    half = 0.5 * x
    return half + half * jnp.tanh(half)


MOD_TN = 1536


def _mod_kernel(c_ref, w_ref, b_ref, o_ref):
    s = jax.nn.silu(c_ref[...])
    s_hi = s.astype(BF16)
    s_lo = (s - s_hi.astype(F32)).astype(BF16)
    w = w_ref[...].astype(BF16)
    o_ref[...] = _dot(s_hi, w) + _dot(s_lo, w) + b_ref[...]


def _modulation(cs, w_mod, b_mod):
    return pl.pallas_call(
        _mod_kernel,
        grid=(DEPTH, 6 * D // MOD_TN),
        in_specs=[
            pl.BlockSpec((MOD_ROWS, D), lambda i, n: (0, 0)),
            pl.BlockSpec((None, D, MOD_TN), lambda i, n: (i, 0, n)),
            pl.BlockSpec((None, 1, MOD_TN), lambda i, n: (i, 0, n)),
        ],
        out_specs=pl.BlockSpec((None, MOD_ROWS, MOD_TN), lambda i, n: (i, 0, n)),
        out_shape=jax.ShapeDtypeStruct((DEPTH, MOD_ROWS, 6 * D), F32),
        compiler_params=pltpu.CompilerParams(
            dimension_semantics=("arbitrary", "arbitrary"), vmem_limit_bytes=VMEM_LIMIT),
        name="modulation",
    )(cs, w_mod, b_mod.reshape(DEPTH, 1, 6 * D))


IN0_OFF = (0, 256, 512, 1024, 1536, 1792, 2048, 2560, 3072, 3104)


def _transpose_bf16(w):
    return w.astype(F32).T.astype(BF16)


def _in0_kernel(x_ref, ctx_ref, mod_ref, wT_ref, g2T, gbc,
                qa_o, kaT_o, va_o, ga_o, qb_o, kbT_o, vb_o, gb_o, lfT_o, lbT_o, wbT_s):
    t = pl.program_id(0)
    rows = lambda i: slice(IN0_OFF[i], IN0_OFF[i + 1])
    piece = lambda i: wbT_s[rows(i), :]

    @pl.when(t == 0)
    def _():
        for i in range(len(IN0_OFF) - 1):
            wbT_s[rows(i), :] = wT_ref[rows(i), :].astype(BF16)

    xt = jnp.where(t == N_LAT_TILES, ctx_ref[...], x_ref[...])
    sh1 = mod_ref[:, 0:D]
    sc1 = mod_ref[:, D:2 * D]
    h = (xt * (1.0 + sc1) + sh1).astype(BF16)
    qk_scale = HEAD_DK ** -0.5

    def put_slabs(o_ref, val):
        for i in range(TM // SP):
            o_ref[i] = val[:, i * SP:(i + 1) * SP].astype(o_ref.dtype)

    lrT = _dot_nt(piece(8), h).astype(BF16)
    ga_o[...] = _silu(_dot_nt(h, piece(3))).astype(BF16)
    lsT = _log_sigmoid(_dot(g2T[...], lrT) + gbc[...]) * (1.0 / GATE_TAU)
    put_slabs(lfT_o, lsT[0:QK_W, :])
    put_slabs(lbT_o, lsT[QK_W:2 * QK_W, :])
    gb_o[...] = _silu(_dot_nt(h, piece(7))).astype(BF16)

    put_slabs(kaT_o, _dot_nt(piece(1), h) * qk_scale)
    put_slabs(kbT_o, _dot_nt(piece(5), h))
    qa_o[...] = _dot_nt(h, piece(0)).astype(BF16)
    qb_o[...] = (_dot_nt(h, piece(4)) * qk_scale).astype(BF16)
    va_o[...] = _dot_nt(h, piece(2)).astype(BF16)
    vb_o[...] = _dot_nt(h, piece(6)).astype(BF16)


def _in_proj0(x2, ctx2, mods, w_inT, g2T, gbc):
    row = lambda t: (t, 0)
    slab = lambda t: (t, 0, 0)
    nsl = TM // SP
    row_out = lambda width: (jax.ShapeDtypeStruct((ROWS, width), BF16), pl.BlockSpec((TM, width), row))
    slab_out = lambda dt: (jax.ShapeDtypeStruct((N_SLABS, QK_W, SP), dt), pl.BlockSpec((nsl, QK_W, SP), slab))
    outs = [row_out(QK_W), slab_out(BF16), row_out(V_W), row_out(V_W),
            row_out(QK_W), slab_out(BF16), row_out(V_W), row_out(V_W),
            slab_out(BF16), slab_out(BF16)]
    in_specs = [
        pl.BlockSpec((TM, D), lambda t: (jnp.minimum(t, N_LAT_TILES - 1), 0)),
        _full_spec((B * L, D)),
        _mod_spec(0),
        _full_spec(w_inT.shape), _full_spec(g2T.shape), _full_spec(gbc.shape),
    ]
    return pl.pallas_call(
        _in0_kernel,
        grid=(N_TILES,),
        in_specs=in_specs,
        out_specs=[o[1] for o in outs],
        out_shape=[o[0] for o in outs],
        scratch_shapes=[pltpu.VMEM(w_inT.shape, BF16)],
        compiler_params=pltpu.CompilerParams(
            dimension_semantics=("arbitrary",), vmem_limit_bytes=VMEM_LIMIT),
        name="in_proj0",
    )(x2, ctx2, mods, w_inT, g2T, gbc)


def _chunk_diag(kv, c):
    r0 = c * 2 * HEAD_DK
    return jnp.concatenate([kv[r0:r0 + HEAD_DK, 0:HEAD_DV],
                            kv[r0 + HEAD_DK:r0 + 2 * HEAD_DK, HEAD_DV:2 * HEAD_DV]], axis=0)


def _scan_kernel(*refs, gla):
    if gla:
        (q_ref, kT_ref, v_ref, g_ref, lfT_ref, lbT_ref,
         qc_ref, kTc_ref, vc_ref, gc_ref, lfTc_ref, lbTc_ref, ng_ref, wsrc_ref,
         y_ref, yc_ref, wdst_ref, s_ref, r_ref, rst_ref, sst_ref, kv_ref, dec_ref,
         kd_ref, lhs_ref, pm_ref) = refs
        lat = (q_ref, kT_ref, v_ref, g_ref, lfT_ref, lbT_ref)
        cxt = (qc_ref, kTc_ref, vc_ref, gc_ref, lfTc_ref, lbTc_ref)
    else:
        (q_ref, kT_ref, v_ref, g_ref, qc_ref, kTc_ref, vc_ref, gc_ref, th_ref, wsrc_ref,
         y_ref, yc_ref, wdst_ref, s_ref, r_ref, rst_ref, sst_ref, kv_ref, dec_ref,
         kd_ref, lhs_ref, pm_ref) = refs
        lat = (q_ref, kT_ref, v_ref, g_ref, None, None)
        cxt = (qc_ref, kTc_ref, vc_ref, gc_ref, None, None)

    phase = pl.program_id(2)
    j = pl.program_id(3)

    wdst_ref[...] = wsrc_ref[...].astype(BF16)

    ri = lax.broadcasted_iota(jnp.int32, (SP, SP), 0)
    ci = lax.broadcasted_iota(jnp.int32, (SP, SP), 1)
    same = (ri // CH) == (ci // CH)
    first_lane = ci < CH
    head_a = ci < HEAD_DK

    if gla:
        as_w = lambda m: m.astype(BF16)
        tot = jnp.concatenate([jnp.broadcast_to(ri < CH, (SP, SP)), jnp.broadcast_to(ri >= CH, (SP, SP))], axis=1)
        w_end_f = jnp.concatenate([as_w(same & (ri > ci)), as_w(tot)], axis=1)
        w_end_b = jnp.concatenate([as_w(same & (ri < ci)), as_w(tot)], axis=1)
        w_cum_f = as_w(same & (ri <= ci))
        w_cum_b = as_w(same & (ri >= ci))
    else:
        hp = pl.program_id(1)
        th = [[th_ref[0, dr, 2 * hp + hd] for hd in range(2)] for dr in range(2)]
        lane1 = lax.broadcasted_iota(jnp.int32, (1, SP), 1) < HEAD_DK
        row1 = lax.broadcasted_iota(jnp.int32, (SP, 1), 0) < HEAD_DK
        lg_row = [jnp.log1p(-jnp.exp(jnp.where(lane1, th[dr][0], th[dr][1]))) for dr in range(2)]
        lg_col = [jnp.log1p(-jnp.exp(jnp.where(row1, th[dr][0], th[dr][1]))) for dr in range(2)]
        it = (ci % CH).astype(F32)
        ir = (ri % CH).astype(F32)
        ret_end_f = jnp.exp((CH - 1.0 - it) * lg_col[0])
        ret_end_b = jnp.exp(it * lg_col[1])
        ret_dec_f = jnp.exp(jnp.broadcast_to(CH * lg_col[0], (SP, SP)))
        ret_dec_b = jnp.exp(jnp.broadcast_to(CH * lg_col[1], (SP, SP)))
        ret_ebTi = jnp.exp(-(it + 1.0) * lg_col[0])
        ret_erTi = jnp.exp(-(CH - it) * lg_col[1])
        ret_eb = jnp.exp((ir + 1.0) * lg_row[0])
        ret_er = jnp.exp((CH - ir) * lg_row[1])

    def kv_stage(blk, n, fwd):
        _, kT_r, v_r, _, lfT_r, lbT_r = blk

        def body(p, carry):
            kT = kT_r[p].astype(F32)
            v = v_r[pl.ds(pl.multiple_of(p * SP, SP), SP), :]
            if gla:
                res = _dot((lfT_r if fwd else lbT_r)[p], w_end_f if fwd else w_end_b)
                e_end = jnp.exp(res[:, 0:SP])
                dec0 = jnp.exp(res[:, SP:2 * SP])
                dec1 = jnp.exp(res[:, 2 * SP:3 * SP])
            else:
                e_end = ret_end_f if fwd else ret_end_b
                dec0 = dec1 = ret_dec_f if fwd else ret_dec_b
            ke = kT * e_end
            lhs = jnp.concatenate([jnp.where(first_lane, ke, 0.0), jnp.where(first_lane, 0.0, ke)],
                                  axis=0).astype(BF16)
            kv = _dot(lhs, v)
            kv_ref[2 * p] = _chunk_diag(kv, 0)
            kv_ref[2 * p + 1] = _chunk_diag(kv, 1)
            dec_ref[2 * p] = dec0
            dec_ref[2 * p + 1] = dec1
            return carry

        lax.fori_loop(0, n, body, 0, unroll=min(n, KV_UNROLL))

    def phase0_block(blk, n, slot0):
        kv_stage(blk, n, fwd=False)

        def body(i, r_state):
            p = n - 1 - i
            rst_ref[slot0 + p, :, HEAD_DV:2 * HEAD_DV] = r_state.astype(BF16)
            r_state = dec_ref[2 * p + 1] * r_state + kv_ref[2 * p + 1]
            rst_ref[slot0 + p, :, 0:HEAD_DV] = r_state.astype(BF16)
            return dec_ref[2 * p] * r_state + kv_ref[2 * p]

        r_ref[...] = lax.fori_loop(0, n, body, r_ref[...])

    def phase1_block(blk, n, slot0, out_ref):
        q_r, kT_r, v_r, g_r, lfT_r, lbT_r = blk
        kv_stage(blk, n, fwd=True)

        def rec(p, s_state):
            sst_ref[p, :, 0:HEAD_DV] = s_state.astype(BF16)
            s_state = dec_ref[2 * p] * s_state + kv_ref[2 * p]
            sst_ref[p, :, HEAD_DV:2 * HEAD_DV] = s_state.astype(BF16)
            return dec_ref[2 * p + 1] * s_state + kv_ref[2 * p + 1]

        s_ref[...] = lax.fori_loop(0, n, rec, s_ref[...])

        r2 = lax.broadcasted_iota(jnp.int32, (2 * SP, SP), 0) % SP
        c2 = lax.broadcasted_iota(jnp.int32, (2 * SP, SP), 1)
        same2 = (r2 // CH) == (c2 // CH)
        mask_f = same2 & (r2 >= c2)
        mask_b = same2 & (r2 < c2)

        def prep(p, carry):
            rows = pl.ds(pl.multiple_of(p * SP, SP), SP)
            q = q_r[rows, :].astype(F32)
            kT = kT_r[p].astype(F32)
            if gla:
                lfT = lfT_r[p]
                lbT = lbT_r[p]
                ebTi = jnp.exp(-_dot(lfT, w_cum_f))
                erTi = jnp.exp(-_dot(lbT, w_cum_b))
                e_b = jnp.exp(_dot_nt(w_cum_b, lfT))
                e_r = jnp.exp(_dot_nt(w_cum_f, lbT))
            else:
                ebTi, erTi, e_b, e_r = ret_ebTi, ret_erTi, ret_eb, ret_er
            kd_ref[p, 0:SP, 0:SP] = (kT * ebTi).astype(BF16)
            kd_ref[p, SP:2 * SP, SP:2 * SP] = (kT * erTi).astype(BF16)
            qf = q * e_b
            qb = q * e_r
            lhs_ref[p, :, 0:SP] = jnp.concatenate(
                [jnp.where(head_a, qf, 0.0), jnp.where(head_a, 0.0, qf)], axis=0).astype(BF16)
            lhs_ref[p, :, SP:2 * SP] = jnp.concatenate(
                [jnp.where(head_a, qb, 0.0), jnp.where(head_a, 0.0, qb)], axis=0).astype(BF16)
            return carry

        def score(p, carry):
            sc = _dot(lhs_ref[p], kd_ref[p])
            pm_ref[p] = jnp.where(mask_f, sc[:, 0:SP], jnp.where(mask_b, sc[:, SP:2 * SP], 0.0)).astype(BF16)
            return carry

        def emit(p, carry):
            rows = pl.ds(pl.multiple_of(p * SP, SP), SP)
            v = v_r[rows, :]
            states = jnp.concatenate([sst_ref[p], rst_ref[slot0 + p]], axis=0)
            o_int = _dot(lhs_ref[p], states)
            o_a = _dot(pm_ref[p, 0:SP, :], v[:, 0:HEAD_DV]) + jnp.concatenate(
                [o_int[0:CH, 0:HEAD_DV], o_int[CH:SP, HEAD_DV:2 * HEAD_DV]], axis=0)
            o_b = _dot(pm_ref[p, SP:2 * SP, :], v[:, HEAD_DV:2 * HEAD_DV]) + jnp.concatenate(
                [o_int[SP:SP + CH, 0:HEAD_DV], o_int[SP + CH:2 * SP, HEAD_DV:2 * HEAD_DV]], axis=0)

            def nrm(o):
                y = o * lax.rsqrt(jnp.mean(o * o, axis=-1, keepdims=True) + RMS_EPS)
                return y * ng_ref[...] if gla else y

            y = jnp.concatenate([nrm(o_a), nrm(o_b)], axis=1) * g_r[rows, :].astype(F32)
            out_ref[rows, :] = y.astype(BF16)
            return carry

        lax.fori_loop(0, n, prep, 0, unroll=min(n, OUT_UNROLL))
        lax.fori_loop(0, n, score, 0, unroll=min(n, OUT_UNROLL))
        lax.fori_loop(0, n, emit, 0, unroll=min(n, OUT_UNROLL))

    @pl.when((pl.program_id(0) == 0) & (pl.program_id(1) == 0) & (phase == 0) & (j == 0))
    def _():
        kd_ref[...] = jnp.zeros_like(kd_ref)

    @pl.when(phase == 0)
    def _():
        @pl.when(j == 0)
        def _():
            r_ref[...] = jnp.zeros_like(r_ref)
            phase0_block(cxt, NP_CTX, 0)

        phase0_block(lat, NP, NP_CTX + (NBLK - 1 - j) * NP)

    @pl.when(phase == 1)
    def _():
        @pl.when(j == 0)
        def _():
            s_ref[...] = jnp.zeros_like(s_ref)
            phase1_block(cxt, NP_CTX, 0, yc_ref)

        phase1_block(lat, NP, NP_CTX + j * NP, y_ref)


SCAN_STEPS = B * 2 * 2 * NBLK


def _scan_group(q, kT, v, g, extra, wsrc, *, gla):
    w_spec = pl.BlockSpec((None,) + wsrc.shape[1:],
                          lambda b, p, ph, j: (((b * 2 + p) * 2 + ph) * NBLK + j, 0, 0))

    def blk(b, ph, j, used_in_phase0):
        jj = jnp.where(ph == 0, NBLK - 1 - j, j)
        if not used_in_phase0:
            jj = jnp.where(ph == 0, 0, jj)
        return b * NBLK + jj

    def lat_specs(used0):
        return dict(
            row=lambda w: pl.BlockSpec((TB, w), lambda b, p, ph, j: (blk(b, ph, j, used0), p)),
            slab=pl.BlockSpec((NP, SP, SP), lambda b, p, ph, j: (blk(b, ph, j, used0), p, 0)))

    ctx_row = lambda w: pl.BlockSpec((L, w), lambda b, p, ph, j: (LAT_ROWS // L + b, p))
    ctx_slab = pl.BlockSpec((NP_CTX, SP, SP), lambda b, p, ph, j: (LAT_ROWS // L + b, p, 0))
    used, unused = lat_specs(True), lat_specs(False)

    in_specs = [unused["row"](2 * HEAD_DK), used["slab"], used["row"](2 * HEAD_DV), unused["row"](2 * HEAD_DV)]
    ctx_specs = [ctx_row(2 * HEAD_DK), ctx_slab, ctx_row(2 * HEAD_DV), ctx_row(2 * HEAD_DV)]
    if gla:
        lfT, lbT, ng = extra
        in_specs += [unused["slab"], used["slab"]]
        ctx_specs += [ctx_slab, ctx_slab]
        args = (q, kT, v, g, lfT, lbT, q, kT, v, g, lfT, lbT, ng, wsrc)
        in_specs = in_specs + ctx_specs + [pl.BlockSpec((1, HEAD_DV), lambda b, p, ph, j: (0, 0)), w_spec]
    else:
        (theta,) = extra
        args = (q, kT, v, g, q, kT, v, g, theta, wsrc)
        in_specs = in_specs + ctx_specs + [pl.BlockSpec(memory_space=pltpu.SMEM), w_spec]
    return pl.pallas_call(
        functools.partial(_scan_kernel, gla=gla),
        grid=(B, 2, 2, NBLK),
        in_specs=in_specs,
        out_specs=[
            pl.BlockSpec((TB, 2 * HEAD_DV), lambda b, p, ph, j: (b * NBLK + jnp.where(ph == 0, 0, j), p)),
            pl.BlockSpec((L, 2 * HEAD_DV), lambda b, p, ph, j: (b, p)),
            w_spec,
        ],
        out_shape=[jax.ShapeDtypeStruct((LAT_ROWS, V_W), BF16), jax.ShapeDtypeStruct((B * L, V_W), BF16),
                   jax.ShapeDtypeStruct(wsrc.shape, BF16)],
        scratch_shapes=[
            pltpu.VMEM((SP, HEAD_DV), F32),
            pltpu.VMEM((SP, HEAD_DV), F32),
            pltpu.VMEM((NP_ALL, SP, 2 * HEAD_DV), BF16),
            pltpu.VMEM((NP, SP, 2 * HEAD_DV), BF16),
            pltpu.VMEM((2 * NP, SP, HEAD_DV), F32),
            pltpu.VMEM((2 * NP, SP, HEAD_DV), F32),
            pltpu.VMEM((NP, 2 * SP, 2 * SP), BF16),
            pltpu.VMEM((NP, 2 * SP, 2 * SP), BF16),
            pltpu.VMEM((NP, 2 * SP, SP), BF16),
        ],
        compiler_params=pltpu.CompilerParams(
            dimension_semantics=("arbitrary",) * 4, vmem_limit_bytes=VMEM_LIMIT),
        name="scan_gla" if gla else "scan_ret",
    )(*args)


FF_CH = 512


def _out_kernel(*refs, layer, split_ctx):
    if split_ctx:
        (ya_ref, yb_ref, yac_ref, ybc_ref, x_ref, ctx_ref, mod_ref, lng_ref, lnb_ref,
         wo_ref, w1_ref, w2_ref, o_ref, wo_s) = refs
        is_ctx = pl.program_id(0) == N_LAT_TILES
        x = jnp.where(is_ctx, ctx_ref[...], x_ref[...])
        ya = jnp.where(is_ctx, yac_ref[...], ya_ref[...])
        yb = jnp.where(is_ctx, ybc_ref[...], yb_ref[...])
    else:
        ya_ref, yb_ref, x_ref, mod_ref, lng_ref, lnb_ref, wo_ref, w1_ref, w2_ref, o_ref, wo_s = refs
        x = x_ref[...]
        ya = ya_ref[...]
        yb = yb_ref[...]
    g1 = mod_ref[:, 2 * D:3 * D]
    sh2 = mod_ref[:, 3 * D:4 * D]
    sc2 = mod_ref[:, 4 * D:5 * D]
    g2 = mod_ref[:, 5 * D:6 * D]
    ln_g0 = lng_ref[2 * layer:2 * layer + 1, :]
    ln_g1 = lng_ref[2 * layer + 1:2 * layer + 2, :]
    ln_b0 = lnb_ref[2 * layer:2 * layer + 1, :]
    ln_b1 = lnb_ref[2 * layer + 1:2 * layer + 2, :]

    @pl.when(pl.program_id(0) == 0)
    def _():
        wo_s[...] = wo_ref[...].astype(BF16)

    half = D // 2
    y = _dot(ya, wo_s[0:half, :]) + _dot(yb, wo_s[half:D, :])
    x1 = _layer_norm(ALPHA * x + g1 * y, ln_g0, ln_b0)
    h2 = (x1 * (1.0 + sc2) + sh2).astype(BF16)
    acc = jnp.zeros((TM, D), F32)
    for c in range(D_FF // FF_CH):
        cols = slice(c * FF_CH, (c + 1) * FF_CH)
        hc = jnp.maximum(_dot(h2, w1_ref[:, cols]), 0.0)
        acc = acc + _dot((hc * hc).astype(BF16), w2_ref[cols, :])
    o_ref[...] = _layer_norm(ALPHA * x1 + g2 * acc, ln_g1, ln_b1)


def _out_mlp(ys, xs, ctx2, mods, ln, wo, w1, w2, *, layer, n_tiles, split_ctx):
    half = D // 2
    lat_row = lambda t: (jnp.minimum(t, N_LAT_TILES - 1), 0)
    if split_ctx:
        ya, yb, yac, ybc = ys
        in_specs = [pl.BlockSpec((TM, half), lat_row), pl.BlockSpec((TM, half), lat_row),
                    _full_spec((B * L, half)), _full_spec((B * L, half)),
                    pl.BlockSpec((TM, D), lat_row), _full_spec((B * L, D))]
        args = [ya, yb, yac, ybc, xs, ctx2]
    else:
        (att,) = ys
        in_specs = [pl.BlockSpec((TM, half), lambda t: (t, 0)), pl.BlockSpec((TM, half), lambda t: (t, 1)),
                    pl.BlockSpec((TM, D), lambda t: (t, 0))]
        args = [att, att, xs]
    stacked = lambda w, i: pl.BlockSpec((None,) + w.shape[1:], lambda t: (i, 0, 0), pipeline_mode=pl.Buffered(1))
    in_specs += [_mod_spec(layer), _full_spec(ln[0].shape), _full_spec(ln[1].shape),
                 stacked(wo, 0), stacked(w1, layer), stacked(w2, layer)]
    args += [mods, ln[0], ln[1], wo, w1, w2]
    return pl.pallas_call(
        functools.partial(_out_kernel, layer=layer, split_ctx=split_ctx),
        grid=(n_tiles,),
        in_specs=in_specs,
        out_specs=pl.BlockSpec((TM, D), lambda t: (t, 0)),
        out_shape=jax.ShapeDtypeStruct((n_tiles * TM, D), F32),
        compiler_params=pltpu.CompilerParams(
            dimension_semantics=("arbitrary",), vmem_limit_bytes=VMEM_LIMIT),
        scratch_shapes=[pltpu.VMEM((D, D), BF16)],
        name="out_mlp%d" % layer,
    )(*args)


def _in1_kernel(x_ref, mod_ref, cos_ref, sin_ref, w_ref, qT_o, k_o, vT_o, wqT_s, wk_s, wvT_s):
    @pl.when(pl.program_id(0) == 0)
    def _():
        wqT_s[...] = _transpose_bf16(w_ref[:, 0:D])
        wk_s[...] = w_ref[:, D:D + ATT_KVW].astype(BF16)
        wvT_s[...] = _transpose_bf16(w_ref[:, D + ATT_KVW:D + 2 * ATT_KVW])

    sh1 = mod_ref[:, 0:D]
    sc1 = mod_ref[:, D:2 * D]
    h = (x_ref[...] * (1.0 + sc1) + sh1).astype(BF16)
    nsl = TM // SP

    cos = cos_ref[...]
    sin = sin_ref[...]
    first = (lax.broadcasted_iota(jnp.int32, (TM, LANES), 1) % (2 * ROT)) < ROT
    sa = jnp.where(first, -sin, 0.0)
    sb = jnp.where(first, 0.0, sin)

    k = _dot(h, wk_s[...])
    for i in range(ATT_KVW // LANES):
        u = k[:, i * LANES:(i + 1) * LANES]
        r = u * cos + pltpu.roll(u, LANES - ROT, 1) * sa + pltpu.roll(u, ROT, 1) * sb
        k_o[:, i * LANES:(i + 1) * LANES] = r.astype(BF16)

    cosT, saT, sbT = cos.T, sa.T, sb.T
    qT = _dot_nt(wqT_s[...], h) * (LOG2E * ATT_DH ** -0.5)
    for i in range(D // LANES):
        u = qT[i * LANES:(i + 1) * LANES, :]
        r = (u * cosT + pltpu.roll(u, LANES - ROT, 0) * saT + pltpu.roll(u, ROT, 0) * sbT).astype(BF16)
        for s in range(nsl):
            qT_o[s, i * LANES:(i + 1) * LANES, :] = r[:, s * SP:(s + 1) * SP]

    vT = _dot_nt(wvT_s[...], h).astype(BF16)
    for s in range(nsl):
        vT_o[s] = vT[:, s * SP:(s + 1) * SP]


def _in_proj1(xs, mods, cos_tab, sin_tab, wqkv):
    tile = lambda t: jnp.where(t == N_LAT_TILES, TILES_PER_BATCH, t % TILES_PER_BATCH)
    row = lambda t: (t, 0)
    slab = lambda t: (t, 0, 0)
    nsl = TM // SP
    tab_spec = pl.BlockSpec((TM, LANES), lambda t: (tile(t), 0))
    return pl.pallas_call(
        _in1_kernel,
        grid=(N_TILES,),
        in_specs=[pl.BlockSpec((TM, D), row), _mod_spec(1), tab_spec, tab_spec,
                  pl.BlockSpec((None,) + wqkv.shape[1:], lambda t: (0, 0, 0), pipeline_mode=pl.Buffered(1))],
        out_specs=[pl.BlockSpec((nsl, D, SP), slab), pl.BlockSpec((TM, ATT_KVW), row),
                   pl.BlockSpec((nsl, ATT_KVW, SP), slab)],
        out_shape=[jax.ShapeDtypeStruct((N_SLABS, D, SP), BF16), jax.ShapeDtypeStruct((ROWS, ATT_KVW), BF16),
                   jax.ShapeDtypeStruct((N_SLABS, ATT_KVW, SP), BF16)],
        scratch_shapes=[pltpu.VMEM((D, D), BF16), pltpu.VMEM((D, ATT_KVW), BF16), pltpu.VMEM((ATT_KVW, D), BF16)],
        compiler_params=pltpu.CompilerParams(
            dimension_semantics=("arbitrary",), vmem_limit_bytes=VMEM_LIMIT),
        name="in_proj1",
    )(xs, mods, cos_tab, sin_tab, wqkv)


def _attn_kernel(sink_ref, qT_ref, k_ref, vT_ref, kc_ref, vcT_ref, o_ref, bias_ref):
    n = pl.program_id(1)
    nql = ATT_GROUP * QB
    lane_g = lax.broadcasted_iota(jnp.int32, (1, nql), 1) // QB
    zero_half = jnp.zeros((ATT_DH, nql), BF16)
    ones_rows = jnp.ones((BF16_ROWS, KWIN + L), BF16)

    @pl.when((pl.program_id(0) == 0) & (n == 0))
    def _():
        kj = lax.broadcasted_iota(jnp.int32, (KWIN, QB), 0)
        qi = lax.broadcasted_iota(jnp.int32, (KWIN, QB), 1)
        for case, delta in enumerate((-QB, 0, -2 * QB)):
            d = kj - qi + delta
            bias_ref[case] = jnp.where((d >= -WINDOW) & (d <= WINDOW), 0.0, -jnp.inf)

    def scores(i, kh):
        n0 = (n * (TQ // QB) + i) * QB
        start = pl.multiple_of(jnp.clip(n0 - QB, 0, T - KWIN), QB)
        bias1 = bias_ref[jnp.where(n0 == 0, 1, jnp.where(n0 == T - QB, 2, 0))]
        bias = jnp.concatenate([bias1] * ATT_GROUP, axis=1)
        pair = slice((kh // 2) * LANES, (kh // 2 + 1) * LANES)
        qT = jnp.concatenate(
            [qT_ref[i, (kh * ATT_GROUP + g) * ATT_DH:(kh * ATT_GROUP + g + 1) * ATT_DH, :]
             for g in range(ATT_GROUP)], axis=1)
        q_pad = jnp.concatenate([zero_half, qT] if kh % 2 else [qT, zero_half], axis=0)
        s_loc = _dot(k_ref[pl.ds(start, KWIN), pair], q_pad) + bias
        s_ctx = _dot(kc_ref[:, pair], q_pad)
        return s_loc, s_ctx, start // SP

    def finish(i, kh, s_loc, s_ctx, slab0):
        drows = slice(kh * ATT_DH, (kh + 1) * ATT_DH)
        sink = jnp.zeros((1, nql), F32)
        for g in range(ATT_GROUP):
            sink = jnp.where(lane_g == g, sink_ref[0, kh * ATT_GROUP + g] * LOG2E, sink)
        m = jnp.maximum(jnp.maximum(jnp.max(s_loc, axis=0, keepdims=True),
                                    jnp.max(s_ctx, axis=0, keepdims=True)), sink)
        pT = jnp.concatenate([jnp.exp2(s_loc - m).astype(BF16),
                              jnp.exp2(s_ctx - m).astype(BF16)], axis=0)
        vT = jnp.concatenate([vT_ref[slab0 + t, drows, :] for t in range(KWIN // SP)]
                             + [vcT_ref[t, drows, :] for t in range(L // SP)], axis=1)
        o_ext = _dot(jnp.concatenate([vT, ones_rows], axis=0), pT)
        denom = o_ext[ATT_DH:ATT_DH + 1, :] + jnp.exp2(sink - m)
        oT = o_ext[0:ATT_DH, :] / denom
        for g in range(0, ATT_GROUP, 2):
            two = jnp.concatenate([oT[:, g * QB:(g + 1) * QB], oT[:, (g + 1) * QB:(g + 2) * QB]], axis=0)
            c0 = (kh * ATT_GROUP + g) * ATT_DH
            o_ref[i * QB:(i + 1) * QB, c0:c0 + 2 * ATT_DH] = two.T.astype(BF16)

    items = [(i, kh) for i in range(TQ // QB) for kh in range(ATT_KVH)]
    pending = [scores(*it) for it in items[:SCORE_AHEAD]]
    for t, item in enumerate(items):
        if t + SCORE_AHEAD < len(items):
            pending.append(scores(*items[t + SCORE_AHEAD]))
        finish(*item, *pending.pop(0))


def _attention(sink, qT, k, vT):
    nq = T // TQ
    return pl.pallas_call(
        _attn_kernel,
        grid=(B, nq),
        in_specs=[
            pl.BlockSpec(memory_space=pltpu.SMEM),
            pl.BlockSpec((TQ // SP, D, SP), lambda b, n: (b * nq + n, 0, 0)),
            pl.BlockSpec((T, ATT_KVW), lambda b, n: (b, 0)),
            pl.BlockSpec((T // SP, ATT_KVW, SP), lambda b, n: (b, 0, 0)),
            pl.BlockSpec((L, ATT_KVW), lambda b, n: (LAT_ROWS // L + b, 0)),
            pl.BlockSpec((L // SP, ATT_KVW, SP), lambda b, n: (LAT_ROWS // L + b, 0, 0)),
        ],
        out_specs=pl.BlockSpec((TQ, D), lambda b, n: (b * nq + n, 0)),
        out_shape=jax.ShapeDtypeStruct((LAT_ROWS, D), BF16),
        compiler_params=pltpu.CompilerParams(
            dimension_semantics=("arbitrary", "arbitrary"), vmem_limit_bytes=VMEM_LIMIT),
        scratch_shapes=[pltpu.VMEM((3, KWIN, QB), F32)],
        name="window_attn",
    )(sink, qT, k, vT, k, vT)


def _rope_tables():
    half = ATT_DH // 2
    inv_freq = np.power(np.float32(ROPE_BASE), -np.arange(0, half, 2, dtype=np.float32) / np.float32(half))
    inv_freq = inv_freq.astype(np.float32)
    pos = np.arange(T)
    ang_r = (pos // GRID_W).astype(np.float32)[:, None] * inv_freq[None, :]
    ang_c = (pos % GRID_W).astype(np.float32)[:, None] * inv_freq[None, :]
    ang = np.concatenate([ang_r, ang_r, ang_c, ang_c], axis=1)
    ang = np.concatenate([ang, np.zeros((TM, ATT_DH), np.float32)], axis=0)
    ang = np.concatenate([ang, ang], axis=1)
    return np.cos(ang).astype(np.float32), np.sin(ang).astype(np.float32)


def kernel(x, c, ctx, c_ctx, w_mod, b_mod, ln_g, ln_b, mlp_w1, mlp_w2, ev_w_in, ev_ret_theta, ev_gla_gk_w,
           ev_gla_gk_b, ev_gla_norm_g, ev_w_out, od_w_qkv, od_sink, od_w_out):
    x2 = x.reshape(LAT_ROWS, D)
    ctx2 = ctx.reshape(B * L, D)

    cs = jnp.concatenate([c, c_ctx[None, :], jnp.zeros((MOD_ROWS - B - 1, D), F32)], axis=0)
    mods = _modulation(cs, w_mod, b_mod).reshape(DEPTH * MOD_ROWS, 1, 6 * D)
    ln = (ln_g.reshape(2 * DEPTH, D), ln_b.reshape(2 * DEPTH, D))

    gk_w = ev_gla_gk_w[0]
    zeros = jnp.zeros((GATE_RANK, QK_W), F32)
    g2 = jnp.concatenate([jnp.concatenate([gk_w[0], zeros], axis=1),
                          jnp.concatenate([zeros, gk_w[1]], axis=1)], axis=0)
    qa, kaT, va, ga, qb, kbT, vb, gbv, lfT, lbT = _in_proj0(
        x2, ctx2, mods, ev_w_in[0].T, g2.T.astype(BF16), ev_gla_gk_b[0].reshape(2 * QK_W, 1))

    w1_rows = DEPTH * D // SCAN_STEPS
    w2_rows = DEPTH * D_FF // SCAN_STEPS
    y_ret, yc_ret, w1 = _scan_group(qa, kaT, va, ga, (ev_ret_theta,),
                                    mlp_w1.reshape(SCAN_STEPS, w1_rows, D_FF), gla=False)
    y_gla, yc_gla, w2 = _scan_group(qb, kbT, vb, gbv, (lfT, lbT, ev_gla_norm_g),
                                    mlp_w2.reshape(SCAN_STEPS, w2_rows, D), gla=True)
    w1 = w1.reshape(DEPTH, D, D_FF)
    w2 = w2.reshape(DEPTH, D_FF, D)

    xs = _out_mlp((y_ret, y_gla, yc_ret, yc_gla), x2, ctx2, mods, ln, ev_w_out, w1, w2,
                  layer=0, n_tiles=N_TILES, split_ctx=True)

    cos_tab, sin_tab = _rope_tables()
    q1T, k1, v1T = _in_proj1(xs, mods, cos_tab, sin_tab, od_w_qkv)
    att = _attention(od_sink, q1T, k1, v1T)
    out = _out_mlp((att,), xs, None, mods, ln, od_w_out, w1, w2,
                   layer=1, n_tiles=N_LAT_TILES, split_ctx=False)
    return out.reshape(B, T, D)
```

```python
import functools

import jax
import jax.numpy as jnp
import numpy as np
from jax import lax
from jax.experimental import pallas as pl
from jax.experimental.pallas import tpu as pltpu

F32 = jnp.float32
BF16 = jnp.bfloat16

D = 1024
B = 2
T = 8192
L = 256
DEPTH = 2
GRID_W = 64
D_FF = 4 * D
HEAD_DV = 128
HEAD_DK = 64
GATE_RANK = 16
GATE_TAU = 16.0
QK_W = 256
V_W = 512
ATT_DH = 64
ATT_QH = 16
ATT_KVH = 4
ATT_GROUP = 4
ATT_KVW = ATT_KVH * ATT_DH
WINDOW = 128
ROPE_BASE = 10000.0
ALPHA = (2.0 * DEPTH) ** 0.25
LN_EPS = 1e-5
RMS_EPS = 1e-6

LANES = 128
SUBLANES = 8
BF16_ROWS = 2 * SUBLANES
MOD_ROWS = SUBLANES
ROT = ATT_DH // 4

TM = 512
LAT_ROWS = B * T
ROWS = LAT_ROWS + B * L
N_LAT_TILES = LAT_ROWS // TM
N_TILES = ROWS // TM
TILES_PER_BATCH = T // TM

CH = 64
SP = 2 * CH
TB = 4096
NP = TB // SP
NP_CTX = L // SP
NBLK = T // TB
NP_ALL = NP_CTX + T // SP
N_SLABS = ROWS // SP
assert SP == LANES and 2 * HEAD_DK == LANES and HEAD_DV == LANES
KV_UNROLL = 32
OUT_UNROLL = 32

TQ = 2048
QB = 128
KWIN = 3 * QB
SCORE_AHEAD = 2
LOG2E = 1.4426950408889634

VMEM_LIMIT = 56 * 1024 * 1024


def _dot(a, b):
    return jnp.dot(a, b, preferred_element_type=F32)


def _dot_nt(a, b):
    return lax.dot_general(a, b, (((1,), (1,)), ((), ())), preferred_element_type=F32)


def _full_spec(shape):
    nd = len(shape)
    return pl.BlockSpec(shape, lambda *_: (0,) * nd, pipeline_mode=pl.Buffered(1))


def _mod_row(t):
    return jnp.minimum(t // TILES_PER_BATCH, B)


def _mod_spec(layer):
    return pl.BlockSpec((None, 1, 6 * D), lambda t: (layer * MOD_ROWS + _mod_row(t), 0, 0))


def _layer_norm(x, g, b):
    mu = jnp.mean(x, axis=-1, keepdims=True)
    xc = x - mu
    var = jnp.mean(xc * xc, axis=-1, keepdims=True)
    return xc * lax.rsqrt(var + LN_EPS) * g + b


def _log_sigmoid(z):
    return jnp.minimum(z, 0.0) - jnp.log(1.0 + jnp.exp(-jnp.abs(z)))


def _silu(x):
    half = 0.5 * x
    return half + half * jnp.tanh(half)


MOD_TN = 1536


def _mod_kernel(c_ref, w_ref, b_ref, o_ref):
    s = jax.nn.silu(c_ref[...])
    s_hi = s.astype(BF16)
    s_lo = (s - s_hi.astype(F32)).astype(BF16)
    w = w_ref[...].astype(BF16)
    o_ref[...] = _dot(s_hi, w) + _dot(s_lo, w) + b_ref[...]


def _modulation(cs, w_mod, b_mod):
    return pl.pallas_call(
        _mod_kernel,
        grid=(DEPTH, 6 * D // MOD_TN),
        in_specs=[
            pl.BlockSpec((MOD_ROWS, D), lambda i, n: (0, 0)),
            pl.BlockSpec((None, D, MOD_TN), lambda i, n: (i, 0, n)),
            pl.BlockSpec((None, 1, MOD_TN), lambda i, n: (i, 0, n)),
        ],
        out_specs=pl.BlockSpec((None, MOD_ROWS, MOD_TN), lambda i, n: (i, 0, n)),
        out_shape=jax.ShapeDtypeStruct((DEPTH, MOD_ROWS, 6 * D), F32),
        compiler_params=pltpu.CompilerParams(
            dimension_semantics=("arbitrary", "arbitrary"), vmem_limit_bytes=VMEM_LIMIT),
        name="modulation",
    )(cs, w_mod, b_mod.reshape(DEPTH, 1, 6 * D))


IN0_OFF = (0, 256, 512, 1024, 1536, 1792, 2048, 2560, 3072, 3104)


def _transpose_bf16(w):
    return w.astype(F32).T.astype(BF16)


def _in0_kernel(x_ref, ctx_ref, mod_ref, wT_ref, g2T, gbc,
                qa_o, kaT_o, va_o, ga_o, qb_o, kbT_o, vb_o, gb_o, lfT_o, lbT_o, wbT_s):
    t = pl.program_id(0)
    rows = lambda i: slice(IN0_OFF[i], IN0_OFF[i + 1])
    piece = lambda i: wbT_s[rows(i), :]

    @pl.when(t == 0)
    def _():
        for i in range(len(IN0_OFF) - 1):
            wbT_s[rows(i), :] = wT_ref[rows(i), :].astype(BF16)

    xt = jnp.where(t == N_LAT_TILES, ctx_ref[...], x_ref[...])
    sh1 = mod_ref[:, 0:D]
    sc1 = mod_ref[:, D:2 * D]
    h = (xt * (1.0 + sc1) + sh1).astype(BF16)
    qk_scale = HEAD_DK ** -0.5

    def put_slabs(o_ref, val):
        for i in range(TM // SP):
            o_ref[i] = val[:, i * SP:(i + 1) * SP].astype(o_ref.dtype)

    lrT = _dot_nt(piece(8), h).astype(BF16)
    ga_o[...] = _silu(_dot_nt(h, piece(3))).astype(BF16)
    lsT = _log_sigmoid(_dot(g2T[...], lrT) + gbc[...]) * (1.0 / GATE_TAU)
    put_slabs(lfT_o, lsT[0:QK_W, :])
    put_slabs(lbT_o, lsT[QK_W:2 * QK_W, :])
    gb_o[...] = _silu(_dot_nt(h, piece(7))).astype(BF16)

    put_slabs(kaT_o, _dot_nt(piece(1), h) * qk_scale)
    put_slabs(kbT_o, _dot_nt(piece(5), h))
    qa_o[...] = _dot_nt(h, piece(0)).astype(BF16)
    qb_o[...] = (_dot_nt(h, piece(4)) * qk_scale).astype(BF16)
    va_o[...] = _dot_nt(h, piece(2)).astype(BF16)
    vb_o[...] = _dot_nt(h, piece(6)).astype(BF16)


def _in_proj0(x2, ctx2, mods, w_inT, g2T, gbc):
    row = lambda t: (t, 0)
    slab = lambda t: (t, 0, 0)
    nsl = TM // SP
    row_out = lambda width: (jax.ShapeDtypeStruct((ROWS, width), BF16), pl.BlockSpec((TM, width), row))
    slab_out = lambda dt: (jax.ShapeDtypeStruct((N_SLABS, QK_W, SP), dt), pl.BlockSpec((nsl, QK_W, SP), slab))
    outs = [row_out(QK_W), slab_out(BF16), row_out(V_W), row_out(V_W),
            row_out(QK_W), slab_out(BF16), row_out(V_W), row_out(V_W),
            slab_out(BF16), slab_out(BF16)]
    in_specs = [
        pl.BlockSpec((TM, D), lambda t: (jnp.minimum(t, N_LAT_TILES - 1), 0)),
        _full_spec((B * L, D)),
        _mod_spec(0),
        _full_spec(w_inT.shape), _full_spec(g2T.shape), _full_spec(gbc.shape),
    ]
    return pl.pallas_call(
        _in0_kernel,
        grid=(N_TILES,),
        in_specs=in_specs,
        out_specs=[o[1] for o in outs],
        out_shape=[o[0] for o in outs],
        scratch_shapes=[pltpu.VMEM(w_inT.shape, BF16)],
        compiler_params=pltpu.CompilerParams(
            dimension_semantics=("arbitrary",), vmem_limit_bytes=VMEM_LIMIT),
        name="in_proj0",
    )(x2, ctx2, mods, w_inT, g2T, gbc)


def _chunk_diag(kv, c):
    r0 = c * 2 * HEAD_DK
    return jnp.concatenate([kv[r0:r0 + HEAD_DK, 0:HEAD_DV],
                            kv[r0 + HEAD_DK:r0 + 2 * HEAD_DK, HEAD_DV:2 * HEAD_DV]], axis=0)


def _scan_kernel(*refs, gla):
    if gla:
        (q_ref, kT_ref, v_ref, g_ref, lfT_ref, lbT_ref,
         qc_ref, kTc_ref, vc_ref, gc_ref, lfTc_ref, lbTc_ref, ng_ref, wsrc_ref,
         y_ref, yc_ref, wdst_ref, s_ref, r_ref, rst_ref, sst_ref, kv_ref, dec_ref,
         kd_ref, lhs_ref, pm_ref) = refs
        lat = (q_ref, kT_ref, v_ref, g_ref, lfT_ref, lbT_ref)
        cxt = (qc_ref, kTc_ref, vc_ref, gc_ref, lfTc_ref, lbTc_ref)
    else:
        (q_ref, kT_ref, v_ref, g_ref, qc_ref, kTc_ref, vc_ref, gc_ref, th_ref, wsrc_ref,
         y_ref, yc_ref, wdst_ref, s_ref, r_ref, rst_ref, sst_ref, kv_ref, dec_ref,
         kd_ref, lhs_ref, pm_ref) = refs
        lat = (q_ref, kT_ref, v_ref, g_ref, None, None)
        cxt = (qc_ref, kTc_ref, vc_ref, gc_ref, None, None)

    phase = pl.program_id(2)
    j = pl.program_id(3)

    wdst_ref[...] = wsrc_ref[...].astype(BF16)

    ri = lax.broadcasted_iota(jnp.int32, (SP, SP), 0)
    ci = lax.broadcasted_iota(jnp.int32, (SP, SP), 1)
    same = (ri // CH) == (ci // CH)
    first_lane = ci < CH
    head_a = ci < HEAD_DK

    if gla:
        as_w = lambda m: m.astype(BF16)
        tot = jnp.concatenate([jnp.broadcast_to(ri < CH, (SP, SP)), jnp.broadcast_to(ri >= CH, (SP, SP))], axis=1)
        w_end_f = jnp.concatenate([as_w(same & (ri > ci)), as_w(tot)], axis=1)
        w_end_b = jnp.concatenate([as_w(same & (ri < ci)), as_w(tot)], axis=1)
        w_cum_f = as_w(same & (ri <= ci))
        w_cum_b = as_w(same & (ri >= ci))
    else:
        hp = pl.program_id(1)
        th = [[th_ref[0, dr, 2 * hp + hd] for hd in range(2)] for dr in range(2)]
        lane1 = lax.broadcasted_iota(jnp.int32, (1, SP), 1) < HEAD_DK
        row1 = lax.broadcasted_iota(jnp.int32, (SP, 1), 0) < HEAD_DK
        lg_row = [jnp.log1p(-jnp.exp(jnp.where(lane1, th[dr][0], th[dr][1]))) for dr in range(2)]
        lg_col = [jnp.log1p(-jnp.exp(jnp.where(row1, th[dr][0], th[dr][1]))) for dr in range(2)]
        it = (ci % CH).astype(F32)
        ir = (ri % CH).astype(F32)
        ret_end_f = jnp.exp((CH - 1.0 - it) * lg_col[0])
        ret_end_b = jnp.exp(it * lg_col[1])
        ret_dec_f = jnp.exp(jnp.broadcast_to(CH * lg_col[0], (SP, SP)))
        ret_dec_b = jnp.exp(jnp.broadcast_to(CH * lg_col[1], (SP, SP)))
        ret_ebTi = jnp.exp(-(it + 1.0) * lg_col[0])
        ret_erTi = jnp.exp(-(CH - it) * lg_col[1])
        ret_eb = jnp.exp((ir + 1.0) * lg_row[0])
        ret_er = jnp.exp((CH - ir) * lg_row[1])

    def kv_stage(blk, n, fwd):
        _, kT_r, v_r, _, lfT_r, lbT_r = blk

        def body(p, carry):
            kT = kT_r[p].astype(F32)
            v = v_r[pl.ds(pl.multiple_of(p * SP, SP), SP), :]
            if gla:
                res = _dot((lfT_r if fwd else lbT_r)[p], w_end_f if fwd else w_end_b)
                e_end = jnp.exp(res[:, 0:SP])
                dec0 = jnp.exp(res[:, SP:2 * SP])
                dec1 = jnp.exp(res[:, 2 * SP:3 * SP])
            else:
                e_end = ret_end_f if fwd else ret_end_b
                dec0 = dec1 = ret_dec_f if fwd else ret_dec_b
            ke = kT * e_end
            lhs = jnp.concatenate([jnp.where(first_lane, ke, 0.0), jnp.where(first_lane, 0.0, ke)],
                                  axis=0).astype(BF16)
            kv = _dot(lhs, v)
            kv_ref[2 * p] = _chunk_diag(kv, 0)
            kv_ref[2 * p + 1] = _chunk_diag(kv, 1)
            dec_ref[2 * p] = dec0
            dec_ref[2 * p + 1] = dec1
            return carry

        lax.fori_loop(0, n, body, 0, unroll=min(n, KV_UNROLL))

    def phase0_block(blk, n, slot0):
        kv_stage(blk, n, fwd=False)

        def body(i, r_state):
            p = n - 1 - i
            rst_ref[slot0 + p, :, HEAD_DV:2 * HEAD_DV] = r_state.astype(BF16)
            r_state = dec_ref[2 * p + 1] * r_state + kv_ref[2 * p + 1]
            rst_ref[slot0 + p, :, 0:HEAD_DV] = r_state.astype(BF16)
            return dec_ref[2 * p] * r_state + kv_ref[2 * p]

        r_ref[...] = lax.fori_loop(0, n, body, r_ref[...])

    def phase1_block(blk, n, slot0, out_ref):
        q_r, kT_r, v_r, g_r, lfT_r, lbT_r = blk
        kv_stage(blk, n, fwd=True)

        def rec(p, s_state):
            sst_ref[p, :, 0:HEAD_DV] = s_state.astype(BF16)
            s_state = dec_ref[2 * p] * s_state + kv_ref[2 * p]
            sst_ref[p, :, HEAD_DV:2 * HEAD_DV] = s_state.astype(BF16)
            return dec_ref[2 * p + 1] * s_state + kv_ref[2 * p + 1]

        s_ref[...] = lax.fori_loop(0, n, rec, s_ref[...])

        r2 = lax.broadcasted_iota(jnp.int32, (2 * SP, SP), 0) % SP
        c2 = lax.broadcasted_iota(jnp.int32, (2 * SP, SP), 1)
        same2 = (r2 // CH) == (c2 // CH)
        mask_f = same2 & (r2 >= c2)
        mask_b = same2 & (r2 < c2)

        def prep(p, carry):
            rows = pl.ds(pl.multiple_of(p * SP, SP), SP)
            q = q_r[rows, :].astype(F32)
            kT = kT_r[p].astype(F32)
            if gla:
                lfT = lfT_r[p]
                lbT = lbT_r[p]
                ebTi = jnp.exp(-_dot(lfT, w_cum_f))
                erTi = jnp.exp(-_dot(lbT, w_cum_b))
                e_b = jnp.exp(_dot_nt(w_cum_b, lfT))
                e_r = jnp.exp(_dot_nt(w_cum_f, lbT))
            else:
                ebTi, erTi, e_b, e_r = ret_ebTi, ret_erTi, ret_eb, ret_er
            kd_ref[p, 0:SP, 0:SP] = (kT * ebTi).astype(BF16)
            kd_ref[p, SP:2 * SP, SP:2 * SP] = (kT * erTi).astype(BF16)
            qf = q * e_b
            qb = q * e_r
            lhs_ref[p, :, 0:SP] = jnp.concatenate(
                [jnp.where(head_a, qf, 0.0), jnp.where(head_a, 0.0, qf)], axis=0).astype(BF16)
            lhs_ref[p, :, SP:2 * SP] = jnp.concatenate(
                [jnp.where(head_a, qb, 0.0), jnp.where(head_a, 0.0, qb)], axis=0).astype(BF16)
            return carry

        def score(p, carry):
            sc = _dot(lhs_ref[p], kd_ref[p])
            pm_ref[p] = jnp.where(mask_f, sc[:, 0:SP], jnp.where(mask_b, sc[:, SP:2 * SP], 0.0)).astype(BF16)
            return carry

        def emit(p, carry):
            rows = pl.ds(pl.multiple_of(p * SP, SP), SP)
            v = v_r[rows, :]
            states = jnp.concatenate([sst_ref[p], rst_ref[slot0 + p]], axis=0)
            o_int = _dot(lhs_ref[p], states)
            o_a = _dot(pm_ref[p, 0:SP, :], v[:, 0:HEAD_DV]) + jnp.concatenate(
                [o_int[0:CH, 0:HEAD_DV], o_int[CH:SP, HEAD_DV:2 * HEAD_DV]], axis=0)
            o_b = _dot(pm_ref[p, SP:2 * SP, :], v[:, HEAD_DV:2 * HEAD_DV]) + jnp.concatenate(
                [o_int[SP:SP + CH, 0:HEAD_DV], o_int[SP + CH:2 * SP, HEAD_DV:2 * HEAD_DV]], axis=0)

            def nrm(o):
                y = o * lax.rsqrt(jnp.mean(o * o, axis=-1, keepdims=True) + RMS_EPS)
                return y * ng_ref[...] if gla else y

            y = jnp.concatenate([nrm(o_a), nrm(o_b)], axis=1) * g_r[rows, :].astype(F32)
            out_ref[rows, :] = y.astype(BF16)
            return carry

        lax.fori_loop(0, n, prep, 0, unroll=min(n, OUT_UNROLL))
        lax.fori_loop(0, n, score, 0, unroll=min(n, OUT_UNROLL))
        lax.fori_loop(0, n, emit, 0, unroll=min(n, OUT_UNROLL))

    @pl.when((pl.program_id(0) == 0) & (pl.program_id(1) == 0) & (phase == 0) & (j == 0))
    def _():
        kd_ref[...] = jnp.zeros_like(kd_ref)

    @pl.when(phase == 0)
    def _():
        @pl.when(j == 0)
        def _():
            r_ref[...] = jnp.zeros_like(r_ref)
            phase0_block(cxt, NP_CTX, 0)

        phase0_block(lat, NP, NP_CTX + (NBLK - 1 - j) * NP)

    @pl.when(phase == 1)
    def _():
        @pl.when(j == 0)
        def _():
            s_ref[...] = jnp.zeros_like(s_ref)
            phase1_block(cxt, NP_CTX, 0, yc_ref)

        phase1_block(lat, NP, NP_CTX + j * NP, y_ref)


SCAN_STEPS = B * 2 * 2 * NBLK


def _scan_group(q, kT, v, g, extra, wsrc, *, gla):
    w_spec = pl.BlockSpec((None,) + wsrc.shape[1:],
                          lambda b, p, ph, j: (((b * 2 + p) * 2 + ph) * NBLK + j, 0, 0))

    def blk(b, ph, j, used_in_phase0):
        jj = jnp.where(ph == 0, NBLK - 1 - j, j)
        if not used_in_phase0:
            jj = jnp.where(ph == 0, 0, jj)
        return b * NBLK + jj

    def lat_specs(used0):
        return dict(
            row=lambda w: pl.BlockSpec((TB, w), lambda b, p, ph, j: (blk(b, ph, j, used0), p)),
            slab=pl.BlockSpec((NP, SP, SP), lambda b, p, ph, j: (blk(b, ph, j, used0), p, 0)))

    ctx_row = lambda w: pl.BlockSpec((L, w), lambda b, p, ph, j: (LAT_ROWS // L + b, p))
    ctx_slab = pl.BlockSpec((NP_CTX, SP, SP), lambda b, p, ph, j: (LAT_ROWS // L + b, p, 0))
    used, unused = lat_specs(True), lat_specs(False)

    in_specs = [unused["row"](2 * HEAD_DK), used["slab"], used["row"](2 * HEAD_DV), unused["row"](2 * HEAD_DV)]
    ctx_specs = [ctx_row(2 * HEAD_DK), ctx_slab, ctx_row(2 * HEAD_DV), ctx_row(2 * HEAD_DV)]
    if gla:
        lfT, lbT, ng = extra
        in_specs += [unused["slab"], used["slab"]]
        ctx_specs += [ctx_slab, ctx_slab]
        args = (q, kT, v, g, lfT, lbT, q, kT, v, g, lfT, lbT, ng, wsrc)
        in_specs = in_specs + ctx_specs + [pl.BlockSpec((1, HEAD_DV), lambda b, p, ph, j: (0, 0)), w_spec]
    else:
        (theta,) = extra
        args = (q, kT, v, g, q, kT, v, g, theta, wsrc)
        in_specs = in_specs + ctx_specs + [pl.BlockSpec(memory_space=pltpu.SMEM), w_spec]
    return pl.pallas_call(
        functools.partial(_scan_kernel, gla=gla),
        grid=(B, 2, 2, NBLK),
        in_specs=in_specs,
        out_specs=[
            pl.BlockSpec((TB, 2 * HEAD_DV), lambda b, p, ph, j: (b * NBLK + jnp.where(ph == 0, 0, j), p)),
            pl.BlockSpec((L, 2 * HEAD_DV), lambda b, p, ph, j: (b, p)),
            w_spec,
        ],
        out_shape=[jax.ShapeDtypeStruct((LAT_ROWS, V_W), BF16), jax.ShapeDtypeStruct((B * L, V_W), BF16),
                   jax.ShapeDtypeStruct(wsrc.shape, BF16)],
        scratch_shapes=[
            pltpu.VMEM((SP, HEAD_DV), F32),
            pltpu.VMEM((SP, HEAD_DV), F32),
            pltpu.VMEM((NP_ALL, SP, 2 * HEAD_DV), BF16),
            pltpu.VMEM((NP, SP, 2 * HEAD_DV), BF16),
            pltpu.VMEM((2 * NP, SP, HEAD_DV), F32),
            pltpu.VMEM((2 * NP, SP, HEAD_DV), F32),
            pltpu.VMEM((NP, 2 * SP, 2 * SP), BF16),
            pltpu.VMEM((NP, 2 * SP, 2 * SP), BF16),
            pltpu.VMEM((NP, 2 * SP, SP), BF16),
        ],
        compiler_params=pltpu.CompilerParams(
            dimension_semantics=("arbitrary",) * 4, vmem_limit_bytes=VMEM_LIMIT),
        name="scan_gla" if gla else "scan_ret",
    )(*args)


FF_CH = 512


def _out_kernel(*refs, layer, split_ctx):
    if split_ctx:
        (ya_ref, yb_ref, yac_ref, ybc_ref, x_ref, ctx_ref, mod_ref, lng_ref, lnb_ref,
         wo_ref, w1_ref, w2_ref, o_ref, wo_s) = refs
        is_ctx = pl.program_id(0) == N_LAT_TILES
        x = jnp.where(is_ctx, ctx_ref[...], x_ref[...])
        ya = jnp.where(is_ctx, yac_ref[...], ya_ref[...])
        yb = jnp.where(is_ctx, ybc_ref[...], yb_ref[...])
    else:
        ya_ref, yb_ref, x_ref, mod_ref, lng_ref, lnb_ref, wo_ref, w1_ref, w2_ref, o_ref, wo_s = refs
        x = x_ref[...]
        ya = ya_ref[...]
        yb = yb_ref[...]
    g1 = mod_ref[:, 2 * D:3 * D]
    sh2 = mod_ref[:, 3 * D:4 * D]
    sc2 = mod_ref[:, 4 * D:5 * D]
    g2 = mod_ref[:, 5 * D:6 * D]
    ln_g0 = lng_ref[2 * layer:2 * layer + 1, :]
    ln_g1 = lng_ref[2 * layer + 1:2 * layer + 2, :]
    ln_b0 = lnb_ref[2 * layer:2 * layer + 1, :]
    ln_b1 = lnb_ref[2 * layer + 1:2 * layer + 2, :]

    @pl.when(pl.program_id(0) == 0)
    def _():
        wo_s[...] = wo_ref[...].astype(BF16)

    half = D // 2
    y = _dot(ya, wo_s[0:half, :]) + _dot(yb, wo_s[half:D, :])
    x1 = _layer_norm(ALPHA * x + g1 * y, ln_g0, ln_b0)
    h2 = (x1 * (1.0 + sc2) + sh2).astype(BF16)
    acc = jnp.zeros((TM, D), F32)
    for c in range(D_FF // FF_CH):
        cols = slice(c * FF_CH, (c + 1) * FF_CH)
        hc = jnp.maximum(_dot(h2, w1_ref[:, cols]), 0.0)
        acc = acc + _dot((hc * hc).astype(BF16), w2_ref[cols, :])
    o_ref[...] = _layer_norm(ALPHA * x1 + g2 * acc, ln_g1, ln_b1)


def _out_mlp(ys, xs, ctx2, mods, ln, wo, w1, w2, *, layer, n_tiles, split_ctx):
    half = D // 2
    lat_row = lambda t: (jnp.minimum(t, N_LAT_TILES - 1), 0)
    if split_ctx:
        ya, yb, yac, ybc = ys
        in_specs = [pl.BlockSpec((TM, half), lat_row), pl.BlockSpec((TM, half), lat_row),
                    _full_spec((B * L, half)), _full_spec((B * L, half)),
                    pl.BlockSpec((TM, D), lat_row), _full_spec((B * L, D))]
        args = [ya, yb, yac, ybc, xs, ctx2]
    else:
        (att,) = ys
        in_specs = [pl.BlockSpec((TM, half), lambda t: (t, 0)), pl.BlockSpec((TM, half), lambda t: (t, 1)),
                    pl.BlockSpec((TM, D), lambda t: (t, 0))]
        args = [att, att, xs]
    stacked = lambda w, i: pl.BlockSpec((None,) + w.shape[1:], lambda t: (i, 0, 0), pipeline_mode=pl.Buffered(1))
    in_specs += [_mod_spec(layer), _full_spec(ln[0].shape), _full_spec(ln[1].shape),
                 stacked(wo, 0), stacked(w1, layer), stacked(w2, layer)]
    args += [mods, ln[0], ln[1], wo, w1, w2]
    return pl.pallas_call(
        functools.partial(_out_kernel, layer=layer, split_ctx=split_ctx),
        grid=(n_tiles,),
        in_specs=in_specs,
        out_specs=pl.BlockSpec((TM, D), lambda t: (t, 0)),
        out_shape=jax.ShapeDtypeStruct((n_tiles * TM, D), F32),
        compiler_params=pltpu.CompilerParams(
            dimension_semantics=("arbitrary",), vmem_limit_bytes=VMEM_LIMIT),
        scratch_shapes=[pltpu.VMEM((D, D), BF16)],
        name="out_mlp%d" % layer,
    )(*args)


def _in1_kernel(x_ref, mod_ref, cos_ref, sin_ref, w_ref, qT_o, k_o, vT_o, wqT_s, wk_s, wvT_s):
    @pl.when(pl.program_id(0) == 0)
    def _():
        wqT_s[...] = _transpose_bf16(w_ref[:, 0:D])
        wk_s[...] = w_ref[:, D:D + ATT_KVW].astype(BF16)
        wvT_s[...] = _transpose_bf16(w_ref[:, D + ATT_KVW:D + 2 * ATT_KVW])

    sh1 = mod_ref[:, 0:D]
    sc1 = mod_ref[:, D:2 * D]
    h = (x_ref[...] * (1.0 + sc1) + sh1).astype(BF16)
    nsl = TM // SP

    cos = cos_ref[...]
    sin = sin_ref[...]
    first = (lax.broadcasted_iota(jnp.int32, (TM, LANES), 1) % (2 * ROT)) < ROT
    sa = jnp.where(first, -sin, 0.0)
    sb = jnp.where(first, 0.0, sin)

    k = _dot(h, wk_s[...])
    for i in range(ATT_KVW // LANES):
        u = k[:, i * LANES:(i + 1) * LANES]
        r = u * cos + pltpu.roll(u, LANES - ROT, 1) * sa + pltpu.roll(u, ROT, 1) * sb
        k_o[:, i * LANES:(i + 1) * LANES] = r.astype(BF16)

    cosT, saT, sbT = cos.T, sa.T, sb.T
    qT = _dot_nt(wqT_s[...], h) * (LOG2E * ATT_DH ** -0.5)
    for i in range(D // LANES):
        u = qT[i * LANES:(i + 1) * LANES, :]
        r = (u * cosT + pltpu.roll(u, LANES - ROT, 0) * saT + pltpu.roll(u, ROT, 0) * sbT).astype(BF16)
        for s in range(nsl):
            qT_o[s, i * LANES:(i + 1) * LANES, :] = r[:, s * SP:(s + 1) * SP]

    vT = _dot_nt(wvT_s[...], h).astype(BF16)
    for s in range(nsl):
        vT_o[s] = vT[:, s * SP:(s + 1) * SP]


def _in_proj1(xs, mods, cos_tab, sin_tab, wqkv):
    tile = lambda t: jnp.where(t == N_LAT_TILES, TILES_PER_BATCH, t % TILES_PER_BATCH)
    row = lambda t: (t, 0)
    slab = lambda t: (t, 0, 0)
    nsl = TM // SP
    tab_spec = pl.BlockSpec((TM, LANES), lambda t: (tile(t), 0))
    return pl.pallas_call(
        _in1_kernel,
        grid=(N_TILES,),
        in_specs=[pl.BlockSpec((TM, D), row), _mod_spec(1), tab_spec, tab_spec,
                  pl.BlockSpec((None,) + wqkv.shape[1:], lambda t: (0, 0, 0), pipeline_mode=pl.Buffered(1))],
        out_specs=[pl.BlockSpec((nsl, D, SP), slab), pl.BlockSpec((TM, ATT_KVW), row),
                   pl.BlockSpec((nsl, ATT_KVW, SP), slab)],
        out_shape=[jax.ShapeDtypeStruct((N_SLABS, D, SP), BF16), jax.ShapeDtypeStruct((ROWS, ATT_KVW), BF16),
                   jax.ShapeDtypeStruct((N_SLABS, ATT_KVW, SP), BF16)],
        scratch_shapes=[pltpu.VMEM((D, D), BF16), pltpu.VMEM((D, ATT_KVW), BF16), pltpu.VMEM((ATT_KVW, D), BF16)],
        compiler_params=pltpu.CompilerParams(
            dimension_semantics=("arbitrary",), vmem_limit_bytes=VMEM_LIMIT),
        name="in_proj1",
    )(xs, mods, cos_tab, sin_tab, wqkv)


def _attn_kernel(sink_ref, qT_ref, k_ref, vT_ref, kc_ref, vcT_ref, o_ref, bias_ref):
    n = pl.program_id(1)
    nql = ATT_GROUP * QB
    lane_g = lax.broadcasted_iota(jnp.int32, (1, nql), 1) // QB
    zero_half = jnp.zeros((ATT_DH, nql), BF16)
    ones_rows = jnp.ones((BF16_ROWS, KWIN + L), BF16)

    @pl.when((pl.program_id(0) == 0) & (n == 0))
    def _():
        kj = lax.broadcasted_iota(jnp.int32, (KWIN, QB), 0)
        qi = lax.broadcasted_iota(jnp.int32, (KWIN, QB), 1)
        for case, delta in enumerate((-QB, 0, -2 * QB)):
            d = kj - qi + delta
            bias_ref[case] = jnp.where((d >= -WINDOW) & (d <= WINDOW), 0.0, -jnp.inf)

    def scores(i, kh):
        n0 = (n * (TQ // QB) + i) * QB
        start = pl.multiple_of(jnp.clip(n0 - QB, 0, T - KWIN), QB)
        bias1 = bias_ref[jnp.where(n0 == 0, 1, jnp.where(n0 == T - QB, 2, 0))]
        bias = jnp.concatenate([bias1] * ATT_GROUP, axis=1)
        pair = slice((kh // 2) * LANES, (kh // 2 + 1) * LANES)
        qT = jnp.concatenate(
            [qT_ref[i, (kh * ATT_GROUP + g) * ATT_DH:(kh * ATT_GROUP + g + 1) * ATT_DH, :]
             for g in range(ATT_GROUP)], axis=1)
        q_pad = jnp.concatenate([zero_half, qT] if kh % 2 else [qT, zero_half], axis=0)
        s_loc = _dot(k_ref[pl.ds(start, KWIN), pair], q_pad) + bias
        s_ctx = _dot(kc_ref[:, pair], q_pad)
        return s_loc, s_ctx, start // SP

    def finish(i, kh, s_loc, s_ctx, slab0):
        drows = slice(kh * ATT_DH, (kh + 1) * ATT_DH)
        sink = jnp.zeros((1, nql), F32)
        for g in range(ATT_GROUP):
            sink = jnp.where(lane_g == g, sink_ref[0, kh * ATT_GROUP + g] * LOG2E, sink)
        m = jnp.maximum(jnp.maximum(jnp.max(s_loc, axis=0, keepdims=True),
                                    jnp.max(s_ctx, axis=0, keepdims=True)), sink)
        pT = jnp.concatenate([jnp.exp2(s_loc - m).astype(BF16),
                              jnp.exp2(s_ctx - m).astype(BF16)], axis=0)
        vT = jnp.concatenate([vT_ref[slab0 + t, drows, :] for t in range(KWIN // SP)]
                             + [vcT_ref[t, drows, :] for t in range(L // SP)], axis=1)
        o_ext = _dot(jnp.concatenate([vT, ones_rows], axis=0), pT)
        denom = o_ext[ATT_DH:ATT_DH + 1, :] + jnp.exp2(sink - m)
        oT = o_ext[0:ATT_DH, :] / denom
        for g in range(0, ATT_GROUP, 2):
            two = jnp.concatenate([oT[:, g * QB:(g + 1) * QB], oT[:, (g + 1) * QB:(g + 2) * QB]], axis=0)
            c0 = (kh * ATT_GROUP + g) * ATT_DH
            o_ref[i * QB:(i + 1) * QB, c0:c0 + 2 * ATT_DH] = two.T.astype(BF16)

    items = [(i, kh) for i in range(TQ // QB) for kh in range(ATT_KVH)]
    pending = [scores(*it) for it in items[:SCORE_AHEAD]]
    for t, item in enumerate(items):
        if t + SCORE_AHEAD < len(items):
            pending.append(scores(*items[t + SCORE_AHEAD]))
        finish(*item, *pending.pop(0))


def _attention(sink, qT, k, vT):
    nq = T // TQ
    return pl.pallas_call(
        _attn_kernel,
        grid=(B, nq),
        in_specs=[
            pl.BlockSpec(memory_space=pltpu.SMEM),
            pl.BlockSpec((TQ // SP, D, SP), lambda b, n: (b * nq + n, 0, 0)),
            pl.BlockSpec((T, ATT_KVW), lambda b, n: (b, 0)),
            pl.BlockSpec((T // SP, ATT_KVW, SP), lambda b, n: (b, 0, 0)),
            pl.BlockSpec((L, ATT_KVW), lambda b, n: (LAT_ROWS // L + b, 0)),
            pl.BlockSpec((L // SP, ATT_KVW, SP), lambda b, n: (LAT_ROWS // L + b, 0, 0)),
        ],
        out_specs=pl.BlockSpec((TQ, D), lambda b, n: (b * nq + n, 0)),
        out_shape=jax.ShapeDtypeStruct((LAT_ROWS, D), BF16),
        compiler_params=pltpu.CompilerParams(
            dimension_semantics=("arbitrary", "arbitrary"), vmem_limit_bytes=VMEM_LIMIT),
        scratch_shapes=[pltpu.VMEM((3, KWIN, QB), F32)],
        name="window_attn",
    )(sink, qT, k, vT, k, vT)


def _rope_tables():
    half = ATT_DH // 2
    inv_freq = np.power(np.float32(ROPE_BASE), -np.arange(0, half, 2, dtype=np.float32) / np.float32(half))
    inv_freq = inv_freq.astype(np.float32)
    pos = np.arange(T)
    ang_r = (pos // GRID_W).astype(np.float32)[:, None] * inv_freq[None, :]
    ang_c = (pos % GRID_W).astype(np.float32)[:, None] * inv_freq[None, :]
    ang = np.concatenate([ang_r, ang_r, ang_c, ang_c], axis=1)
    ang = np.concatenate([ang, np.zeros((TM, ATT_DH), np.float32)], axis=0)
    ang = np.concatenate([ang, ang], axis=1)
    return np.cos(ang).astype(np.float32), np.sin(ang).astype(np.float32)


def kernel(x, c, ctx, c_ctx, w_mod, b_mod, ln_g, ln_b, mlp_w1, mlp_w2, ev_w_in, ev_ret_theta, ev_gla_gk_w,
           ev_gla_gk_b, ev_gla_norm_g, ev_w_out, od_w_qkv, od_sink, od_w_out):
    x2 = x.reshape(LAT_ROWS, D)
    ctx2 = ctx.reshape(B * L, D)

    cs = jnp.concatenate([c, c_ctx[None, :], jnp.zeros((MOD_ROWS - B - 1, D), F32)], axis=0)
    mods = _modulation(cs, w_mod, b_mod).reshape(DEPTH * MOD_ROWS, 1, 6 * D)
    ln = (ln_g.reshape(2 * DEPTH, D), ln_b.reshape(2 * DEPTH, D))

    gk_w = ev_gla_gk_w[0]
    zeros = jnp.zeros((GATE_RANK, QK_W), F32)
    g2 = jnp.concatenate([jnp.concatenate([gk_w[0], zeros], axis=1),
                          jnp.concatenate([zeros, gk_w[1]], axis=1)], axis=0)
    qa, kaT, va, ga, qb, kbT, vb, gbv, lfT, lbT = _in_proj0(
        x2, ctx2, mods, ev_w_in[0].T, g2.T.astype(BF16), ev_gla_gk_b[0].reshape(2 * QK_W, 1))

    w1_rows = DEPTH * D // SCAN_STEPS
    w2_rows = DEPTH * D_FF // SCAN_STEPS
    y_ret, yc_ret, w1 = _scan_group(qa, kaT, va, ga, (ev_ret_theta,),
                                    mlp_w1.reshape(SCAN_STEPS, w1_rows, D_FF), gla=False)
    y_gla, yc_gla, w2 = _scan_group(qb, kbT, vb, gbv, (lfT, lbT, ev_gla_norm_g),
                                    mlp_w2.reshape(SCAN_STEPS, w2_rows, D), gla=True)
    w1 = w1.reshape(DEPTH, D, D_FF)
    w2 = w2.reshape(DEPTH, D_FF, D)

    xs = _out_mlp((y_ret, y_gla, yc_ret, yc_gla), x2, ctx2, mods, ln, ev_w_out, w1, w2,
                  layer=0, n_tiles=N_TILES, split_ctx=True)

    cos_tab, sin_tab = _rope_tables()
    q1T, k1, v1T = _in_proj1(xs, mods, cos_tab, sin_tab, od_w_qkv)
    att = _attention(od_sink, q1T, k1, v1T)
    out = _out_mlp((att,), xs, None, mods, ln, od_w_out, w1, w2,
                   layer=1, n_tiles=N_LAT_TILES, split_ctx=False)
    return out.reshape(B, T, D)
```
